```python
import math
import jax, jax.numpy as jnp
from jax import lax
import numpy as np

D_MODEL = 2048
BATCH = 2
SEQ = 4096
DEPTH = 4
DEC_BATCH = 8
DEC_SEQ = 8
PAST_LEN = 16384
PAGE_SIZE = 128

HEAD_DIM = 128
NSA_WIDTH = D_MODEL // 2
SB_WIDTH = D_MODEL // 4
SSM_WIDTH = D_MODEL // 4
MIX_WIDTH = NSA_WIDTH + SB_WIDTH + SSM_WIDTH
NSA_HEADS = NSA_WIDTH // HEAD_DIM
NSA_KV_HEADS = 2
NSA_GROUP = NSA_HEADS // NSA_KV_HEADS
CMP_BLOCK = 32
CMP_STRIDE = 16
SEL_BLOCK = 64
N_SEL = 16
WINDOW = 512
SB_HEADS = SB_WIDTH // HEAD_DIM
SSM_CH = 16
SSM_GROUPS = SSM_WIDTH // SSM_CH
SSM_STATE = 64
NUM_BUCKETS = 32
MAX_DISTANCE = 128
Q_BLOCK = 128
DN_ALPHA = (2 * DEPTH) ** 0.25
DN_BETA = (8 * DEPTH) ** -0.25
FORCE_SCORE = 1e4
EPS = 1e-5
IN_SIZES = (NSA_WIDTH, 6 * NSA_KV_HEADS * HEAD_DIM, 3 * NSA_HEADS, NSA_WIDTH,
            SB_WIDTH, SB_WIDTH, SB_WIDTH, SB_WIDTH, SSM_WIDTH, SSM_WIDTH)
IN_WIDTH = sum(IN_SIZES)

kernel_name = 'hymba_nsa_sb_s5_decoder_step'


def t5_bucket(dist):
    n = jnp.maximum(dist, 0)
    exact = NUM_BUCKETS // 2
    nf = jnp.maximum(n, 1).astype(jnp.float32)
    large = exact + (jnp.log(nf / exact) / math.log(MAX_DISTANCE / exact) * (NUM_BUCKETS - exact)).astype(jnp.int32)
    return jnp.where(n < exact, n, jnp.minimum(large, NUM_BUCKETS - 1))


def masked_softmax(s, valid):
    s = jnp.where(valid, s, -jnp.inf)
    m = jnp.max(s, axis=-1, keepdims=True)
    m = jnp.where(jnp.isfinite(m), m, 0.0)
    e = jnp.where(valid, jnp.exp(s - m), 0.0)
    return e / jnp.maximum(jnp.sum(e, axis=-1, keepdims=True), 1e-30)


def blockwise(fn, xs, axes):
    n_q = xs[0].shape[axes[0]]
    if n_q <= Q_BLOCK or n_q % Q_BLOCK:
        return fn(*xs)
    nb = n_q // Q_BLOCK

    def split(a, ax):
        a = jnp.moveaxis(a, ax, 0)
        return a.reshape((nb, Q_BLOCK) + a.shape[1:])

    blocks = tuple(split(a, ax) for a, ax in zip(xs, axes))

    def body(blk):
        return fn(*[jnp.moveaxis(a, 0, ax) for a, ax in zip(blk, axes)])

    out = jnp.moveaxis(lax.map(body, blocks), 0, 1)
    return out.reshape((out.shape[0], n_q) + out.shape[3:])


def gqa_attend(q, k, v, q_pos, k_pos, valid, rel_bias):
    s = jnp.einsum('btgrd,bngd->bgrtn', q, k).astype(jnp.float32) * HEAD_DIM ** -0.5
    bias = rel_bias[t5_bucket(q_pos[:, None] - k_pos[None, :])]
    bias = jnp.transpose(bias, (2, 0, 1)).reshape(NSA_KV_HEADS, NSA_GROUP, q_pos.shape[0], k_pos.shape[0])
    p = masked_softmax(s + bias.astype(jnp.float32), valid)
    return jnp.einsum('bgrtn,bngd->btgrd', p.astype(v.dtype), v), p


def window_valid(q_pos, k_pos):
    dq = q_pos[:, None] - k_pos[None, :]
    return (dq >= 0) & (dq < WINDOW) & (k_pos[None, :] >= 0)


def compress(k, w):
    b, l, g, d = k.shape
    nc = l // CMP_STRIDE
    kc = k[:, :nc * CMP_STRIDE].reshape(b, nc, CMP_STRIDE, g, d)
    return (jnp.einsum('bnjgd,jd->bngd', kc[:, :-1], w[:CMP_STRIDE])
            + jnp.einsum('bnjgd,jd->bngd', kc[:, 1:], w[CMP_STRIDE:]))


def sel_attend(q, idx, q_pos, kb, vb, rel_bias):
    b = q.shape[0]
    bi = jnp.arange(b)[:, None, None, None]
    gi = jnp.arange(NSA_KV_HEADS)[None, :, None, None]
    kg = kb[bi, gi, idx]
    vg = vb[bi, gi, idx]
    s = jnp.einsum('btgrd,bgtkjd->bgrtkj', q, kg).astype(jnp.float32) * HEAD_DIM ** -0.5
    kpos = idx[..., None] * SEL_BLOCK + jnp.arange(SEL_BLOCK)
    dist = q_pos[None, None, :, None, None] - kpos
    tab = rel_bias.reshape(NUM_BUCKETS, NSA_KV_HEADS, NSA_GROUP)
    bias = jnp.moveaxis(tab[t5_bucket(dist), gi[..., None]], -1, 2)
    s = s + bias.astype(jnp.float32)
    shp = s.shape
    valid = (dist >= 0)[:, :, None]
    p = masked_softmax(s.reshape(shp[:4] + (-1,)), valid.reshape(valid.shape[:4] + (-1,))).reshape(shp)
    return jnp.einsum('bgrtkj,bgtkjd->btgrd', p.astype(vg.dtype), vg)


def nsa_cmp_sel(q, rows, q_pos, w_cmp, rel_bias):
    b, l = rows.shape[:2]
    ck = compress(rows[:, :, 0], w_cmp[0])
    cv = compress(rows[:, :, 1], w_cmp[1])
    n_cmp = ck.shape[1]
    cend = jnp.arange(n_cmp) * CMP_STRIDE + CMP_BLOCK - 1
    o_cmp, p_cmp = gqa_attend(q, ck, cv, q_pos, cend, cend[None, :] <= q_pos[:, None], rel_bias)
    cps = SEL_BLOCK // CMP_STRIDE
    n_sel = -(-l // SEL_BLOCK)
    imp = jnp.sum(p_cmp, axis=2)
    imp = jnp.pad(imp, ((0, 0), (0, 0), (0, 0), (0, n_sel * cps - n_cmp)))
    imp = imp.reshape(imp.shape[:3] + (n_sel, cps)).sum(-1)
    blk = jnp.arange(n_sel)[None, :]
    cur = (q_pos // SEL_BLOCK)[:, None]
    forced = (blk == 0) | (blk == cur) | (blk == cur - 1)
    score = jnp.where(forced, FORCE_SCORE, jnp.where(blk * SEL_BLOCK <= q_pos[:, None], imp, -1.0))
    _, idx = lax.top_k(score, min(N_SEL, n_sel))
    pad = n_sel * SEL_BLOCK - l

    def to_blocks(r):
        r = jnp.pad(r, ((0, 0), (0, pad), (0, 0), (0, 0)))
        return r.reshape(b, n_sel, SEL_BLOCK, NSA_KV_HEADS, HEAD_DIM).transpose(0, 3, 1, 2, 4)

    kb, vb = to_blocks(rows[:, :, 2]), to_blocks(rows[:, :, 3])
    o_sel = blockwise(lambda qq, ii, pp: sel_attend(qq, ii, pp, kb, vb, rel_bias), (q, idx, q_pos), (1, 2, 0))
    return o_cmp, o_sel


def window_prompt(q, k, v, q_pos, rel_bias):
    kp = jnp.pad(k, ((0, 0), (WINDOW, 0), (0, 0), (0, 0)))
    vp = jnp.pad(v, ((0, 0), (WINDOW, 0), (0, 0), (0, 0)))

    def fn(qq, pp):
        tq = qq.shape[1]
        start = pp[0]
        kk = lax.dynamic_slice_in_dim(kp, start, WINDOW + tq, axis=1)
        vv = lax.dynamic_slice_in_dim(vp, start, WINDOW + tq, axis=1)
        kpos = start - WINDOW + jnp.arange(WINDOW + tq)
        return gqa_attend(qq, kk, vv, pp, kpos, window_valid(pp, kpos), rel_bias)[0]

    return blockwise(fn, (q, q_pos), (1, 0))


def sb_attend(q, k, v, q_pos, k_pos):
    z = jnp.einsum('bqhd,bkhd->bhqk', q, k).astype(jnp.float32) * HEAD_DIM ** -0.5
    valid = k_pos[None, :] < q_pos[:, None]
    log_1m = jnp.where(valid, jax.nn.log_sigmoid(-z), 0.0)
    rc = lax.cumsum(log_1m, axis=3, reverse=True)
    rc_excl = jnp.concatenate([rc[..., 1:], jnp.zeros_like(rc[..., :1])], axis=3)
    a = jnp.where(valid, jnp.exp(jax.nn.log_sigmoid(z) + rc_excl), 0.0)
    return jnp.einsum('bhqk,bkhd->bqhd', a.astype(v.dtype), v)


def cmul_combine(e1, e2):
    a1r, a1i, b1r, b1i = e1
    a2r, a2i, b2r, b2i = e2
    return (a1r * a2r - a1i * a2i, a1r * a2i + a1i * a2r,
            a2r * b1r - a2i * b1i + b2r, a2r * b1i + a2i * b1r + b2i)


def ssm_scan(u, h0, lam_re, lam_im, log_dt, b_re, b_im, c_re, c_im, d_skip):
    f32 = jnp.float32
    lam_re, lam_im = lam_re.astype(f32), lam_im.astype(f32)
    dt = jnp.exp(log_dt.astype(f32))[:, None]
    mag = jnp.exp(lam_re * dt)
    a_re, a_im = mag * jnp.cos(lam_im * dt), mag * jnp.sin(lam_im * dt)
    den = lam_re * lam_re + lam_im * lam_im
    f_re = ((a_re - 1.0) * lam_re + a_im * lam_im) / den
    f_im = (a_im * lam_re - (a_re - 1.0) * lam_im) / den
    b_re, b_im = b_re.astype(f32), b_im.astype(f32)
    bb_re = f_re[..., None] * b_re - f_im[..., None] * b_im
    bb_im = f_re[..., None] * b_im + f_im[..., None] * b_re
    uf = u.astype(f32)
    x_re = jnp.einsum('btgn,gpn->btgp', uf, bb_re)
    x_im = jnp.einsum('btgn,gpn->btgp', uf, bb_im)
    h0 = h0.astype(f32)
    x_re = x_re.at[:, 0].add(a_re * h0[:, 0] - a_im * h0[:, 1])
    x_im = x_im.at[:, 0].add(a_re * h0[:, 1] + a_im * h0[:, 0])
    elems = (jnp.broadcast_to(a_re, x_re.shape), jnp.broadcast_to(a_im, x_im.shape), x_re, x_im)
    _, _, h_re, h_im = lax.associative_scan(cmul_combine, elems, axis=1)
    y = (jnp.einsum('gnp,btgp->btgn', c_re.astype(f32), h_re)
         - jnp.einsum('gnp,btgp->btgn', c_im.astype(f32), h_im)
         + d_skip.astype(f32).reshape(SSM_GROUPS, SSM_CH) * uf)
    return y.reshape(u.shape[0], u.shape[1], SSM_WIDTH), jnp.stack([h_re[:, -1], h_im[:, -1]], axis=1)


def glu(y, w):
    a, g = jnp.split(y @ w, 2, axis=-1)
    return a * jax.nn.sigmoid(g)


def gated_rms(y, z, g):
    h = (y * jax.nn.silu(z)).astype(jnp.float32)
    h = h * lax.rsqrt(jnp.mean(h * h, axis=-1, keepdims=True) + EPS) * g.astype(jnp.float32)
    return h.astype(y.dtype)


def layer_norm(h, g, b):
    hf = h.astype(jnp.float32)
    mu = jnp.mean(hf, axis=-1, keepdims=True)
    var = jnp.mean((hf - mu) ** 2, axis=-1, keepdims=True)
    return ((hf - mu) * lax.rsqrt(var + EPS) * g.astype(jnp.float32) + b.astype(jnp.float32)).astype(h.dtype)


def trunk_layer(x, pos, rel_bias, lw, past):
    (w_in, w_cmp, lam_re, lam_im, log_dt, b_re, b_im, c_re, c_im, d_skip,
     w_glu, mix_gain, w_out, ln_g, ln_b) = lw
    b, t, _ = x.shape
    split_points = np.cumsum(IN_SIZES)[:-1].tolist()
    q_a, kv_a, g_a, z_a, q_b, k_b, v_b, z_b, u_c, z_c = jnp.split(x @ w_in, split_points, axis=-1)
    q_a = q_a.reshape(b, t, NSA_KV_HEADS, NSA_GROUP, HEAD_DIM)
    kv_a = kv_a.reshape(b, t, 6, NSA_KV_HEADS, HEAD_DIM)
    nsa_new, win_new = kv_a[:, :, :4], kv_a[:, :, 4:]
    sb_new = jnp.stack([k_b, v_b], axis=2).reshape(b, t, 2, SB_HEADS, HEAD_DIM)
    q_b = q_b.reshape(b, t, SB_HEADS, HEAD_DIM)
    if past is None:
        nsa_all, sb_all = nsa_new, sb_new
        h0 = jnp.zeros((b, 2, SSM_GROUPS, SSM_STATE), jnp.float32)
    else:
        nsa_past, sb_past, win_buf, h0 = past
        nsa_all = jnp.concatenate([nsa_past, nsa_new], axis=1)
        sb_all = jnp.concatenate([sb_past, sb_new], axis=1)
    o_cmp, o_sel = nsa_cmp_sel(q_a, nsa_all, pos, w_cmp, rel_bias)
    if past is None:
        o_win = window_prompt(q_a, win_new[:, :, 0], win_new[:, :, 1], pos, rel_bias)
        win_state = win_new[:, max(t - WINDOW, 0):]
    else:
        wb = win_buf.shape[1]
        win_all = jnp.concatenate([win_buf, win_new], axis=1)
        kpos = sb_past.shape[1] - wb + jnp.arange(wb + t)
        o_win, _ = gqa_attend(q_a, win_all[:, :, 0], win_all[:, :, 1], pos, kpos, window_valid(pos, kpos), rel_bias)
        win_state = win_all[:, t:]
    gates = jax.nn.sigmoid(g_a.reshape(b, t, NSA_KV_HEADS, NSA_GROUP, 3))
    o_a = (gates[..., 0:1] * o_cmp + gates[..., 1:2] * o_sel + gates[..., 2:3] * o_win).reshape(b, t, NSA_WIDTH)
    kpos_b = jnp.arange(sb_all.shape[1])
    o_b = blockwise(lambda qq, pp: sb_attend(qq, sb_all[:, :, 0], sb_all[:, :, 1], pp, kpos_b),
                    (q_b, pos), (1, 0)).reshape(b, t, SB_WIDTH)
    y_c, h_last = ssm_scan(u_c.reshape(b, t, SSM_GROUPS, SSM_CH), h0, lam_re, lam_im, log_dt,
                           b_re, b_im, c_re, c_im, d_skip)
    y_c = glu(y_c.astype(x.dtype), w_glu)
    mixed = jnp.concatenate([
        gated_rms(o_a, z_a, mix_gain[:NSA_WIDTH]),
        gated_rms(o_b, z_b, mix_gain[NSA_WIDTH:NSA_WIDTH + SB_WIDTH]),
        gated_rms(y_c, z_c, mix_gain[NSA_WIDTH + SB_WIDTH:])], axis=-1)
    y = layer_norm(DN_ALPHA * x + mixed @ w_out, ln_g, ln_b)
    return y, nsa_new, sb_new, win_state, h_last


def setup_inputs(seed: int = 0) -> dict:
    key = jax.random.key(seed)
    ks = jax.random.split(key, 24)
    f32 = jnp.float32
    n_pages = PAST_LEN // PAGE_SIZE
    n_used = DEC_BATCH * n_pages
    n_phys = n_used + (n_used + 3) // 4

    def nrm(k, shape, s=1.0):
        return s * jax.random.normal(k, shape, f32)

    return {
        'x_prompt': nrm(ks[0], (BATCH, SEQ, D_MODEL)),
        'x_sample': nrm(ks[1], (DEC_BATCH, DEC_SEQ, D_MODEL)),
        'cache_nsa': nrm(ks[2], (DEPTH, n_phys, PAGE_SIZE, 4, NSA_KV_HEADS, HEAD_DIM)),
        'cache_sb': nrm(ks[3], (DEPTH, n_phys, PAGE_SIZE, 2, SB_HEADS, HEAD_DIM)),
        'state_win': nrm(ks[4], (DEPTH, DEC_BATCH, min(WINDOW, PAST_LEN), 2, NSA_KV_HEADS, HEAD_DIM)),
        'state_ssm': nrm(ks[5], (DEPTH, DEC_BATCH, 2, SSM_GROUPS, SSM_STATE)),
        'page_table': jax.random.permutation(ks[6], n_phys)[:n_used].reshape(DEC_BATCH, n_pages).astype(jnp.int32),
        'rel_bias': nrm(ks[7], (NUM_BUCKETS, NSA_HEADS), 0.5),
        'w_in': nrm(ks[8], (DEPTH, D_MODEL, IN_WIDTH), D_MODEL ** -0.5),
        'w_cmp': (1.0 + nrm(ks[9], (DEPTH, 2, CMP_BLOCK, HEAD_DIM), 0.1)) * CMP_BLOCK ** -0.5,
        'ssm_lam_re': -0.5 + nrm(ks[10], (DEPTH, SSM_GROUPS, SSM_STATE), 0.01),
        'ssm_lam_im': math.pi * jnp.arange(SSM_STATE, dtype=f32) + nrm(ks[11], (DEPTH, SSM_GROUPS, SSM_STATE), 0.01),
        'ssm_log_dt': jax.random.uniform(ks[12], (DEPTH, SSM_GROUPS), f32, math.log(1e-3), math.log(1e-1)),
        'ssm_b_re': nrm(ks[13], (DEPTH, SSM_GROUPS, SSM_STATE, SSM_CH), (2 * SSM_CH) ** -0.5),
        'ssm_b_im': nrm(ks[14], (DEPTH, SSM_GROUPS, SSM_STATE, SSM_CH), (2 * SSM_CH) ** -0.5),
        'ssm_c_re': nrm(ks[15], (DEPTH, SSM_GROUPS, SSM_CH, SSM_STATE), (2 * SSM_STATE) ** -0.5),
        'ssm_c_im': nrm(ks[16], (DEPTH, SSM_GROUPS, SSM_CH, SSM_STATE), (2 * SSM_STATE) ** -0.5),
        'ssm_d': nrm(ks[17], (DEPTH, SSM_WIDTH)),
        'w_glu': nrm(ks[18], (DEPTH, SSM_WIDTH, 2 * SSM_WIDTH), SSM_WIDTH ** -0.5),
        'mix_gain': 1.0 + nrm(ks[19], (DEPTH, MIX_WIDTH), 0.01),
        'w_out': nrm(ks[20], (DEPTH, MIX_WIDTH, D_MODEL), DN_BETA * MIX_WIDTH ** -0.5),
        'ln_g': 1.0 + nrm(ks[21], (DEPTH, D_MODEL), 0.01),
        'ln_b': nrm(ks[22], (DEPTH, D_MODEL), 0.01),
    }


def reference(x_prompt, x_sample, cache_nsa, cache_sb, state_win, state_ssm, page_table, rel_bias,
              w_in, w_cmp, ssm_lam_re, ssm_lam_im, ssm_log_dt, ssm_b_re, ssm_b_im, ssm_c_re, ssm_c_im,
              ssm_d, w_glu, mix_gain, w_out, ln_g, ln_b):
    db = x_sample.shape[0]
    past_len = page_table.shape[1] * cache_nsa.shape[2]
    pos_p = jnp.arange(x_prompt.shape[1])
    pos_s = past_len + jnp.arange(x_sample.shape[1])
    yp, ys = x_prompt, x_sample
    outs_p, outs_s = [], []
    for l in range(DEPTH):
        lw = (w_in[l], w_cmp[l], ssm_lam_re[l], ssm_lam_im[l], ssm_log_dt[l], ssm_b_re[l], ssm_b_im[l],
              ssm_c_re[l], ssm_c_im[l], ssm_d[l], w_glu[l], mix_gain[l], w_out[l], ln_g[l], ln_b[l])
        yp, *sp = trunk_layer(yp, pos_p, rel_bias, lw, None)
        nsa_past = cache_nsa[l][page_table].reshape((db, past_len) + cache_nsa.shape[3:])
        sb_past = cache_sb[l][page_table].reshape((db, past_len) + cache_sb.shape[3:])
        ys, *ss = trunk_layer(ys, pos_s, rel_bias, lw, (nsa_past, sb_past, state_win[l], state_ssm[l]))
        outs_p.append(sp)
        outs_s.append(ss)
    nsa_p = jnp.stack([o[0] for o in outs_p])
    nsa_s = jnp.stack([o[0] for o in outs_s])
    sb_p = jnp.stack([o[1] for o in outs_p])
    sb_s = jnp.stack([o[1] for o in outs_s])
    win_p = jnp.stack([o[2] for o in outs_p])
    win_s = jnp.stack([o[2] for o in outs_s])
    ssm_p = jnp.stack([o[3] for o in outs_p])
    ssm_s = jnp.stack([o[3] for o in outs_s])
    return (yp, ys, nsa_p, nsa_s, sb_p, sb_s, win_p, win_s, ssm_p, ssm_s)
```

```python
import functools
import math

import jax
import jax.numpy as jnp
from jax import lax
from jax.experimental import pallas as pl
from jax.experimental.pallas import tpu as pltpu

F32 = jnp.float32
BF16 = jnp.bfloat16

HEAD_DIM = 128
NSA_KV_HEADS = 2
NSA_GROUP = 4
NSA_HEADS = NSA_KV_HEADS * NSA_GROUP
NSA_WIDTH = NSA_HEADS * HEAD_DIM
SB_HEADS = 4
SB_WIDTH = SB_HEADS * HEAD_DIM
SSM_CH = 16
SSM_GROUPS = 32
SSM_STATE = 64
SSM_WIDTH = SSM_CH * SSM_GROUPS
SSM_LANES = SSM_GROUPS * SSM_STATE
CMP_BLOCK = 32
CMP_STRIDE = 16
SEL_BLOCK = 64
CMP_PER_SEL = SEL_BLOCK // CMP_STRIDE
N_SEL = 16
WINDOW = 512
NUM_BUCKETS = 32
MAX_DISTANCE = 128
FORCE_SCORE = 1e4
EPS = 1e-5
SCALE = HEAD_DIM ** -0.5
NEG = -1e30

LANES = 128
SUBLANES = 8
VMEM_LIMIT = 48 * 1024 * 1024

TQ = 256
TK = 256
PAGES_PER_STEP = 8
SSM_TILE = 512
SSM_TILES = SSM_LANES // SSM_TILE
SSM_U_TILE = SSM_WIDTH // SSM_TILES


def _cparams(sem):
    return pltpu.CompilerParams(dimension_semantics=sem, vmem_limit_bytes=VMEM_LIMIT)


def _dot(a, b):
    return jnp.dot(a, b, preferred_element_type=F32)


def _dot_nt(a, b):
    return lax.dot_general(a, b, (((1,), (1,)), ((), ())), preferred_element_type=F32)


def _split_dot(x, w):
    hi = x.astype(BF16)
    lo = (x - hi.astype(F32)).astype(BF16)
    return _dot(hi, w) + _dot(lo, w)


def _sigmoid(x):
    return 1.0 / (1.0 + jnp.exp(-x))


def _lane_tile(x, n):
    return x if n == 1 else jnp.concatenate([x] * n, axis=1)


def _mm_kernel(x_ref, w_ref, *o_refs):
    acc = _dot(x_ref[...], w_ref[...])
    for o in o_refs:
        o[...] = acc.astype(o.dtype)


def _matmul(x, w, out_dtypes, name):
    m, k = x.shape
    n = w.shape[1]
    tm = min(m, 1024)
    tn = min(n, 512)
    outs = pl.pallas_call(
        _mm_kernel,
        grid=(m // tm, n // tn),
        in_specs=[pl.BlockSpec((tm, k), lambda i, j: (i, 0)),
                  pl.BlockSpec((k, tn), lambda i, j: (0, j))],
        out_specs=[pl.BlockSpec((tm, tn), lambda i, j: (i, j)) for _ in out_dtypes],
        out_shape=[jax.ShapeDtypeStruct((m, n), d) for d in out_dtypes],
        compiler_params=_cparams(("parallel", "parallel")),
        name=name,
    )(x, w)
    return outs


def _masked_softmax(s, valid):
    s = jnp.where(valid, s, NEG)
    m = jnp.max(s, axis=1, keepdims=True)
    e = jnp.where(valid, jnp.exp(s - m), 0.0)
    return e / jnp.maximum(jnp.sum(e, axis=1, keepdims=True), 1e-30)


def _cmp_attend(qg, ck, cv, bias, rows):
    s = _dot_nt(qg, ck) * SCALE + bias
    p = _masked_softmax(s, bias > 0.5 * NEG)
    o = _dot(p.astype(BF16), cv)
    imp = p[0:rows] + p[rows:2 * rows] + p[2 * rows:3 * rows] + p[3 * rows:4 * rows]
    nc = ck.shape[0]
    nb = nc // CMP_PER_SEL
    grp = (lax.broadcasted_iota(jnp.int32, (nc, nb), 0) // CMP_PER_SEL
           == lax.broadcasted_iota(jnp.int32, (nc, nb), 1))
    return o, _split_dot(imp, jnp.where(grp, 1.0, 0.0).astype(BF16))


def _select_blocks(imp, pos, k_top):
    rows, nb = imp.shape
    blk = lax.broadcasted_iota(jnp.int32, (rows, nb), 1)
    cur = pos // SEL_BLOCK
    score = jnp.where(blk * SEL_BLOCK <= pos, imp, -1.0)
    score = jnp.where(blk == 0, FORCE_SCORE, score)
    score = jnp.where(blk == cur, FORCE_SCORE, score)
    score = jnp.where(blk == cur - 1, FORCE_SCORE, score)
    cnt = jnp.zeros((rows, nb), F32)
    for i in range(nb):
        col = score[:, i:i + 1]
        ge = jnp.where(col >= score, 1.0, 0.0)
        gt = jnp.where(col > score, 1.0, 0.0)
        cnt = cnt + jnp.where(blk > i, ge, gt)
    return jnp.where(cnt < k_top, 1.0, 0.0)


def _flash_init(m_ref, l_ref, acc_ref):
    m_ref[...] = jnp.full(m_ref.shape, NEG, F32)
    l_ref[...] = jnp.zeros(l_ref.shape, F32)
    acc_ref[...] = jnp.zeros(acc_ref.shape, F32)


def _flash_tile(s, v, m_ref, l_ref, acc_ref):
    m_prev = m_ref[...]
    m_new = jnp.maximum(m_prev, jnp.max(s, axis=1, keepdims=True))
    alpha = jnp.exp(m_prev - m_new)
    p = jnp.exp(s - m_new[:, 0:1])
    l_ref[...] = alpha * l_ref[...] + jnp.sum(p, axis=1, keepdims=True)
    acc_ref[...] = alpha * acc_ref[...] + _dot(p.astype(BF16), v)
    m_ref[...] = m_new


def _pad_keys(x):
    pad = jnp.zeros((LANES - x.shape[0], x.shape[1]), F32)
    return jnp.concatenate([x, pad], axis=0).astype(BF16)


def _stack_heads(q_ref_slice, width):
    return jnp.concatenate([q_ref_slice[:, HEAD_DIM * r:HEAD_DIM * (r + 1)] for r in range(NSA_GROUP)], axis=0)


def _log_sigmoid(z):
    return jnp.minimum(z, 0.0) - jnp.log1p(jnp.exp(-jnp.abs(z)))


def _strict_upper(n):
    return jnp.where(lax.broadcasted_iota(jnp.int32, (n, n), 0) > lax.broadcasted_iota(jnp.int32, (n, n), 1),
                     1.0, 0.0).astype(BF16)


def _prompt_cmp_kernel(q_ref, k0_ref, k1_ref, v0_ref, v1_ref, w_ref, bias_ref, o_ref, sel_ref, ck_ref, cv_ref,
                       *, tq, nc):
    qt = pl.program_id(1)

    @pl.when(qt == 0)
    def _():
        for g in range(NSA_KV_HEADS):
            for kv, src, dst in ((0, (k0_ref, k1_ref)[g], ck_ref), (1, (v0_ref, v1_ref)[g], cv_ref)):
                first = jnp.zeros((nc, HEAD_DIM), F32)
                second = jnp.zeros((nc, HEAD_DIM), F32)
                for j in range(CMP_STRIDE):
                    rows = src[0, pl.ds(j, nc, stride=CMP_STRIDE), :]
                    first = first + rows * w_ref[kv, j:j + 1, :]
                    second = second + rows * w_ref[kv, CMP_STRIDE + j:CMP_STRIDE + j + 1, :]
                dst[g] = (first + pltpu.roll(second, nc - 1, 0)).astype(BF16)

    pos = qt * tq + lax.broadcasted_iota(jnp.int32, (tq, nc // CMP_PER_SEL), 0)
    for g in range(NSA_KV_HEADS):
        qg = _stack_heads(q_ref[0, :, pl.ds(g * NSA_GROUP * HEAD_DIM, NSA_GROUP * HEAD_DIM)], HEAD_DIM)
        bias = bias_ref[NSA_GROUP * g:NSA_GROUP * (g + 1)].reshape(NSA_GROUP * tq, nc)
        o, imp = _cmp_attend(qg, ck_ref[g], cv_ref[g], bias, tq)
        for r in range(NSA_GROUP):
            h = NSA_GROUP * g + r
            o_ref[0, :, h * HEAD_DIM:(h + 1) * HEAD_DIM] = o[r * tq:(r + 1) * tq]
        sel_ref[0, g] = _select_blocks(imp, pos, N_SEL)


def _prompt_cmp(q, kvf, w_cmp, bias_cmp, b, t):
    nc = t // CMP_STRIDE
    nb = nc // CMP_PER_SEL
    q3 = q.reshape(b, t, q.shape[-1])
    kv3 = kvf.reshape(b, t, kvf.shape[-1])
    return pl.pallas_call(
        functools.partial(_prompt_cmp_kernel, tq=TQ, nc=nc),
        grid=(b, t // TQ),
        in_specs=[pl.BlockSpec((1, TQ, NSA_WIDTH), lambda i, j: (i, j, 0)),
                  *[pl.BlockSpec((1, t, HEAD_DIM), lambda i, j, c=c: (i, 0, c)) for c in range(4)],
                  pl.BlockSpec((2, CMP_BLOCK, HEAD_DIM), lambda i, j: (0, 0, 0)),
                  pl.BlockSpec((NSA_HEADS, TQ, nc), lambda i, j: (0, j, 0))],
        out_specs=[pl.BlockSpec((1, TQ, NSA_WIDTH), lambda i, j: (i, j, 0)),
                   pl.BlockSpec((1, NSA_KV_HEADS, TQ, nb), lambda i, j: (i, 0, j, 0))],
        out_shape=[jax.ShapeDtypeStruct((b, t, NSA_WIDTH), F32),
                   jax.ShapeDtypeStruct((b, NSA_KV_HEADS, t, nb), F32)],
        scratch_shapes=[pltpu.VMEM((NSA_KV_HEADS, nc, HEAD_DIM), BF16),
                        pltpu.VMEM((NSA_KV_HEADS, nc, HEAD_DIM), BF16)],
        compiler_params=_cparams(("parallel", "arbitrary")),
        name="prompt_cmp",
    )(q3, kv3, kv3, kv3, kv3, w_cmp, bias_cmp)


def _prompt_selwin_kernel(q_ref, ks_ref, vs_ref, kw_ref, vw_ref, sel_ref, tb_ref, osel_ref, owin_ref,
                          m_ref, l_ref, acc_ref, *, tq, tk):
    i = pl.program_id(2)
    rows = NSA_GROUP * tq
    qg = _stack_heads(q_ref[0], HEAD_DIM)
    selb = sel_ref[0, 0].astype(BF16)
    nb = selb.shape[1]
    blocks_per_tile = tk // SEL_BLOCK
    erow = lax.broadcasted_iota(jnp.int32, (nb, tk), 0)
    ecol = lax.broadcasted_iota(jnp.int32, (nb, tk), 1) // SEL_BLOCK

    def scores(k_ref, j, table):
        k = k_ref[0, pl.ds(pl.multiple_of(j * tk, tk), tk), :]
        return _dot_nt(qg, k) * SCALE + tb_ref[table].reshape(rows, tk)

    def sel_tile(j, table):
        s = scores(ks_ref, j, table)
        expand = jnp.where(erow == j * blocks_per_tile + ecol, 1.0, 0.0).astype(BF16)
        chosen = _dot(selb, expand) > 0.5
        s = jnp.where(chosen[None], s.reshape(NSA_GROUP, tq, tk), NEG).reshape(rows, tk)
        v = vs_ref[0, pl.ds(pl.multiple_of(j * tk, tk), tk), :]
        _flash_tile(s, v, m_ref, l_ref, acc_ref)

    def win_tile(j, table):
        s = scores(kw_ref, j, table)
        v = vw_ref[0, pl.ds(pl.multiple_of(j * tk, tk), tk), :]
        _flash_tile(s, v, m_ref, l_ref, acc_ref)

    def finish(o_ref):
        o = acc_ref[...] / l_ref[...]
        for r in range(NSA_GROUP):
            o_ref[0, :, r * HEAD_DIM:(r + 1) * HEAD_DIM] = o[r * tq:(r + 1) * tq]

    _flash_init(m_ref, l_ref, acc_ref)

    def far_body(j, carry):
        sel_tile(j, 2)
        return carry

    lax.fori_loop(0, jnp.maximum(i - 1, 0), far_body, 0)

    @pl.when(i >= 1)
    def _():
        sel_tile(i - 1, 1)

    sel_tile(i, 0)
    finish(osel_ref)

    _flash_init(m_ref, l_ref, acc_ref)
    win_tile(i, 0)

    @pl.when(i >= 1)
    def _():
        win_tile(i - 1, 1)

    @pl.when(i >= 2)
    def _():
        win_tile(i - 2, 3)

    finish(owin_ref)


def _prompt_selwin(q, kvb, sel, tables, b, t):
    nb = sel.shape[-1]
    q3 = q.reshape(b, t, q.shape[-1])
    kv3 = kvb.reshape(b, t, kvb.shape[-1])
    gw = NSA_GROUP * HEAD_DIM
    kcol = lambda base: pl.BlockSpec((1, t, HEAD_DIM), lambda i, g, j: (i, 0, base + g))
    return pl.pallas_call(
        functools.partial(_prompt_selwin_kernel, tq=TQ, tk=TK),
        grid=(b, NSA_KV_HEADS, t // TQ),
        in_specs=[pl.BlockSpec((1, TQ, gw), lambda i, g, j: (i, j, g)),
                  kcol(4), kcol(6), kcol(8), kcol(10),
                  pl.BlockSpec((1, 1, TQ, nb), lambda i, g, j: (i, g, j, 0)),
                  pl.BlockSpec((4, NSA_GROUP, TQ, TK), lambda i, g, j: (0, g, 0, 0))],
        out_specs=[pl.BlockSpec((1, TQ, gw), lambda i, g, j: (i, j, g)),
                   pl.BlockSpec((1, TQ, gw), lambda i, g, j: (i, j, g))],
        out_shape=[jax.ShapeDtypeStruct((b, t, NSA_WIDTH), F32),
                   jax.ShapeDtypeStruct((b, t, NSA_WIDTH), F32)],
        scratch_shapes=[pltpu.VMEM((NSA_GROUP * TQ, HEAD_DIM), F32),
                        pltpu.VMEM((NSA_GROUP * TQ, HEAD_DIM), F32),
                        pltpu.VMEM((NSA_GROUP * TQ, HEAD_DIM), F32)],
        compiler_params=_cparams(("parallel", "parallel", "parallel")),
        name="prompt_selwin",
    )(q3, kv3, kv3, kv3, kv3, sel, tables)


def _sb_tile(q, k, v, carry, upper, diag):
    z = _dot_nt(q, k) * SCALE
    ls = _log_sigmoid(z)
    l1m = ls - z
    if diag:
        valid = (lax.broadcasted_iota(jnp.int32, z.shape, 1) < lax.broadcasted_iota(jnp.int32, z.shape, 0))
        l1m = jnp.where(valid, l1m, 0.0)
    a = jnp.exp(ls + _split_dot(l1m, upper) + carry)
    if diag:
        a = jnp.where(valid, a, 0.0)
    return _dot(a.astype(BF16), v), carry + jnp.sum(l1m, axis=1, keepdims=True)


def _prompt_sb_kernel(q_ref, k_ref, v_ref, o_ref, *, tq, tk):
    i = pl.program_id(2)
    q = q_ref[0]
    upper = _strict_upper(tk)
    start = pl.multiple_of(i * tk, tk)
    acc, carry = _sb_tile(q, k_ref[0, pl.ds(start, tk), :], v_ref[0, pl.ds(start, tk), :],
                          jnp.zeros((tq, 1), F32), upper, True)

    def body(jj, state):
        acc, carry = state
        st = pl.multiple_of((i - 1 - jj) * tk, tk)
        o, carry = _sb_tile(q, k_ref[0, pl.ds(st, tk), :], v_ref[0, pl.ds(st, tk), :], carry, upper, False)
        return acc + o, carry

    acc, _ = lax.fori_loop(0, i, body, (acc, carry))
    o_ref[0] = acc


def _prompt_sb(q, kvb, b, t):
    q3 = q.reshape(b, t, q.shape[-1])
    kv3 = kvb.reshape(b, t, kvb.shape[-1])
    qbase = NSA_WIDTH // HEAD_DIM
    return pl.pallas_call(
        functools.partial(_prompt_sb_kernel, tq=TQ, tk=TQ),
        grid=(b, SB_HEADS, t // TQ),
        in_specs=[pl.BlockSpec((1, TQ, HEAD_DIM), lambda i, h, j: (i, j, qbase + h)),
                  pl.BlockSpec((1, t, HEAD_DIM), lambda i, h, j: (i, 0, 12 + h)),
                  pl.BlockSpec((1, t, HEAD_DIM), lambda i, h, j: (i, 0, 16 + h))],
        out_specs=pl.BlockSpec((1, TQ, HEAD_DIM), lambda i, h, j: (i, j, h)),
        out_shape=jax.ShapeDtypeStruct((b, t, SB_WIDTH), F32),
        compiler_params=_cparams(("parallel", "parallel", "parallel")),
        name="prompt_sb",
    )(q3, kv3, kv3)


def _ssm_kernel(u_ref, h0_ref, wre_ref, wim_ref, cre_ref, cim_ref, d_ref, pre_ref, pim_ref,
                y_ref, hl_ref, hr_ref, hi_ref, *, t, tc):
    p8r = pre_ref[...]
    p8i = pim_ref[...]
    row = lax.broadcasted_iota(jnp.int32, (tc, SSM_TILE), 0) % SUBLANES

    def chunk(c, carry):
        cr, ci = carry
        start = pl.multiple_of(c * tc, tc)
        u = u_ref[0, pl.ds(start, tc), :]
        ub = u.astype(BF16)
        xr = _dot(ub, wre_ref[0])
        xi = _dot(ub, wim_ref[0])
        for sh in (1, 2, 4):
            ar = p8r[sh - 1:sh, :]
            ai = p8i[sh - 1:sh, :]
            sr = jnp.where(row >= sh, pltpu.roll(xr, sh, 0), 0.0)
            si = jnp.where(row >= sh, pltpu.roll(xi, sh, 0), 0.0)
            xr, xi = xr + ar * sr - ai * si, xi + ar * si + ai * sr
        for g in range(tc // SUBLANES):
            lo, hi = g * SUBLANES, (g + 1) * SUBLANES
            br = xr[lo:hi] + p8r * cr - p8i * ci
            bi = xi[lo:hi] + p8r * ci + p8i * cr
            hr_ref[lo:hi, :] = br
            hi_ref[lo:hi, :] = bi
            cr, ci = br[SUBLANES - 1:SUBLANES], bi[SUBLANES - 1:SUBLANES]
        y = (_dot(hr_ref[...].astype(BF16), cre_ref[0]) - _dot(hi_ref[...].astype(BF16), cim_ref[0])
             + d_ref[0] * u)
        y_ref[0, pl.ds(start, tc), :] = y
        return cr, ci

    cr, ci = lax.fori_loop(0, t // tc, chunk, (h0_ref[0, 0:1, :], h0_ref[0, 1:2, :]))
    hl_ref[0, 0:1, :] = cr
    hl_ref[0, 1:2, :] = ci


def _ssm(zc, h0, sw, b, t):
    wre, wim, cre, cim, dsk, pre, pim = sw
    tc = min(t, 256)
    u3 = zc.reshape(b, t, zc.shape[-1])
    ubase = (NSA_WIDTH + SB_WIDTH) // SSM_U_TILE
    wspec = lambda shp: pl.BlockSpec((1,) + shp, lambda i, k: (k, 0, 0))
    return pl.pallas_call(
        functools.partial(_ssm_kernel, t=t, tc=tc),
        grid=(b, SSM_TILES),
        in_specs=[pl.BlockSpec((1, t, SSM_U_TILE), lambda i, k: (i, 0, ubase + k)),
                  pl.BlockSpec((1, 2, SSM_TILE), lambda i, k: (i, 0, k)),
                  wspec((SSM_U_TILE, SSM_TILE)), wspec((SSM_U_TILE, SSM_TILE)),
                  wspec((SSM_TILE, SSM_U_TILE)), wspec((SSM_TILE, SSM_U_TILE)),
                  wspec((1, SSM_U_TILE)),
                  pl.BlockSpec((SUBLANES, SSM_TILE), lambda i, k: (0, k)),
                  pl.BlockSpec((SUBLANES, SSM_TILE), lambda i, k: (0, k))],
        out_specs=[pl.BlockSpec((1, t, SSM_U_TILE), lambda i, k: (i, 0, k)),
                   pl.BlockSpec((1, 2, SSM_TILE), lambda i, k: (i, 0, k))],
        out_shape=[jax.ShapeDtypeStruct((b, t, SSM_WIDTH), F32),
                   jax.ShapeDtypeStruct((b, 2, SSM_LANES), F32)],
        scratch_shapes=[pltpu.VMEM((tc, SSM_TILE), F32), pltpu.VMEM((tc, SSM_TILE), F32)],
        compiler_params=_cparams(("parallel", "parallel")),
        name="ssm_scan",
    )(u3, h0, wre, wim, cre, cim, dsk, pre, pim)


def _rms(h, gain):
    return h * lax.rsqrt(jnp.mean(h * h, axis=1, keepdims=True) + EPS) * gain


def _silu(z):
    return z * _sigmoid(z)


def _mix_kernel(ocmp_ref, osel_ref, owin_ref, gate_ref, ob_ref, yc_ref, za_ref, zb_ref, zc_ref, x_ref,
                wglu_ref, wout_ref, gain_ref, lng_ref, lnb_ref, y_ref, yb_ref, mixed_ref, *, alpha):
    gates = _sigmoid(gate_ref[...])
    for h in range(NSA_HEADS):
        sl = slice(h * HEAD_DIM, (h + 1) * HEAD_DIM)
        oa = (gates[:, 3 * h:3 * h + 1] * ocmp_ref[:, sl] + gates[:, 3 * h + 1:3 * h + 2] * osel_ref[:, sl]
              + gates[:, 3 * h + 2:3 * h + 3] * owin_ref[:, sl])
        mixed_ref[:, sl] = oa * _silu(za_ref[:, sl])
    mixed_ref[:, 0:NSA_WIDTH] = _rms(mixed_ref[:, 0:NSA_WIDTH], gain_ref[:, 0:NSA_WIDTH])
    b0, c0 = NSA_WIDTH, NSA_WIDTH + SB_WIDTH
    mixed_ref[:, b0:c0] = _rms(ob_ref[...] * _silu(zb_ref[...]), gain_ref[:, b0:c0])
    glu = _dot(yc_ref[...].astype(BF16), wglu_ref[...])
    yc = glu[:, 0:SSM_WIDTH] * _sigmoid(glu[:, SSM_WIDTH:2 * SSM_WIDTH])
    mixed_ref[:, c0:c0 + SSM_WIDTH] = _rms(yc * _silu(zc_ref[...]), gain_ref[:, c0:c0 + SSM_WIDTH])
    h = alpha * x_ref[...] + _dot(mixed_ref[...].astype(BF16), wout_ref[...])
    mu = jnp.mean(h, axis=1, keepdims=True)
    hc = h - mu
    var = jnp.mean(hc * hc, axis=1, keepdims=True)
    y = hc * lax.rsqrt(var + EPS) * lng_ref[...] + lnb_ref[...]
    y_ref[...] = y
    yb_ref[...] = y.astype(BF16)


def _mix_out(ocmp, osel, owin, gate, ob, yc, zc, x, wglu, wout, gain, lng, lnb, alpha):
    m, d = x.shape
    tm = min(m, 256)
    row = lambda w, c=0: pl.BlockSpec((tm, w), lambda i, c=c: (i, c))
    full = lambda a: pl.BlockSpec(a.shape, lambda i: (0,) * a.ndim)
    return pl.pallas_call(
        functools.partial(_mix_kernel, alpha=alpha),
        grid=(m // tm,),
        in_specs=[row(NSA_WIDTH), row(NSA_WIDTH), row(NSA_WIDTH), row(LANES), row(SB_WIDTH), row(SSM_WIDTH),
                  row(NSA_WIDTH, 0), row(SB_WIDTH, NSA_WIDTH // SB_WIDTH),
                  row(SSM_WIDTH, (NSA_WIDTH + SB_WIDTH + SSM_WIDTH) // SSM_WIDTH), row(d),
                  full(wglu), full(wout), full(gain), full(lng), full(lnb)],
        out_specs=[row(d), row(d)],
        out_shape=[jax.ShapeDtypeStruct((m, d), F32), jax.ShapeDtypeStruct((m, d), BF16)],
        scratch_shapes=[pltpu.VMEM((tm, d), F32)],
        compiler_params=_cparams(("parallel",)),
        name="mix_out",
    )(ocmp.reshape(m, -1), osel.reshape(m, -1), owin.reshape(m, -1), gate, ob.reshape(m, -1),
      yc.reshape(m, -1), zc, zc, zc, x, wglu, wout, gain, lng, lnb)


def _sample_compress_kernel(pt_ref, *refs):
    pages = refs[:PAGES_PER_STEP]
    w1_ref, w2_ref, first_ref, second_ref = refs[PAGES_PER_STEP:]
    page = pages[0].shape[2]
    per_page = page // CMP_STRIDE
    half = w1_ref.shape[1]
    pool = jnp.where(lax.broadcasted_iota(jnp.int32, (per_page, page), 1) // CMP_STRIDE
                     == lax.broadcasted_iota(jnp.int32, (per_page, page), 0), 1.0, 0.0).astype(BF16)
    for i, pg in enumerate(pages):
        rows = pg[0, 0]
        prod = jnp.concatenate([rows * w1_ref[...], rows * w2_ref[...]], axis=1)
        hi = prod.astype(BF16)
        lo = (prod - hi.astype(F32)).astype(BF16)
        sums = _dot(pool, hi) + _dot(pool, lo)
        first_ref[0, i * per_page:(i + 1) * per_page, :] = sums[:, 0:half]
        second_ref[0, i * per_page:(i + 1) * per_page, :] = sums[:, half:2 * half]


def _sample_compress(cache, layer, page_table, w_cmp):
    db, n_pages = page_table.shape
    page = cache.shape[2]
    per_page = page // CMP_STRIDE
    steps = n_pages // PAGES_PER_STEP
    nc = n_pages * per_page
    half = 4 * HEAD_DIM

    def pspec(i):
        return pl.BlockSpec((1, 1, page, half),
                            lambda b, s, pt, i=i: (layer, pt[b, s * PAGES_PER_STEP + i], 0, 0))

    rows = PAGES_PER_STEP * per_page
    tiled = lambda w: jnp.concatenate([jnp.tile(w[kv], (per_page, 1)) for kv in (0, 0, 1, 1)], axis=1)
    w1, w2 = tiled(w_cmp[:, :CMP_STRIDE]), tiled(w_cmp[:, CMP_STRIDE:])
    return pl.pallas_call(
        _sample_compress_kernel,
        grid_spec=pltpu.PrefetchScalarGridSpec(
            num_scalar_prefetch=1,
            grid=(db, steps),
            in_specs=[pspec(i) for i in range(PAGES_PER_STEP)]
                     + [pl.BlockSpec((page, half), lambda b, s, pt: (0, 0)),
                        pl.BlockSpec((page, half), lambda b, s, pt: (0, 0))],
            out_specs=[pl.BlockSpec((1, rows, half), lambda b, s, pt: (b, s, 0)),
                       pl.BlockSpec((1, rows, half), lambda b, s, pt: (b, s, 0))]),
        out_shape=[jax.ShapeDtypeStruct((db, nc, half), F32), jax.ShapeDtypeStruct((db, nc, half), F32)],
        compiler_params=_cparams(("parallel", "parallel")),
        name="sample_compress",
    )(page_table, *([cache] * PAGES_PER_STEP), w1, w2)


def _sample_cmpwin_kernel(q_ref, first_ref, second_ref, bias_ref, win_ref, wbias_ref,
                          ocmp_ref, sel_ref, owin_ref, *, ts, nc, pos0):
    comp = first_ref[0] + pltpu.roll(second_ref[0], nc - 1, 0)
    nb = nc // CMP_PER_SEL
    pos = pos0 + lax.broadcasted_iota(jnp.int32, (ts, nb), 0)
    for g in range(NSA_KV_HEADS):
        qg = _stack_heads(q_ref[0, :, pl.ds(g * NSA_GROUP * HEAD_DIM, NSA_GROUP * HEAD_DIM)], HEAD_DIM)
        ck = comp[:, g * HEAD_DIM:(g + 1) * HEAD_DIM].astype(BF16)
        cv = comp[:, (2 + g) * HEAD_DIM:(3 + g) * HEAD_DIM].astype(BF16)
        bias = bias_ref[NSA_GROUP * g:NSA_GROUP * (g + 1)].reshape(NSA_GROUP * ts, nc)
        o, imp = _cmp_attend(qg, ck, cv, bias, ts)
        sel_ref[0, g] = _select_blocks(imp, pos, N_SEL - 1)
        kw = win_ref[0, :, g * HEAD_DIM:(g + 1) * HEAD_DIM].astype(BF16)
        vw = win_ref[0, :, (2 + g) * HEAD_DIM:(3 + g) * HEAD_DIM].astype(BF16)
        wb = wbias_ref[NSA_GROUP * g:NSA_GROUP * (g + 1)].reshape(NSA_GROUP * ts, kw.shape[0])
        sw = _dot_nt(qg, kw) * SCALE + wb
        ow = _dot(_masked_softmax(sw, wb > 0.5 * NEG).astype(BF16), vw)
        for r in range(NSA_GROUP):
            h = NSA_GROUP * g + r
            ocmp_ref[0, :, h * HEAD_DIM:(h + 1) * HEAD_DIM] = o[r * ts:(r + 1) * ts]
            owin_ref[0, :, h * HEAD_DIM:(h + 1) * HEAD_DIM] = ow[r * ts:(r + 1) * ts]


def _sample_cmpwin(q, first, second, bias_cmp, win_all, bias_win, db, ts, past_len):
    nc = first.shape[1]
    nb = nc // CMP_PER_SEL
    nw = win_all.shape[1]
    q3 = q.reshape(db, ts, q.shape[-1])
    return pl.pallas_call(
        functools.partial(_sample_cmpwin_kernel, ts=ts, nc=nc, pos0=past_len),
        grid=(db,),
        in_specs=[pl.BlockSpec((1, ts, NSA_WIDTH), lambda b: (b, 0, 0)),
                  pl.BlockSpec((1, nc, 4 * HEAD_DIM), lambda b: (b, 0, 0)),
                  pl.BlockSpec((1, nc, 4 * HEAD_DIM), lambda b: (b, 0, 0)),
                  pl.BlockSpec((NSA_HEADS, ts, nc), lambda b: (0, 0, 0)),
                  pl.BlockSpec((1, nw, 4 * HEAD_DIM), lambda b: (b, 0, 0)),
                  pl.BlockSpec((NSA_HEADS, ts, nw), lambda b: (0, 0, 0))],
        out_specs=[pl.BlockSpec((1, ts, NSA_WIDTH), lambda b: (b, 0, 0)),
                   pl.BlockSpec((1, NSA_KV_HEADS, ts, nb), lambda b: (b, 0, 0, 0)),
                   pl.BlockSpec((1, ts, NSA_WIDTH), lambda b: (b, 0, 0))],
        out_shape=[jax.ShapeDtypeStruct((db, ts, NSA_WIDTH), F32),
                   jax.ShapeDtypeStruct((db, NSA_KV_HEADS, ts, nb), F32),
                   jax.ShapeDtypeStruct((db, ts, NSA_WIDTH), F32)],
        compiler_params=_cparams(("parallel",)),
        name="sample_cmpwin",
    )(q3, first, second, bias_cmp, win_all, bias_win)


def _sample_sel_kernel(pt_ref, *refs, ts, page):
    pages = refs[:PAGES_PER_STEP]
    q_ref, sel_ref, new_ref, bias_ref, nbias_ref, o_ref, m_ref, l_ref, acc_ref = refs[PAGES_PER_STEP:]
    s_idx = pl.program_id(1)
    last = pl.num_programs(1) - 1
    rows = NSA_GROUP * ts
    nkeys = PAGES_PER_STEP * page
    blocks = nkeys // SEL_BLOCK

    @pl.when(s_idx == 0)
    def _():
        _flash_init(m_ref, l_ref, acc_ref)

    expand = jnp.where(lax.broadcasted_iota(jnp.int32, (blocks, nkeys), 0)
                       == lax.broadcasted_iota(jnp.int32, (blocks, nkeys), 1) // SEL_BLOCK, 1.0, 0.0).astype(BF16)
    for g in range(NSA_KV_HEADS):
        qg = _stack_heads(q_ref[0, :, pl.ds(g * NSA_GROUP * HEAD_DIM, NSA_GROUP * HEAD_DIM)], HEAD_DIM)
        k = jnp.concatenate([pg[0, 0, :, g * HEAD_DIM:(g + 1) * HEAD_DIM] for pg in pages], axis=0).astype(BF16)
        v = jnp.concatenate([pg[0, 0, :, (2 + g) * HEAD_DIM:(3 + g) * HEAD_DIM] for pg in pages],
                            axis=0).astype(BF16)
        bias = bias_ref[jnp.where(s_idx == last, 1, 0), NSA_GROUP * g:NSA_GROUP * (g + 1)].reshape(rows, nkeys)
        s = _dot_nt(qg, k) * SCALE + bias
        chosen = _dot(sel_ref[0, g, 0].astype(BF16), expand) > 0.5
        s = jnp.where(chosen[None], s.reshape(NSA_GROUP, ts, nkeys), NEG).reshape(rows, nkeys)
        sl = slice(g * rows, (g + 1) * rows)
        _flash_tile(s, v, m_ref.at[sl], l_ref.at[sl], acc_ref.at[sl])

        @pl.when(s_idx == last)
        def _():
            kn = _pad_keys(new_ref[0, :, g * HEAD_DIM:(g + 1) * HEAD_DIM])
            vn = _pad_keys(new_ref[0, :, (2 + g) * HEAD_DIM:(3 + g) * HEAD_DIM])
            nbias = nbias_ref[NSA_GROUP * g:NSA_GROUP * (g + 1)].reshape(rows, LANES)
            _flash_tile(_dot_nt(qg, kn) * SCALE + nbias, vn, m_ref.at[sl], l_ref.at[sl], acc_ref.at[sl])
            o = acc_ref[sl] / l_ref[sl]
            for r in range(NSA_GROUP):
                h = NSA_GROUP * g + r
                o_ref[0, :, h * HEAD_DIM:(h + 1) * HEAD_DIM] = o[r * ts:(r + 1) * ts]


def _sample_sel(cache, layer, page_table, q, sel_steps, kv_new, bias_steps, bias_new, db, ts):
    n_pages = page_table.shape[1]
    page = cache.shape[2]
    steps = n_pages // PAGES_PER_STEP
    half = 4 * HEAD_DIM
    nkeys = PAGES_PER_STEP * page
    blocks = nkeys // SEL_BLOCK
    q3 = q.reshape(db, ts, q.shape[-1])
    new3 = kv_new.reshape(db, ts, kv_new.shape[-1])

    def pspec(i):
        return pl.BlockSpec((1, 1, page, half),
                            lambda b, s, pt, i=i: (layer, pt[b, s * PAGES_PER_STEP + i], 0, 1))

    return pl.pallas_call(
        functools.partial(_sample_sel_kernel, ts=ts, page=page),
        grid_spec=pltpu.PrefetchScalarGridSpec(
            num_scalar_prefetch=1,
            grid=(db, steps),
            in_specs=[pspec(i) for i in range(PAGES_PER_STEP)]
                     + [pl.BlockSpec((1, ts, NSA_WIDTH), lambda b, s, pt: (b, 0, 0)),
                        pl.BlockSpec((1, NSA_KV_HEADS, 1, ts, blocks), lambda b, s, pt: (b, 0, s, 0, 0)),
                        pl.BlockSpec((1, ts, half), lambda b, s, pt: (b, 0, 1)),
                        pl.BlockSpec((2, NSA_HEADS, ts, nkeys), lambda b, s, pt: (0, 0, 0, 0)),
                        pl.BlockSpec((NSA_HEADS, ts, LANES), lambda b, s, pt: (0, 0, 0))],
            out_specs=pl.BlockSpec((1, ts, NSA_WIDTH), lambda b, s, pt: (b, 0, 0)),
            scratch_shapes=[pltpu.VMEM((NSA_HEADS * ts, HEAD_DIM), F32),
                            pltpu.VMEM((NSA_HEADS * ts, HEAD_DIM), F32),
                            pltpu.VMEM((NSA_HEADS * ts, HEAD_DIM), F32)]),
        out_shape=jax.ShapeDtypeStruct((db, ts, NSA_WIDTH), F32),
        compiler_params=_cparams(("parallel", "arbitrary")),
        name="sample_sel",
    )(page_table, *([cache] * PAGES_PER_STEP), q3, sel_steps, new3, bias_steps, bias_new)


def _sample_sb_kernel(pt_ref, *refs, ts, page, seg):
    pages = refs[:PAGES_PER_STEP]
    q_ref, knew_ref, vnew_ref, o_ref, carry_ref, acc_ref = refs[PAGES_PER_STEP:]
    s_idx = pl.program_id(1)
    last = pl.num_programs(1) - 1
    nkeys = PAGES_PER_STEP * page
    upper = _strict_upper(seg)

    for h in range(SB_HEADS):
        q = q_ref[0, :, h * HEAD_DIM:(h + 1) * HEAD_DIM]
        rs = slice(h * ts, (h + 1) * ts)

        @pl.when(s_idx == 0)
        def _():
            kn = _pad_keys(knew_ref[0, :, h * HEAD_DIM:(h + 1) * HEAD_DIM])
            vn = _pad_keys(vnew_ref[0, :, h * HEAD_DIM:(h + 1) * HEAD_DIM])
            o, carry = _sb_tile(q, kn, vn, jnp.zeros((ts, 1), F32), _strict_upper(LANES), True)
            acc_ref[rs, :] = o
            carry_ref[rs, :] = jnp.broadcast_to(carry, (ts, HEAD_DIM))

        k = jnp.concatenate([pg[0, 0, :, h * HEAD_DIM:(h + 1) * HEAD_DIM] for pg in pages], axis=0).astype(BF16)
        v = jnp.concatenate([pg[0, 0, :, (SB_HEADS + h) * HEAD_DIM:(SB_HEADS + h + 1) * HEAD_DIM]
                             for pg in pages], axis=0).astype(BF16)
        acc = acc_ref[rs, :]
        carry = carry_ref[rs, 0:1]
        for sg in reversed(range(nkeys // seg)):
            o, carry = _sb_tile(q, k[sg * seg:(sg + 1) * seg], v[sg * seg:(sg + 1) * seg], carry, upper, False)
            acc = acc + o
        acc_ref[rs, :] = acc
        carry_ref[rs, :] = jnp.broadcast_to(carry, (ts, HEAD_DIM))

        @pl.when(s_idx == last)
        def _():
            o_ref[0, :, h * HEAD_DIM:(h + 1) * HEAD_DIM] = acc


def _sample_sb(cache, layer, page_table, q, kv_new, db, ts):
    n_pages = page_table.shape[1]
    page = cache.shape[2]
    steps = n_pages // PAGES_PER_STEP
    width = 2 * SB_WIDTH
    q3 = q.reshape(db, ts, q.shape[-1])
    new3 = kv_new.reshape(db, ts, kv_new.shape[-1])

    def pspec(i):
        return pl.BlockSpec((1, 1, page, width),
                            lambda b, s, pt, i=i: (layer, pt[b, (steps - 1 - s) * PAGES_PER_STEP + i], 0, 0))

    return pl.pallas_call(
        functools.partial(_sample_sb_kernel, ts=ts, page=page, seg=TK),
        grid_spec=pltpu.PrefetchScalarGridSpec(
            num_scalar_prefetch=1,
            grid=(db, steps),
            in_specs=[pspec(i) for i in range(PAGES_PER_STEP)]
                     + [pl.BlockSpec((1, ts, SB_WIDTH), lambda b, s, pt: (b, 0, NSA_WIDTH // SB_WIDTH)),
                        pl.BlockSpec((1, ts, SB_WIDTH), lambda b, s, pt: (b, 0, 3)),
                        pl.BlockSpec((1, ts, SB_WIDTH), lambda b, s, pt: (b, 0, 4))],
            out_specs=pl.BlockSpec((1, ts, SB_WIDTH), lambda b, s, pt: (b, 0, 0)),
            scratch_shapes=[pltpu.VMEM((SB_HEADS * ts, HEAD_DIM), F32),
                            pltpu.VMEM((SB_HEADS * ts, HEAD_DIM), F32)]),
        out_shape=jax.ShapeDtypeStruct((db, ts, SB_WIDTH), F32),
        compiler_params=_cparams(("parallel", "arbitrary")),
        name="sample_sb",
    )(page_table, *([cache] * PAGES_PER_STEP), q3, new3, new3)


def _t5_bucket(dist):
    n = jnp.maximum(dist, 0)
    exact = NUM_BUCKETS // 2
    nf = jnp.maximum(n, 1).astype(F32)
    large = exact + (jnp.log(nf / exact) / math.log(MAX_DISTANCE / exact) * (NUM_BUCKETS - exact)).astype(jnp.int32)
    return jnp.where(n < exact, n, jnp.minimum(large, NUM_BUCKETS - 1))


def _bias(rel_bias, delta, valid):
    b = jnp.moveaxis(rel_bias[_t5_bucket(delta)].astype(F32), -1, 0)
    return jnp.where(valid[None], b, NEG)


def _prompt_tables(rel_bias, t):
    tt = jnp.arange(TQ)[:, None]
    ss = jnp.arange(TK)[None, :]
    d = tt - ss
    tables = jnp.stack([
        _bias(rel_bias, d, d >= 0),
        _bias(rel_bias, d + TK, d > -TK),
        _bias(rel_bias, d + 2 * TK, d > -2 * TK),
        _bias(rel_bias, d + 2 * TK, d + 2 * TK < WINDOW)
    ])
    nc = t // CMP_STRIDE
    dc = jnp.arange(t)[:, None] - (jnp.arange(nc)[None, :] * CMP_STRIDE + CMP_BLOCK - 1)
    return tables, _bias(rel_bias, dc, dc >= 0)


def _sample_tables(rel_bias, ts, past_len, page, nw, nw_pad):
    qpos = past_len + jnp.arange(ts)[:, None]
    nc = past_len // CMP_STRIDE
    dc = qpos - (jnp.arange(nc)[None, :] * CMP_STRIDE + CMP_BLOCK - 1)
    bias_cmp = _bias(rel_bias, dc, dc >= 0)
    widx = jnp.arange(nw_pad)[None, :]
    dw = qpos - (past_len - nw + widx)
    bias_win = _bias(rel_bias, dw, (dw >= 0) & (dw < WINDOW) & (widx < nw + ts))
    nkeys = PAGES_PER_STEP * page
    d_last = qpos - (past_len - nkeys + jnp.arange(nkeys)[None, :])
    d_far = d_last + nkeys
    bias_steps = jnp.stack([_bias(rel_bias, d_far, d_far >= 0), _bias(rel_bias, d_last, d_last >= 0)])
    nidx = jnp.arange(LANES)[None, :]
    dn = jnp.arange(ts)[:, None] - nidx
    return bias_cmp, bias_win, bias_steps, _bias(rel_bias, dn, (dn >= 0) & (nidx < ts))


def _ssm_weights(lam_re, lam_im, log_dt, b_re, b_im, c_re, c_im, d_skip):
    lam_re, lam_im = lam_re.astype(F32), lam_im.astype(F32)
    dt = jnp.exp(log_dt.astype(F32))[:, None]
    mag = jnp.exp(lam_re * dt)
    a_re, a_im = mag * jnp.cos(lam_im * dt), mag * jnp.sin(lam_im * dt)
    den = lam_re * lam_re + lam_im * lam_im
    f_re = ((a_re - 1.0) * lam_re + a_im * lam_im) / den
    f_im = (a_im * lam_re - (a_re - 1.0) * lam_im) / den
    b_re, b_im = b_re.astype(F32), b_im.astype(F32)
    bb_re = f_re[..., None] * b_re - f_im[..., None] * b_im
    bb_im = f_re[..., None] * b_im + f_im[..., None] * b_re
    gpt = SSM_GROUPS // SSM_TILES
    eye = jnp.eye(gpt, dtype=F32)

    def in_mat(bb):
        return jnp.einsum('kgpn,gh->kgnhp', bb.reshape(SSM_TILES, gpt, SSM_STATE, SSM_CH), eye).reshape(
            SSM_TILES, SSM_U_TILE, SSM_TILE).astype(BF16)

    def out_mat(c):
        return jnp.einsum('kgnp,gh->khpgn', c.astype(F32).reshape(SSM_TILES, gpt, SSM_CH, SSM_STATE), eye).reshape(
            SSM_TILES, SSM_TILE, SSM_U_TILE).astype(BF16)

    pr, pi = a_re.reshape(1, SSM_LANES), a_im.reshape(1, SSM_LANES)
    while pr.shape[0] < SUBLANES:
        tr, ti = pr[-1:], pi[-1:]
        pr, pi = (jnp.concatenate([pr, pr * tr - pi * ti]), jnp.concatenate([pi, pr * ti + pi * tr]))
    return (in_mat(bb_re), in_mat(bb_im), out_mat(c_re), out_mat(c_im),
            d_skip.astype(F32).reshape(SSM_TILES, 1, SSM_U_TILE), pr, pi)


def _split_w_in(w_in):
    sizes = (NSA_WIDTH, 6 * NSA_KV_HEADS * HEAD_DIM, 3 * NSA_HEADS, NSA_WIDTH,
             SB_WIDTH, SB_WIDTH, SB_WIDTH, SB_WIDTH, SSM_WIDTH, SSM_WIDTH)
    offs = [0]
    for s in sizes:
        offs.append(offs[-1] + s)
    q_a, kv_a, g_a, z_a, q_b, k_b, v_b, z_b, u_c, z_c = (w_in[..., offs[i]:offs[i + 1]] for i in range(10))
    cat = lambda xs: jnp.concatenate(xs, axis=-1).astype(BF16)
    pad = jnp.zeros(g_a.shape[:-1] + (LANES - g_a.shape[-1],), g_a.dtype)
    return cat([q_a, q_b]), cat([kv_a, k_b, v_b]), cat([z_a, z_b, u_c, z_c]), cat([g_a, pad])


def _in_proj(xb, wl):
    wa, wb, wc, wd = wl
    (q,) = _matmul(xb, wa, (BF16,), "in_proj_q")
    kvf, kvb = _matmul(xb, wb, (F32, BF16), "in_proj_kv")
    (zc,) = _matmul(xb, wc, (F32,), "in_proj_z")
    (gate,) = _matmul(xb, wd, (F32,), "in_proj_gate")
    return q, kvf, kvb, zc, gate


def kernel(x_prompt, x_sample, cache_nsa, cache_sb, state_win, state_ssm, page_table, rel_bias, w_in, w_cmp,
           ssm_lam_re, ssm_lam_im, ssm_log_dt, ssm_b_re, ssm_b_im, ssm_c_re, ssm_c_im, ssm_d, w_glu, mix_gain,
           w_out, ln_g, ln_b):
    depth = w_in.shape[0]
    b, t, d = x_prompt.shape
    db, ts, _ = x_sample.shape
    n_phys, page = cache_nsa.shape[1], cache_nsa.shape[2]
    n_pages = page_table.shape[1]
    past_len = n_pages * page
    nw = state_win.shape[2]
    alpha = (2 * depth) ** 0.25
    assert t % TQ == 0 and WINDOW == 2 * TK and TQ == TK and n_pages % PAGES_PER_STEP == 0
    assert past_len % SEL_BLOCK == 0 and ts <= SUBLANES and nw == WINDOW

    w_groups = _split_w_in(w_in)
    w_glu_b, w_out_b = w_glu.astype(BF16), w_out.astype(BF16)
    ssm_w = jax.vmap(_ssm_weights)(ssm_lam_re, ssm_lam_im, ssm_log_dt, ssm_b_re, ssm_b_im, ssm_c_re, ssm_c_im, ssm_d)
    tables_p, bias_cmp_p = _prompt_tables(rel_bias, t)
    nw_pad = -(-(nw + ts) // LANES) * LANES
    bias_cmp_s, bias_win_s, bias_steps_s, bias_new_s = _sample_tables(rel_bias, ts, past_len, page, nw, nw_pad)
    cache_nsa_r = cache_nsa.reshape(depth, n_phys, page, -1)
    cache_sb_r = cache_sb.reshape(depth, n_phys, page, -1)
    w_cmp = w_cmp.astype(F32)
    page_table = page_table.astype(jnp.int32)

    xp, xs = x_prompt.reshape(b * t, d), x_sample.reshape(db * ts, d)
    xpb, xsb = xp.astype(BF16), xs.astype(BF16)
    outs = [[] for _ in range(8)]
    for l in range(depth):
        wl = tuple(w[l] for w in w_groups)
        sw = tuple(w[l] for w in ssm_w)
        gain, lng, lnb = mix_gain[l].reshape(1, d), ln_g[l].reshape(1, d), ln_b[l].reshape(1, d)

        q, kvf, kvb, zc, gate = _in_proj(xpb, wl)
        o_cmp, sel = _prompt_cmp(q, kvf, w_cmp[l], bias_cmp_p, b, t)
        o_sel, o_win = _prompt_selwin(q, kvb, sel, tables_p, b, t)
        o_b = _prompt_sb(q, kvb, b, t)
        y_c, h_p = _ssm(zc, jnp.zeros((b, 2, SSM_LANES), F32), sw, b, t)
        xp, xpb = _mix_out(o_cmp, o_sel, o_win, gate, o_b, y_c, zc, xp, w_glu_b[l], w_out_b[l], gain, lng, lnb,
                           alpha)
        kv3 = kvf.reshape(b, t, -1)
        outs[0].append(kv3[:, :, 0:1024].reshape(b, t, 4, NSA_KV_HEADS, HEAD_DIM))
        outs[2].append(kv3[:, :, 1536:2560].reshape(b, t, 2, SB_HEADS, HEAD_DIM))
        outs[4].append(kv3[:, max(t - WINDOW, 0):, 1024:1536].reshape(b, -1, 2, NSA_KV_HEADS, HEAD_DIM))
        outs[6].append(h_p.reshape(b, 2, SSM_GROUPS, SSM_STATE))

        q, kvf, kvb, zc, gate = _in_proj(xsb, wl)
        kv3 = kvf.reshape(db, ts, -1)
        win_all = jnp.concatenate([state_win[l].reshape(db, nw, -1), kv3[:, :, 1024:1536],
                                   jnp.zeros((db, nw_pad - nw - ts, 4 * HEAD_DIM), F32)], axis=1)
        first, second = _sample_compress(cache_nsa_r, l, page_table, w_cmp[l])
        o_cmp, sel, o_win = _sample_cmpwin(q, first, second, bias_cmp_s, win_all, bias_win_s, db, ts, past_len)
        steps = n_pages // PAGES_PER_STEP
        sel_steps = jnp.moveaxis(sel.reshape(db, NSA_KV_HEADS, ts, steps, -1), 3, 2)
        o_sel = _sample_sel(cache_nsa_r, l, page_table, q, sel_steps, kvf, bias_steps_s, bias_new_s, db, ts)
        o_b = _sample_sb(cache_sb_r, l, page_table, q, kvf, db, ts)
        y_c, h_s = _ssm(zc, state_ssm[l].reshape(db, 2, SSM_LANES).astype(F32), sw, db, ts)
        xs, xsb = _mix_out(o_cmp, o_sel, o_win, gate, o_b, y_c, zc, xs, w_glu_b[l], w_out_b[l], gain, lng, lnb,
                           alpha)
        outs[1].append(kv3[:, :, 0:1024].reshape(db, ts, 4, NSA_KV_HEADS, HEAD_DIM))
        outs[3].append(kv3[:, :, 1536:2560].reshape(db, ts, 2, SB_HEADS, HEAD_DIM))
        outs[5].append(win_all[:, ts:ts + nw].reshape(db, nw, 2, NSA_KV_HEADS, HEAD_DIM))
        outs[7].append(h_s.reshape(db, 2, SSM_GROUPS, SSM_STATE))

    stacked = [jnp.stack(o) for o in outs]
    return (xp.reshape(b, t, d), xs.reshape(db, ts, d), *stacked)
```

```python
import functools
import math

import jax
import jax.numpy as jnp
from jax import lax
from jax.experimental import pallas as pl
from jax.experimental.pallas import tpu as pltpu

F32 = jnp.float32
BF16 = jnp.bfloat16

HEAD_DIM = 128
NSA_KV_HEADS = 2
NSA_GROUP = 4
NSA_HEADS = NSA_KV_HEADS * NSA_GROUP
NSA_WIDTH = NSA_HEADS * HEAD_DIM
SB_HEADS = 4
SB_WIDTH = SB_HEADS * HEAD_DIM
SSM_CH = 16
SSM_GROUPS = 32
SSM_STATE = 64
SSM_WIDTH = SSM_CH * SSM_GROUPS
SSM_LANES = SSM_GROUPS * SSM_STATE
CMP_BLOCK = 32
CMP_STRIDE = 16
SEL_BLOCK = 64
CMP_PER_SEL = SEL_BLOCK // CMP_STRIDE
N_SEL = 16
WINDOW = 512
NUM_BUCKETS = 32
MAX_DISTANCE = 128
FORCE_SCORE = 1e4
EPS = 1e-5
SCALE = HEAD_DIM ** -0.5
NEG = -1e30

LANES = 128
SUBLANES = 8
VMEM_LIMIT = 48 * 1024 * 1024

TQ = 256
TK = 256
ROW_CHUNK = 128
TQ_SB = 512
PAGES_PER_STEP = 8
ROW_VECS = 8
SSM_TILE = 512
SSM_TILES = SSM_LANES // SSM_TILE
SSM_U_TILE = SSM_WIDTH // SSM_TILES


def _cparams(sem):
    return pltpu.CompilerParams(dimension_semantics=sem, vmem_limit_bytes=VMEM_LIMIT)


def _dot(a, b):
    return jnp.dot(a, b, preferred_element_type=F32)


def _dot_nt(a, b):
    return lax.dot_general(a, b, (((1,), (1,)), ((), ())), preferred_element_type=F32)


def _split_dot(x, w):
    hi = x.astype(BF16)
    lo = (x - hi.astype(F32)).astype(BF16)
    return _dot(hi, w) + _dot(lo, w)


def _sigmoid(x):
    return 1.0 / (1.0 + jnp.exp(-x))


def _lane_tile(x, n):
    return x if n == 1 else jnp.concatenate([x] * n, axis=1)


def _mm_kernel(x_ref, w_ref, *o_refs):
    acc = _dot(x_ref[...], w_ref[...])
    for o in o_refs:
        o[...] = acc.astype(o.dtype)


def _matmul(x, w, out_dtypes, name):
    m, k = x.shape
    n = w.shape[1]
    tm = min(m, 1024)
    tn = min(n, 512)
    outs = pl.pallas_call(
        _mm_kernel,
        grid=(m // tm, n // tn),
        in_specs=[pl.BlockSpec((tm, k), lambda i, j: (i, 0)),
                  pl.BlockSpec((k, tn), lambda i, j: (0, j))],
        out_specs=[pl.BlockSpec((tm, tn), lambda i, j: (i, j)) for _ in out_dtypes],
        out_shape=[jax.ShapeDtypeStruct((m, n), d) for d in out_dtypes],
        compiler_params=_cparams(("parallel", "parallel")),
        name=name,
    )(x, w)
    return outs


def _masked_softmax(s, valid):
    s = jnp.where(valid, s, NEG)
    m = jnp.max(s, axis=1, keepdims=True)
    e = jnp.where(valid, jnp.exp(s - m), 0.0)
    return e / jnp.maximum(jnp.sum(e, axis=1, keepdims=True), 1e-30)


def _cmp_attend(qg, ck, cv, bias, rows):
    s = _dot_nt(qg, ck) * SCALE + bias
    p = _masked_softmax(s, bias > 0.5 * NEG)
    o = _dot(p.astype(BF16), cv)
    imp = p[0:rows] + p[rows:2 * rows] + p[2 * rows:3 * rows] + p[3 * rows:4 * rows]
    nc = ck.shape[0]
    nb = nc // CMP_PER_SEL
    grp = (lax.broadcasted_iota(jnp.int32, (nc, nb), 0) // CMP_PER_SEL
           == lax.broadcasted_iota(jnp.int32, (nc, nb), 1))
    return o, _split_dot(imp, jnp.where(grp, 1.0, 0.0).astype(BF16))


def _select_blocks(imp, pos, k_top):
    rows, nb = imp.shape
    blk = lax.broadcasted_iota(jnp.int32, (rows, nb), 1)
    cur = pos // SEL_BLOCK
    score = jnp.where(blk * SEL_BLOCK <= pos, imp, -1.0)
    score = jnp.where(blk == 0, FORCE_SCORE, score)
    score = jnp.where(blk == cur, FORCE_SCORE, score)
    score = jnp.where(blk == cur - 1, FORCE_SCORE, score)
    cnt = jnp.zeros((rows, nb), F32)
    for i in range(nb):
        col = score[:, i:i + 1]
        ge = jnp.where(col >= score, 1.0, 0.0)
        gt = jnp.where(col > score, 1.0, 0.0)
        cnt = cnt + jnp.where(blk > i, ge, gt)
    return jnp.where(cnt < k_top, 1.0, 0.0)


def _flash_init(m_ref, l_ref, acc_ref):
    m_ref[...] = jnp.full(m_ref.shape, NEG, F32)
    l_ref[...] = jnp.zeros(l_ref.shape, F32)
    acc_ref[...] = jnp.zeros(acc_ref.shape, F32)


def _flash_tile(s, v, m_ref, l_ref, acc_ref):
    m_prev = m_ref[...]
    m_new = jnp.maximum(m_prev, jnp.max(s, axis=1, keepdims=True))
    alpha = jnp.exp(m_prev - m_new)
    p = jnp.exp(s - _lane_tile(m_new, s.shape[1] // LANES))
    l_ref[...] = alpha * l_ref[...] + jnp.sum(p, axis=1, keepdims=True)
    acc_ref[...] = alpha * acc_ref[...] + _dot(p.astype(BF16), v)
    m_ref[...] = m_new


def _page_vec(pg, c, page):
    return pg[0, 0, pl.ds(c, page, stride=ROW_VECS), :]


def _pad_keys(x):
    pad = jnp.zeros((LANES - x.shape[0], x.shape[1]), F32)
    return jnp.concatenate([x, pad], axis=0).astype(BF16)


def _stack_heads(q_ref_slice, width):
    return jnp.concatenate([q_ref_slice[:, HEAD_DIM * r:HEAD_DIM * (r + 1)] for r in range(NSA_GROUP)], axis=0)


def _log_sigmoid(z):
    return jnp.minimum(z, 0.0) - jnp.log(1.0 + jnp.exp(-jnp.abs(z)))


def _strict_upper(n):
    return jnp.where(lax.broadcasted_iota(jnp.int32, (n, n), 0) > lax.broadcasted_iota(jnp.int32, (n, n), 1),
                     1.0, 0.0).astype(BF16)


def _prompt_cmp_kernel(q_ref, k0_ref, k1_ref, v0_ref, v1_ref, w_ref, bias_ref, o_ref, sel_ref, ck_ref, cv_ref,
                       *, tq, nc):
    qt = pl.program_id(1)

    @pl.when(qt == 0)
    def _():
        for g in range(NSA_KV_HEADS):
            for kv, src, dst in ((0, (k0_ref, k1_ref)[g], ck_ref), (1, (v0_ref, v1_ref)[g], cv_ref)):
                first = jnp.zeros((nc, HEAD_DIM), F32)
                second = jnp.zeros((nc, HEAD_DIM), F32)
                for j in range(CMP_STRIDE):
                    rows = src[0, pl.ds(j, nc, stride=CMP_STRIDE), :]
                    first = first + rows * w_ref[kv, j:j + 1, :]
                    second = second + rows * w_ref[kv, CMP_STRIDE + j:CMP_STRIDE + j + 1, :]
                dst[g] = (first + pltpu.roll(second, nc - 1, 0)).astype(BF16)

    pos = qt * tq + lax.broadcasted_iota(jnp.int32, (tq, nc // CMP_PER_SEL), 0)
    for g in range(NSA_KV_HEADS):
        qg = _stack_heads(q_ref[0, :, pl.ds(g * NSA_GROUP * HEAD_DIM, NSA_GROUP * HEAD_DIM)], HEAD_DIM)
        bias = bias_ref[NSA_GROUP * g:NSA_GROUP * (g + 1)].reshape(NSA_GROUP * tq, nc)
        o, imp = _cmp_attend(qg, ck_ref[g], cv_ref[g], bias, tq)
        for r in range(NSA_GROUP):
            h = NSA_GROUP * g + r
            o_ref[0, :, h * HEAD_DIM:(h + 1) * HEAD_DIM] = o[r * tq:(r + 1) * tq]
        sel_ref[0, g] = _select_blocks(imp, pos, N_SEL)


def _prompt_cmp(q, kvf, w_cmp, bias_cmp, b, t):
    nc = t // CMP_STRIDE
    nb = nc // CMP_PER_SEL
    q3 = q.reshape(b, t, q.shape[-1])
    kv3 = kvf.reshape(b, t, kvf.shape[-1])
    return pl.pallas_call(
        functools.partial(_prompt_cmp_kernel, tq=TQ, nc=nc),
        grid=(b, t // TQ),
        in_specs=[pl.BlockSpec((1, TQ, NSA_WIDTH), lambda i, j: (i, j, 0)),
                  *[pl.BlockSpec((1, t, HEAD_DIM), lambda i, j, c=c: (i, 0, c)) for c in range(4)],
                  pl.BlockSpec((2, CMP_BLOCK, HEAD_DIM), lambda i, j: (0, 0, 0)),
                  pl.BlockSpec((NSA_HEADS, TQ, nc), lambda i, j: (0, j, 0))],
        out_specs=[pl.BlockSpec((1, TQ, NSA_WIDTH), lambda i, j: (i, j, 0)),
                   pl.BlockSpec((1, NSA_KV_HEADS, TQ, nb), lambda i, j: (i, 0, j, 0))],
        out_shape=[jax.ShapeDtypeStruct((b, t, NSA_WIDTH), F32),
                   jax.ShapeDtypeStruct((b, NSA_KV_HEADS, t, nb), F32)],
        scratch_shapes=[pltpu.VMEM((NSA_KV_HEADS, nc, HEAD_DIM), BF16),
                        pltpu.VMEM((NSA_KV_HEADS, nc, HEAD_DIM), BF16)],
        compiler_params=_cparams(("parallel", "arbitrary")),
        name="prompt_cmp",
    )(q3, kv3, kv3, kv3, kv3, w_cmp, bias_cmp)


def _prompt_selwin_kernel(q_ref, ks_ref, vs_ref, kw_ref, vw_ref, sel_ref, tb_ref, osel_ref, owin_ref,
                          m_ref, l_ref, acc_ref, *, tq, tk):
    i = pl.program_id(2)
    selb = sel_ref[0, 0].astype(BF16)
    nb = selb.shape[1]
    blocks_per_tile = tk // SEL_BLOCK
    erow = lax.broadcasted_iota(jnp.int32, (nb, tk), 0)
    ecol = lax.broadcasted_iota(jnp.int32, (nb, tk), 1) // SEL_BLOCK

    def tile(k_ref, v_ref, j, table, extra):
        start = pl.multiple_of(j * tk, tk)
        k = k_ref[0, pl.ds(start, tk), :]
        v = v_ref[0, pl.ds(start, tk), :]
        chunks = [(r, off) for r in range(NSA_GROUP) for off in range(0, tq, ROW_CHUNK)]

        def scores(r, off):
            bias = tb_ref[table, r, off:off + ROW_CHUNK, :]
            if extra is not None:
                bias = bias + extra[off:off + ROW_CHUNK]
            return _dot_nt(q_ref[0, off:off + ROW_CHUNK, r * HEAD_DIM:(r + 1) * HEAD_DIM], k) * SCALE + bias

        s_next = scores(*chunks[0])
        for c, (r, off) in enumerate(chunks):
            s = s_next
            if c + 1 < len(chunks):
                s_next = scores(*chunks[c + 1])
            rs = slice(r * tq + off, r * tq + off + ROW_CHUNK)
            _flash_tile(s, v, m_ref.at[rs], l_ref.at[rs], acc_ref.at[rs])

    def sel_tile(j, table):
        expand = jnp.where(erow == j * blocks_per_tile + ecol, 1.0, 0.0).astype(BF16)
        tile(ks_ref, vs_ref, j, table, (1.0 - _dot(selb, expand)) * NEG)

    def win_tile(j, table):
        tile(kw_ref, vw_ref, j, table, None)

    def finish(o_ref):
        for r in range(NSA_GROUP):
            rs = slice(r * tq, (r + 1) * tq)
            o_ref[0, :, r * HEAD_DIM:(r + 1) * HEAD_DIM] = acc_ref[rs] / l_ref[rs]

    _flash_init(m_ref, l_ref, acc_ref)

    def far_body(j, carry):
        sel_tile(j, 2)
        return carry

    lax.fori_loop(0, jnp.maximum(i - 1, 0), far_body, 0)

    @pl.when(i >= 1)
    def _():
        sel_tile(i - 1, 1)

    sel_tile(i, 0)
    finish(osel_ref)

    _flash_init(m_ref, l_ref, acc_ref)
    win_tile(i, 0)

    @pl.when(i >= 1)
    def _():
        win_tile(i - 1, 1)

    @pl.when(i >= 2)
    def _():
        win_tile(i - 2, 3)

    finish(owin_ref)


def _prompt_selwin(q, kvb, sel, tables, b, t):
    nb = sel.shape[-1]
    q3 = q.reshape(b, t, q.shape[-1])
    kv3 = kvb.reshape(b, t, kvb.shape[-1])
    gw = NSA_GROUP * HEAD_DIM
    kcol = lambda base: pl.BlockSpec((1, t, HEAD_DIM), lambda i, g, j: (i, 0, base + g))
    return pl.pallas_call(
        functools.partial(_prompt_selwin_kernel, tq=TQ, tk=TK),
        grid=(b, NSA_KV_HEADS, t // TQ),
        in_specs=[pl.BlockSpec((1, TQ, gw), lambda i, g, j: (i, j, g)),
                  kcol(4), kcol(6), kcol(8), kcol(10),
                  pl.BlockSpec((1, 1, TQ, nb), lambda i, g, j: (i, g, j, 0)),
                  pl.BlockSpec((4, NSA_GROUP, TQ, TK), lambda i, g, j: (0, g, 0, 0))],
        out_specs=[pl.BlockSpec((1, TQ, gw), lambda i, g, j: (i, j, g)),
                   pl.BlockSpec((1, TQ, gw), lambda i, g, j: (i, j, g))],
        out_shape=[jax.ShapeDtypeStruct((b, t, NSA_WIDTH), F32),
                   jax.ShapeDtypeStruct((b, t, NSA_WIDTH), F32)],
        scratch_shapes=[pltpu.VMEM((NSA_GROUP * TQ, HEAD_DIM), F32),
                        pltpu.VMEM((NSA_GROUP * TQ, HEAD_DIM), F32),
                        pltpu.VMEM((NSA_GROUP * TQ, HEAD_DIM), F32)],
        compiler_params=_cparams(("parallel", "parallel", "parallel")),
        name="prompt_selwin",
    )(q3, kv3, kv3, kv3, kv3, sel, tables)


def _sb_weights(z, carry, upper, valid):
    seg = upper.shape[0]
    ls = _log_sigmoid(z)
    l1m = ls - z
    if valid is not None:
        l1m = jnp.where(valid, l1m, 0.0)
    cums = []
    for sg in reversed(range(z.shape[1] // seg)):
        part = l1m[:, sg * seg:(sg + 1) * seg]
        cums.append(_split_dot(part, upper) + carry)
        carry = carry + jnp.sum(part, axis=1, keepdims=True)
    cum = cums[0] if len(cums) == 1 else jnp.concatenate(cums[::-1], axis=1)
    a = jnp.exp(ls + cum)
    if valid is not None:
        a = jnp.where(valid, a, 0.0)
    return a, carry


def _prompt_sb_kernel(q_ref, k_ref, v_ref, o_ref, acc_ref, carry_ref, *, tq, tk):
    i = pl.program_id(2)
    upper = _strict_upper(tk)
    acc_ref[...] = jnp.zeros(acc_ref.shape, F32)
    carry_ref[...] = jnp.zeros(carry_ref.shape, F32)

    def run(block, diag):
        start = pl.multiple_of(block * tq, tq)
        k = k_ref[0, pl.ds(start, tq), :]
        v = v_ref[0, pl.ds(start, tq), :]
        chunks = [slice(off, off + ROW_CHUNK) for off in range(0, tq, ROW_CHUNK)]
        accs = [acc_ref[rs, :] for rs in chunks]
        zs = [_dot_nt(q_ref[0, rs, :], k) * SCALE for rs in chunks]
        valids = [None] * len(chunks)
        if diag:
            valids = [(lax.broadcasted_iota(jnp.int32, (ROW_CHUNK, tq), 1)
                       < lax.broadcasted_iota(jnp.int32, (ROW_CHUNK, tq), 0) + rs.start) for rs in chunks]
        weights = [_sb_weights(z, carry_ref[rs, 0:1], upper, valid) for z, rs, valid in zip(zs, chunks, valids)]
        outs = [_dot(a.astype(BF16), v) for a, _ in weights]
        for rs, acc, o, (_, carry) in zip(chunks, accs, outs, weights):
            acc_ref[rs, :] = acc + o
            carry_ref[rs, :] = jnp.broadcast_to(carry, (ROW_CHUNK, HEAD_DIM))

    run(i, True)

    def body(jj, c):
        run(i - 1 - jj, False)
        return c

    lax.fori_loop(0, i, body, 0)
    o_ref[0] = acc_ref[...]


def _prompt_sb(q, kvb, b, t):
    q3 = q.reshape(b, t, q.shape[-1])
    kv3 = kvb.reshape(b, t, kvb.shape[-1])
    qbase = NSA_WIDTH // HEAD_DIM
    return pl.pallas_call(
        functools.partial(_prompt_sb_kernel, tq=TQ_SB, tk=TK),
        grid=(b, SB_HEADS, t // TQ_SB),
        in_specs=[pl.BlockSpec((1, TQ_SB, HEAD_DIM), lambda i, h, j: (i, j, qbase + h)),
                  pl.BlockSpec((1, t, HEAD_DIM), lambda i, h, j: (i, 0, 12 + h)),
                  pl.BlockSpec((1, t, HEAD_DIM), lambda i, h, j: (i, 0, 16 + h))],
        out_specs=pl.BlockSpec((1, TQ_SB, HEAD_DIM), lambda i, h, j: (i, j, h)),
        out_shape=jax.ShapeDtypeStruct((b, t, SB_WIDTH), F32),
        scratch_shapes=[pltpu.VMEM((TQ_SB, HEAD_DIM), F32), pltpu.VMEM((TQ_SB, HEAD_DIM), F32)],
        compiler_params=_cparams(("parallel", "parallel", "parallel")),
        name="prompt_sb",
    )(q3, kv3, kv3)


def _ssm_kernel(u_ref, h0_ref, wre_ref, wim_ref, cre_ref, cim_ref, d_ref, pre_ref, pim_ref,
                y_ref, hl_ref, hr_ref, hi_ref, *, t, tc):
    p8r = pre_ref[...]
    p8i = pim_ref[...]
    row = lax.broadcasted_iota(jnp.int32, (tc, SSM_TILE), 0) % SUBLANES

    def chunk(c, carry):
        cr, ci = carry
        start = pl.multiple_of(c * tc, tc)
        u = u_ref[0, pl.ds(start, tc), :]
        ub = u.astype(BF16)
        xr = _dot(ub, wre_ref[0])
        xi = _dot(ub, wim_ref[0])
        for sh in (1, 2, 4):
            ar = p8r[sh - 1:sh, :]
            ai = p8i[sh - 1:sh, :]
            sr = jnp.where(row >= sh, pltpu.roll(xr, sh, 0), 0.0)
            si = jnp.where(row >= sh, pltpu.roll(xi, sh, 0), 0.0)
            xr, xi = xr + ar * sr - ai * si, xi + ar * si + ai * sr
        for g in range(tc // SUBLANES):
            lo, hi = g * SUBLANES, (g + 1) * SUBLANES
            br = xr[lo:hi] + p8r * cr - p8i * ci
            bi = xi[lo:hi] + p8r * ci + p8i * cr
            hr_ref[lo:hi, :] = br
            hi_ref[lo:hi, :] = bi
            cr, ci = br[SUBLANES - 1:SUBLANES], bi[SUBLANES - 1:SUBLANES]
        y = (_dot(hr_ref[...].astype(BF16), cre_ref[0]) - _dot(hi_ref[...].astype(BF16), cim_ref[0])
             + d_ref[0] * u)
        y_ref[0, pl.ds(start, tc), :] = y
        return cr, ci

    cr, ci = lax.fori_loop(0, t // tc, chunk, (h0_ref[0, 0:1, :], h0_ref[0, 1:2, :]))
    hl_ref[0, 0:1, :] = cr
    hl_ref[0, 1:2, :] = ci


def _ssm(zc, h0, sw, b, t):
    wre, wim, cre, cim, dsk, pre, pim = sw
    tc = min(t, 256)
    u3 = zc.reshape(b, t, zc.shape[-1])
    ubase = (NSA_WIDTH + SB_WIDTH) // SSM_U_TILE
    wspec = lambda shp: pl.BlockSpec((1,) + shp, lambda i, k: (k, 0, 0))
    return pl.pallas_call(
        functools.partial(_ssm_kernel, t=t, tc=tc),
        grid=(b, SSM_TILES),
        in_specs=[pl.BlockSpec((1, t, SSM_U_TILE), lambda i, k: (i, 0, ubase + k)),
                  pl.BlockSpec((1, 2, SSM_TILE), lambda i, k: (i, 0, k)),
                  wspec((SSM_U_TILE, SSM_TILE)), wspec((SSM_U_TILE, SSM_TILE)),
                  wspec((SSM_TILE, SSM_U_TILE)), wspec((SSM_TILE, SSM_U_TILE)),
                  wspec((1, SSM_U_TILE)),
                  pl.BlockSpec((SUBLANES, SSM_TILE), lambda i, k: (0, k)),
                  pl.BlockSpec((SUBLANES, SSM_TILE), lambda i, k: (0, k))],
        out_specs=[pl.BlockSpec((1, t, SSM_U_TILE), lambda i, k: (i, 0, k)),
                   pl.BlockSpec((1, 2, SSM_TILE), lambda i, k: (i, 0, k))],
        out_shape=[jax.ShapeDtypeStruct((b, t, SSM_WIDTH), F32),
                   jax.ShapeDtypeStruct((b, 2, SSM_LANES), F32)],
        scratch_shapes=[pltpu.VMEM((tc, SSM_TILE), F32), pltpu.VMEM((tc, SSM_TILE), F32)],
        compiler_params=_cparams(("parallel", "parallel")),
        name="ssm_scan",
    )(u3, h0, wre, wim, cre, cim, dsk, pre, pim)


def _rms(h, gain):
    return h * lax.rsqrt(jnp.mean(h * h, axis=1, keepdims=True) + EPS) * gain


def _silu(z):
    return z * _sigmoid(z)


def _mix_kernel(ocmp_ref, osel_ref, owin_ref, gate_ref, ob_ref, yc_ref, za_ref, zb_ref, zc_ref, x_ref,
                wglu_ref, wout_ref, gain_ref, lng_ref, lnb_ref, y_ref, yb_ref, mixed_ref, *, alpha):
    gates = _sigmoid(gate_ref[...])
    for h in range(NSA_HEADS):
        sl = slice(h * HEAD_DIM, (h + 1) * HEAD_DIM)
        oa = (gates[:, 3 * h:3 * h + 1] * ocmp_ref[:, sl] + gates[:, 3 * h + 1:3 * h + 2] * osel_ref[:, sl]
              + gates[:, 3 * h + 2:3 * h + 3] * owin_ref[:, sl])
        mixed_ref[:, sl] = oa * _silu(za_ref[:, sl])
    mixed_ref[:, 0:NSA_WIDTH] = _rms(mixed_ref[:, 0:NSA_WIDTH], gain_ref[:, 0:NSA_WIDTH])
    b0, c0 = NSA_WIDTH, NSA_WIDTH + SB_WIDTH
    mixed_ref[:, b0:c0] = _rms(ob_ref[...] * _silu(zb_ref[...]), gain_ref[:, b0:c0])
    glu = _dot(yc_ref[...].astype(BF16), wglu_ref[...])
    yc = glu[:, 0:SSM_WIDTH] * _sigmoid(glu[:, SSM_WIDTH:2 * SSM_WIDTH])
    mixed_ref[:, c0:c0 + SSM_WIDTH] = _rms(yc * _silu(zc_ref[...]), gain_ref[:, c0:c0 + SSM_WIDTH])
    h = alpha * x_ref[...] + _dot(mixed_ref[...].astype(BF16), wout_ref[...])
    mu = jnp.mean(h, axis=1, keepdims=True)
    hc = h - mu
    var = jnp.mean(hc * hc, axis=1, keepdims=True)
    y = hc * lax.rsqrt(var + EPS) * lng_ref[...] + lnb_ref[...]
    y_ref[...] = y
    yb_ref[...] = y.astype(BF16)


def _mix_out(ocmp, osel, owin, gate, ob, yc, zc, x, wglu, wout, gain, lng, lnb, alpha):
    m, d = x.shape
    tm = min(m, 256)
    row = lambda w, c=0: pl.BlockSpec((tm, w), lambda i, c=c: (i, c))
    full = lambda a: pl.BlockSpec(a.shape, lambda i: (0,) * a.ndim)
    return pl.pallas_call(
        functools.partial(_mix_kernel, alpha=alpha),
        grid=(m // tm,),
        in_specs=[row(NSA_WIDTH), row(NSA_WIDTH), row(NSA_WIDTH), row(LANES), row(SB_WIDTH), row(SSM_WIDTH),
                  row(NSA_WIDTH, 0), row(SB_WIDTH, NSA_WIDTH // SB_WIDTH),
                  row(SSM_WIDTH, (NSA_WIDTH + SB_WIDTH + SSM_WIDTH) // SSM_WIDTH), row(d),
                  full(wglu), full(wout), full(gain), full(lng), full(lnb)],
        out_specs=[row(d), row(d)],
        out_shape=[jax.ShapeDtypeStruct((m, d), F32), jax.ShapeDtypeStruct((m, d), BF16)],
        scratch_shapes=[pltpu.VMEM((tm, d), F32)],
        compiler_params=_cparams(("parallel",)),
        name="mix_out",
    )(ocmp.reshape(m, -1), osel.reshape(m, -1), owin.reshape(m, -1), gate, ob.reshape(m, -1),
      yc.reshape(m, -1), zc, zc, zc, x, wglu, wout, gain, lng, lnb)


def _sample_compress_kernel(pt_ref, *refs):
    pages = refs[:PAGES_PER_STEP]
    w1_ref, w2_ref, first_ref, second_ref = refs[PAGES_PER_STEP:]
    page = pages[0].shape[2] // ROW_VECS
    per_page = page // CMP_STRIDE
    half = w1_ref.shape[1]
    pool = jnp.where(lax.broadcasted_iota(jnp.int32, (per_page, page), 1) // CMP_STRIDE
                     == lax.broadcasted_iota(jnp.int32, (per_page, page), 0), 1.0, 0.0).astype(BF16)
    for i, pg in enumerate(pages):
        rows = jnp.concatenate([_page_vec(pg, c, page) for c in range(2 * NSA_KV_HEADS)], axis=1)
        prod = jnp.concatenate([rows * w1_ref[...], rows * w2_ref[...]], axis=1)
        hi = prod.astype(BF16)
        lo = (prod - hi.astype(F32)).astype(BF16)
        sums = _dot(pool, hi) + _dot(pool, lo)
        first_ref[0, i * per_page:(i + 1) * per_page, :] = sums[:, 0:half]
        second_ref[0, i * per_page:(i + 1) * per_page, :] = sums[:, half:2 * half]


def _sample_compress(cache, layer, page_table, w_cmp):
    db, n_pages = page_table.shape
    page = cache.shape[2] // ROW_VECS
    per_page = page // CMP_STRIDE
    steps = n_pages // PAGES_PER_STEP
    nc = n_pages * per_page
    half = 4 * HEAD_DIM

    def pspec(i):
        return pl.BlockSpec((1, 1, page * ROW_VECS, HEAD_DIM),
                            lambda b, s, pt, i=i: (layer, pt[b, s * PAGES_PER_STEP + i], 0, 0))

    rows = PAGES_PER_STEP * per_page
    tiled = lambda w: jnp.concatenate([jnp.tile(w[kv], (per_page, 1)) for kv in (0, 0, 1, 1)], axis=1)
    w1, w2 = tiled(w_cmp[:, :CMP_STRIDE]), tiled(w_cmp[:, CMP_STRIDE:])
    return pl.pallas_call(
        _sample_compress_kernel,
        grid_spec=pltpu.PrefetchScalarGridSpec(
            num_scalar_prefetch=1,
            grid=(db, steps),
            in_specs=[pspec(i) for i in range(PAGES_PER_STEP)]
                     + [pl.BlockSpec((page, half), lambda b, s, pt: (0, 0)),
                        pl.BlockSpec((page, half), lambda b, s, pt: (0, 0))],
            out_specs=[pl.BlockSpec((1, rows, half), lambda b, s, pt: (b, s, 0)),
                       pl.BlockSpec((1, rows, half), lambda b, s, pt: (b, s, 0))]),
        out_shape=[jax.ShapeDtypeStruct((db, nc, half), F32), jax.ShapeDtypeStruct((db, nc, half), F32)],
        compiler_params=_cparams(("parallel", "parallel")),
        name="sample_compress",
    )(page_table, *([cache] * PAGES_PER_STEP), w1, w2)


def _sample_cmpwin_kernel(q_ref, first_ref, second_ref, bias_ref, win_ref, wbias_ref,
                          ocmp_ref, sel_ref, owin_ref, *, ts, nc, pos0):
    comp = first_ref[0] + pltpu.roll(second_ref[0], nc - 1, 0)
    nb = nc // CMP_PER_SEL
    pos = pos0 + lax.broadcasted_iota(jnp.int32, (ts, nb), 0)
    for g in range(NSA_KV_HEADS):
        qg = _stack_heads(q_ref[0, :, pl.ds(g * NSA_GROUP * HEAD_DIM, NSA_GROUP * HEAD_DIM)], HEAD_DIM)
        ck = comp[:, g * HEAD_DIM:(g + 1) * HEAD_DIM].astype(BF16)
        cv = comp[:, (2 + g) * HEAD_DIM:(3 + g) * HEAD_DIM].astype(BF16)
        bias = bias_ref[NSA_GROUP * g:NSA_GROUP * (g + 1)].reshape(NSA_GROUP * ts, nc)
        o, imp = _cmp_attend(qg, ck, cv, bias, ts)
        sel_ref[0, g] = _select_blocks(imp, pos, N_SEL - 1)
        kw = win_ref[0, :, g * HEAD_DIM:(g + 1) * HEAD_DIM].astype(BF16)
        vw = win_ref[0, :, (2 + g) * HEAD_DIM:(3 + g) * HEAD_DIM].astype(BF16)
        wb = wbias_ref[NSA_GROUP * g:NSA_GROUP * (g + 1)].reshape(NSA_GROUP * ts, kw.shape[0])
        sw = _dot_nt(qg, kw) * SCALE + wb
        ow = _dot(_masked_softmax(sw, wb > 0.5 * NEG).astype(BF16), vw)
        for r in range(NSA_GROUP):
            h = NSA_GROUP * g + r
            ocmp_ref[0, :, h * HEAD_DIM:(h + 1) * HEAD_DIM] = o[r * ts:(r + 1) * ts]
            owin_ref[0, :, h * HEAD_DIM:(h + 1) * HEAD_DIM] = ow[r * ts:(r + 1) * ts]


def _sample_cmpwin(q, first, second, bias_cmp, win_all, bias_win, db, ts, past_len):
    nc = first.shape[1]
    nb = nc // CMP_PER_SEL
    nw = win_all.shape[1]
    q3 = q.reshape(db, ts, q.shape[-1])
    return pl.pallas_call(
        functools.partial(_sample_cmpwin_kernel, ts=ts, nc=nc, pos0=past_len),
        grid=(db,),
        in_specs=[pl.BlockSpec((1, ts, NSA_WIDTH), lambda b: (b, 0, 0)),
                  pl.BlockSpec((1, nc, 4 * HEAD_DIM), lambda b: (b, 0, 0)),
                  pl.BlockSpec((1, nc, 4 * HEAD_DIM), lambda b: (b, 0, 0)),
                  pl.BlockSpec((NSA_HEADS, ts, nc), lambda b: (0, 0, 0)),
                  pl.BlockSpec((1, nw, 4 * HEAD_DIM), lambda b: (b, 0, 0)),
                  pl.BlockSpec((NSA_HEADS, ts, nw), lambda b: (0, 0, 0))],
        out_specs=[pl.BlockSpec((1, ts, NSA_WIDTH), lambda b: (b, 0, 0)),
                   pl.BlockSpec((1, NSA_KV_HEADS, ts, nb), lambda b: (b, 0, 0, 0)),
                   pl.BlockSpec((1, ts, NSA_WIDTH), lambda b: (b, 0, 0))],
        out_shape=[jax.ShapeDtypeStruct((db, ts, NSA_WIDTH), F32),
                   jax.ShapeDtypeStruct((db, NSA_KV_HEADS, ts, nb), F32),
                   jax.ShapeDtypeStruct((db, ts, NSA_WIDTH), F32)],
        compiler_params=_cparams(("parallel",)),
        name="sample_cmpwin",
    )(q3, first, second, bias_cmp, win_all, bias_win)


def _sample_sel_kernel(pt_ref, *refs, ts, page):
    pages = refs[:PAGES_PER_STEP]
    q_ref, sel_ref, new_ref, bias_ref, nbias_ref, o_ref, m_ref, l_ref, acc_ref = refs[PAGES_PER_STEP:]
    s_idx = pl.program_id(1)
    last = pl.num_programs(1) - 1
    rows = NSA_GROUP * ts
    nkeys = PAGES_PER_STEP * page
    blocks = nkeys // SEL_BLOCK

    @pl.when(s_idx == 0)
    def _():
        _flash_init(m_ref, l_ref, acc_ref)

    expand = jnp.where(lax.broadcasted_iota(jnp.int32, (blocks, nkeys), 0)
                       == lax.broadcasted_iota(jnp.int32, (blocks, nkeys), 1) // SEL_BLOCK, 1.0, 0.0).astype(BF16)
    for g in range(NSA_KV_HEADS):
        qg = _stack_heads(q_ref[0, :, pl.ds(g * NSA_GROUP * HEAD_DIM, NSA_GROUP * HEAD_DIM)], HEAD_DIM)
        k = jnp.concatenate([_page_vec(pg, 4 + g, page) for pg in pages], axis=0).astype(BF16)
        v = jnp.concatenate([_page_vec(pg, 6 + g, page) for pg in pages], axis=0).astype(BF16)
        bias = bias_ref[jnp.where(s_idx == last, 1, 0), NSA_GROUP * g:NSA_GROUP * (g + 1)].reshape(rows, nkeys)
        s = _dot_nt(qg, k) * SCALE + bias
        chosen = _dot(sel_ref[0, g, 0].astype(BF16), expand) > 0.5
        s = jnp.where(chosen[None], s.reshape(NSA_GROUP, ts, nkeys), NEG).reshape(rows, nkeys)
        sl = slice(g * rows, (g + 1) * rows)
        _flash_tile(s, v, m_ref.at[sl], l_ref.at[sl], acc_ref.at[sl])

        @pl.when(s_idx == last)
        def _():
            kn = _pad_keys(new_ref[0, :, g * HEAD_DIM:(g + 1) * HEAD_DIM])
            vn = _pad_keys(new_ref[0, :, (2 + g) * HEAD_DIM:(3 + g) * HEAD_DIM])
            nbias = nbias_ref[NSA_GROUP * g:NSA_GROUP * (g + 1)].reshape(rows, LANES)
            _flash_tile(_dot_nt(qg, kn) * SCALE + nbias, vn, m_ref.at[sl], l_ref.at[sl], acc_ref.at[sl])
            o = acc_ref[sl] / l_ref[sl]
            for r in range(NSA_GROUP):
                h = NSA_GROUP * g + r
                o_ref[0, :, h * HEAD_DIM:(h + 1) * HEAD_DIM] = o[r * ts:(r + 1) * ts]


def _sample_sel(cache, layer, page_table, q, sel_steps, kv_new, bias_steps, bias_new, db, ts):
    n_pages = page_table.shape[1]
    page = cache.shape[2] // ROW_VECS
    steps = n_pages // PAGES_PER_STEP
    half = 4 * HEAD_DIM
    nkeys = PAGES_PER_STEP * page
    blocks = nkeys // SEL_BLOCK
    q3 = q.reshape(db, ts, q.shape[-1])
    new3 = kv_new.reshape(db, ts, kv_new.shape[-1])

    def pspec(i):
        return pl.BlockSpec((1, 1, page * ROW_VECS, HEAD_DIM),
                            lambda b, s, pt, i=i: (layer, pt[b, s * PAGES_PER_STEP + i], 0, 0))

    return pl.pallas_call(
        functools.partial(_sample_sel_kernel, ts=ts, page=page),
        grid_spec=pltpu.PrefetchScalarGridSpec(
            num_scalar_prefetch=1,
            grid=(db, steps),
            in_specs=[pspec(i) for i in range(PAGES_PER_STEP)]
                     + [pl.BlockSpec((1, ts, NSA_WIDTH), lambda b, s, pt: (b, 0, 0)),
                        pl.BlockSpec((1, NSA_KV_HEADS, 1, ts, blocks), lambda b, s, pt: (b, 0, s, 0, 0)),
                        pl.BlockSpec((1, ts, half), lambda b, s, pt: (b, 0, 1)),
                        pl.BlockSpec((2, NSA_HEADS, ts, nkeys), lambda b, s, pt: (0, 0, 0, 0)),
                        pl.BlockSpec((NSA_HEADS, ts, LANES), lambda b, s, pt: (0, 0, 0))],
            out_specs=pl.BlockSpec((1, ts, NSA_WIDTH), lambda b, s, pt: (b, 0, 0)),
            scratch_shapes=[pltpu.VMEM((NSA_HEADS * ts, HEAD_DIM), F32),
                            pltpu.VMEM((NSA_HEADS * ts, HEAD_DIM), F32),
                            pltpu.VMEM((NSA_HEADS * ts, HEAD_DIM), F32)]),
        out_shape=jax.ShapeDtypeStruct((db, ts, NSA_WIDTH), F32),
        compiler_params=_cparams(("parallel", "arbitrary")),
        name="sample_sel",
    )(page_table, *([cache] * PAGES_PER_STEP), q3, sel_steps, new3, bias_steps, bias_new)


def _sample_sb_kernel(pt_ref, *refs, ts, page, seg):
    pages = refs[:PAGES_PER_STEP]
    q_ref, knew_ref, vnew_ref, o_ref, carry_ref, acc_ref = refs[PAGES_PER_STEP:]
    s_idx = pl.program_id(1)
    last = pl.num_programs(1) - 1
    nkeys = PAGES_PER_STEP * page
    upper = _strict_upper(seg)

    rows = SB_HEADS * ts
    heads = range(SB_HEADS)
    hs = lambda h: slice(h * HEAD_DIM, (h + 1) * HEAD_DIM)

    def attend(ks, vs, carry, up, valid):
        z = jnp.concatenate([_dot_nt(q_ref[0, :, hs(h)], ks[h]) for h in heads], axis=0) * SCALE
        a, carry = _sb_weights(z, carry, up, valid)
        return jnp.concatenate([_dot(a[h * ts:(h + 1) * ts].astype(BF16), vs[h]) for h in heads], axis=0), carry

    @pl.when(s_idx == 0)
    def _():
        kn = [_pad_keys(knew_ref[0, :, hs(h)]) for h in heads]
        vn = [_pad_keys(vnew_ref[0, :, hs(h)]) for h in heads]
        valid = (lax.broadcasted_iota(jnp.int32, (rows, LANES), 1)
                 < lax.broadcasted_iota(jnp.int32, (rows, LANES), 0) % ts)
        o, carry = attend(kn, vn, jnp.zeros((rows, 1), F32), _strict_upper(LANES), valid)
        acc_ref[...] = o
        carry_ref[...] = jnp.broadcast_to(carry, (rows, HEAD_DIM))

    ks = [jnp.concatenate([_page_vec(pg, h, page) for pg in pages], axis=0).astype(BF16) for h in heads]
    vs = [jnp.concatenate([_page_vec(pg, SB_HEADS + h, page) for pg in pages], axis=0).astype(BF16) for h in heads]
    o, carry = attend(ks, vs, carry_ref[:, 0:1], upper, None)
    acc_ref[...] = acc_ref[...] + o
    carry_ref[...] = jnp.broadcast_to(carry, (rows, HEAD_DIM))

    @pl.when(s_idx == last)
    def _():
        for h in heads:
            o_ref[0, :, hs(h)] = acc_ref[h * ts:(h + 1) * ts, :]


def _sample_sb(cache, layer, page_table, q, kv_new, db, ts):
    n_pages = page_table.shape[1]
    page = cache.shape[2] // ROW_VECS
    steps = n_pages // PAGES_PER_STEP
    q3 = q.reshape(db, ts, q.shape[-1])
    new3 = kv_new.reshape(db, ts, kv_new.shape[-1])

    def pspec(i):
        return pl.BlockSpec((1, 1, page * ROW_VECS, HEAD_DIM),
                            lambda b, s, pt, i=i: (layer, pt[b, (steps - 1 - s) * PAGES_PER_STEP + i], 0, 0))

    return pl.pallas_call(
        functools.partial(_sample_sb_kernel, ts=ts, page=page, seg=TK),
        grid_spec=pltpu.PrefetchScalarGridSpec(
            num_scalar_prefetch=1,
            grid=(db, steps),
            in_specs=[pspec(i) for i in range(PAGES_PER_STEP)]
                     + [pl.BlockSpec((1, ts, SB_WIDTH), lambda b, s, pt: (b, 0, NSA_WIDTH // SB_WIDTH)),
                        pl.BlockSpec((1, ts, SB_WIDTH), lambda b, s, pt: (b, 0, 3)),
                        pl.BlockSpec((1, ts, SB_WIDTH), lambda b, s, pt: (b, 0, 4))],
            out_specs=pl.BlockSpec((1, ts, SB_WIDTH), lambda b, s, pt: (b, 0, 0)),
            scratch_shapes=[pltpu.VMEM((SB_HEADS * ts, HEAD_DIM), F32),
                            pltpu.VMEM((SB_HEADS * ts, HEAD_DIM), F32)]),
        out_shape=jax.ShapeDtypeStruct((db, ts, SB_WIDTH), F32),
        compiler_params=_cparams(("parallel", "arbitrary")),
        name="sample_sb",
    )(page_table, *([cache] * PAGES_PER_STEP), q3, new3, new3)


def _t5_bucket(dist):
    n = jnp.maximum(dist, 0)
    exact = NUM_BUCKETS // 2
    nf = jnp.maximum(n, 1).astype(F32)
    large = exact + (jnp.log(nf / exact) / math.log(MAX_DISTANCE / exact) * (NUM_BUCKETS - exact)).astype(jnp.int32)
    return jnp.where(n < exact, n, jnp.minimum(large, NUM_BUCKETS - 1))


def _bias(rel_bias, delta, valid):
    bucket = _t5_bucket(delta)[None]
    tab = rel_bias.astype(F32).T.reshape((rel_bias.shape[1], NUM_BUCKETS) + (1,) * delta.ndim)
    b = jnp.zeros((rel_bias.shape[1],) + delta.shape, F32)
    for k in range(NUM_BUCKETS):
        b = jnp.where(bucket == k, tab[:, k], b)
    return jnp.where(valid[None], b, NEG)


def _prompt_tables(rel_bias, t):
    tt = jnp.arange(TQ)[:, None]
    ss = jnp.arange(TK)[None, :]
    d = tt - ss
    tables = jnp.stack([
        _bias(rel_bias, d, d >= 0),
        _bias(rel_bias, d + TK, d > -TK),
        _bias(rel_bias, d + 2 * TK, d > -2 * TK),
        _bias(rel_bias, d + 2 * TK, d + 2 * TK < WINDOW)
    ])
    nc = t // CMP_STRIDE
    dc = jnp.arange(t)[:, None] - (jnp.arange(nc)[None, :] * CMP_STRIDE + CMP_BLOCK - 1)
    return tables, _bias(rel_bias, dc, dc >= 0)


def _sample_tables(rel_bias, ts, past_len, page, nw, nw_pad):
    qpos = past_len + jnp.arange(ts)[:, None]
    nc = past_len // CMP_STRIDE
    dc = qpos - (jnp.arange(nc)[None, :] * CMP_STRIDE + CMP_BLOCK - 1)
    bias_cmp = _bias(rel_bias, dc, dc >= 0)
    widx = jnp.arange(nw_pad)[None, :]
    dw = qpos - (past_len - nw + widx)
    bias_win = _bias(rel_bias, dw, (dw >= 0) & (dw < WINDOW) & (widx < nw + ts))
    nkeys = PAGES_PER_STEP * page
    d_last = qpos - (past_len - nkeys + jnp.arange(nkeys)[None, :])
    d_far = d_last + nkeys
    bias_steps = jnp.stack([_bias(rel_bias, d_far, d_far >= 0), _bias(rel_bias, d_last, d_last >= 0)])
    nidx = jnp.arange(LANES)[None, :]
    dn = jnp.arange(ts)[:, None] - nidx
    return bias_cmp, bias_win, bias_steps, _bias(rel_bias, dn, (dn >= 0) & (nidx < ts))


def _ssm_weights(lam_re, lam_im, log_dt, b_re, b_im, c_re, c_im, d_skip):
    lam_re, lam_im = lam_re.astype(F32), lam_im.astype(F32)
    dt = jnp.exp(log_dt.astype(F32))[:, None]
    mag = jnp.exp(lam_re * dt)
    a_re, a_im = mag * jnp.cos(lam_im * dt), mag * jnp.sin(lam_im * dt)
    den = lam_re * lam_re + lam_im * lam_im
    f_re = ((a_re - 1.0) * lam_re + a_im * lam_im) / den
    f_im = (a_im * lam_re - (a_re - 1.0) * lam_im) / den
    b_re, b_im = b_re.astype(F32), b_im.astype(F32)
    bb_re = f_re[..., None] * b_re - f_im[..., None] * b_im
    bb_im = f_re[..., None] * b_im + f_im[..., None] * b_re
    gpt = SSM_GROUPS // SSM_TILES
    eye = jnp.eye(gpt, dtype=F32)

    def in_mat(bb):
        return jnp.einsum('kgpn,gh->kgnhp', bb.reshape(SSM_TILES, gpt, SSM_STATE, SSM_CH), eye).reshape(
            SSM_TILES, SSM_U_TILE, SSM_TILE).astype(BF16)

    def out_mat(c):
        return jnp.einsum('kgnp,gh->khpgn', c.astype(F32).reshape(SSM_TILES, gpt, SSM_CH, SSM_STATE), eye).reshape(
            SSM_TILES, SSM_TILE, SSM_U_TILE).astype(BF16)

    pr, pi = a_re.reshape(1, SSM_LANES), a_im.reshape(1, SSM_LANES)
    while pr.shape[0] < SUBLANES:
        tr, ti = pr[-1:], pi[-1:]
        pr, pi = (jnp.concatenate([pr, pr * tr - pi * ti]), jnp.concatenate([pi, pr * ti + pi * tr]))
    return (in_mat(bb_re), in_mat(bb_im), out_mat(c_re), out_mat(c_im),
            d_skip.astype(F32).reshape(SSM_TILES, 1, SSM_U_TILE), pr, pi)


def _split_w_in(w_in):
    sizes = (NSA_WIDTH, 6 * NSA_KV_HEADS * HEAD_DIM, 3 * NSA_HEADS, NSA_WIDTH,
             SB_WIDTH, SB_WIDTH, SB_WIDTH, SB_WIDTH, SSM_WIDTH, SSM_WIDTH)
    offs = [0]
    for s in sizes:
        offs.append(offs[-1] + s)
    q_a, kv_a, g_a, z_a, q_b, k_b, v_b, z_b, u_c, z_c = (w_in[..., offs[i]:offs[i + 1]] for i in range(10))
    cat = lambda xs: jnp.concatenate(xs, axis=-1).astype(BF16)
    pad = jnp.zeros(g_a.shape[:-1] + (LANES - g_a.shape[-1],), g_a.dtype)
    return cat([q_a, q_b]), cat([kv_a, k_b, v_b]), cat([z_a, z_b, u_c, z_c]), cat([g_a, pad])


def _in_proj(xb, wl):
    wa, wb, wc, wd = wl
    (q,) = _matmul(xb, wa, (BF16,), "in_proj_q")
    kvf, kvb = _matmul(xb, wb, (F32, BF16), "in_proj_kv")
    (zc,) = _matmul(xb, wc, (F32,), "in_proj_z")
    (gate,) = _matmul(xb, wd, (F32,), "in_proj_gate")
    return q, kvf, kvb, zc, gate


def kernel(x_prompt, x_sample, cache_nsa, cache_sb, state_win, state_ssm, page_table, rel_bias, w_in, w_cmp,
           ssm_lam_re, ssm_lam_im, ssm_log_dt, ssm_b_re, ssm_b_im, ssm_c_re, ssm_c_im, ssm_d, w_glu, mix_gain,
           w_out, ln_g, ln_b):
    depth = w_in.shape[0]
    b, t, d = x_prompt.shape
    db, ts, _ = x_sample.shape
    n_phys, page = cache_nsa.shape[1], cache_nsa.shape[2]
    n_pages = page_table.shape[1]
    past_len = n_pages * page
    nw = state_win.shape[2]
    alpha = (2 * depth) ** 0.25
    assert t % TQ == 0 and t % TQ_SB == 0 and TQ_SB % TK == 0 and WINDOW == 2 * TK and TQ == TK and n_pages % PAGES_PER_STEP == 0
    assert past_len % SEL_BLOCK == 0 and ts % SUBLANES == 0 and ts <= SEL_BLOCK and nw == WINDOW

    w_groups = _split_w_in(w_in)
    w_glu_b, w_out_b = w_glu.astype(BF16), w_out.astype(BF16)
    ssm_w = jax.vmap(_ssm_weights)(ssm_lam_re, ssm_lam_im, ssm_log_dt, ssm_b_re, ssm_b_im, ssm_c_re, ssm_c_im, ssm_d)
    tables_p, bias_cmp_p = _prompt_tables(rel_bias, t)
    nw_pad = -(-(nw + ts) // LANES) * LANES
    bias_cmp_s, bias_win_s, bias_steps_s, bias_new_s = _sample_tables(rel_bias, ts, past_len, page, nw, nw_pad)
    cache_nsa_r = cache_nsa.reshape(depth, n_phys, page * ROW_VECS, HEAD_DIM)
    cache_sb_r = cache_sb.reshape(depth, n_phys, page * ROW_VECS, HEAD_DIM)
    w_cmp = w_cmp.astype(F32)
    page_table = page_table.astype(jnp.int32)

    xp, xs = x_prompt.reshape(b * t, d), x_sample.reshape(db * ts, d)
    xpb, xsb = xp.astype(BF16), xs.astype(BF16)
    outs = [[] for _ in range(8)]
    for l in range(depth):
        wl = tuple(w[l] for w in w_groups)
        sw = tuple(w[l] for w in ssm_w)
        gain, lng, lnb = mix_gain[l].reshape(1, d), ln_g[l].reshape(1, d), ln_b[l].reshape(1, d)

        q, kvf, kvb, zc, gate = _in_proj(xpb, wl)
        o_cmp, sel = _prompt_cmp(q, kvf, w_cmp[l], bias_cmp_p, b, t)
        o_sel, o_win = _prompt_selwin(q, kvb, sel, tables_p, b, t)
        o_b = _prompt_sb(q, kvb, b, t)
        y_c, h_p = _ssm(zc, jnp.zeros((b, 2, SSM_LANES), F32), sw, b, t)
        xp, xpb = _mix_out(o_cmp, o_sel, o_win, gate, o_b, y_c, zc, xp, w_glu_b[l], w_out_b[l], gain, lng, lnb,
                           alpha)
        kv3 = kvf.reshape(b, t, -1)
        outs[0].append(kv3[:, :, 0:1024].reshape(b, t, 4, NSA_KV_HEADS, HEAD_DIM))
        outs[2].append(kv3[:, :, 1536:2560].reshape(b, t, 2, SB_HEADS, HEAD_DIM))
        outs[4].append(kv3[:, max(t - WINDOW, 0):, 1024:1536].reshape(b, -1, 2, NSA_KV_HEADS, HEAD_DIM))
        outs[6].append(h_p.reshape(b, 2, SSM_GROUPS, SSM_STATE))

        q, kvf, kvb, zc, gate = _in_proj(xsb, wl)
        kv3 = kvf.reshape(db, ts, -1)
        win_all = jnp.concatenate([state_win[l].reshape(db, nw, -1), kv3[:, :, 1024:1536],
                                   jnp.zeros((db, nw_pad - nw - ts, 4 * HEAD_DIM), F32)], axis=1)
        first, second = _sample_compress(cache_nsa_r, l, page_table, w_cmp[l])
        o_cmp, sel, o_win = _sample_cmpwin(q, first, second, bias_cmp_s, win_all, bias_win_s, db, ts, past_len)
        steps = n_pages // PAGES_PER_STEP
        sel_steps = jnp.moveaxis(sel.reshape(db, NSA_KV_HEADS, ts, steps, -1), 3, 2)
        o_sel = _sample_sel(cache_nsa_r, l, page_table, q, sel_steps, kvf, bias_steps_s, bias_new_s, db, ts)
        o_b = _sample_sb(cache_sb_r, l, page_table, q, kvf, db, ts)
        y_c, h_s = _ssm(zc, state_ssm[l].reshape(db, 2, SSM_LANES).astype(F32), sw, db, ts)
        xs, xsb = _mix_out(o_cmp, o_sel, o_win, gate, o_b, y_c, zc, xs, w_glu_b[l], w_out_b[l], gain, lng, lnb,
                           alpha)
        outs[1].append(kv3[:, :, 0:1024].reshape(db, ts, 4, NSA_KV_HEADS, HEAD_DIM))
        outs[3].append(kv3[:, :, 1536:2560].reshape(db, ts, 2, SB_HEADS, HEAD_DIM))
        outs[5].append(win_all[:, ts:ts + nw].reshape(db, nw, 2, NSA_KV_HEADS, HEAD_DIM))
        outs[7].append(h_s.reshape(db, 2, SSM_GROUPS, SSM_STATE))

    stacked = [jnp.stack(o) for o in outs]
    return (xp.reshape(b, t, d), xs.reshape(db, ts, d), *stacked)
```

```python
import functools
import math

import jax
import jax.numpy as jnp
from jax import lax
from jax.experimental import pallas as pl
from jax.experimental.pallas import tpu as pltpu

F32 = jnp.float32
BF16 = jnp.bfloat16

HEAD_DIM = 128
NSA_KV_HEADS = 2
NSA_GROUP = 4
NSA_HEADS = NSA_KV_HEADS * NSA_GROUP
NSA_WIDTH = NSA_HEADS * HEAD_DIM
SB_HEADS = 4
SB_WIDTH = SB_HEADS * HEAD_DIM
SSM_CH = 16
SSM_GROUPS = 32
SSM_STATE = 64
SSM_WIDTH = SSM_CH * SSM_GROUPS
SSM_LANES = SSM_GROUPS * SSM_STATE
CMP_BLOCK = 32
CMP_STRIDE = 16
SEL_BLOCK = 64
CMP_PER_SEL = SEL_BLOCK // CMP_STRIDE
N_SEL = 16
WINDOW = 512
NUM_BUCKETS = 32
MAX_DISTANCE = 128
FORCE_SCORE = 1e4
EPS = 1e-5
SCALE = HEAD_DIM ** -0.5
LOG2E = math.log2(math.e)
SCALE2 = SCALE * LOG2E
NEG = -1e30

LANES = 128
SUBLANES = 8
VMEM_LIMIT = 48 * 1024 * 1024

TQ = 256
TK = 256
ROW_CHUNK = 128
TQ_SB = 512
PAGES_PER_STEP = 8
ROW_VECS = 8
SSM_TILE = 512
SSM_TILES = SSM_LANES // SSM_TILE
SSM_U_TILE = SSM_WIDTH // SSM_TILES


def _cparams(sem):
    return pltpu.CompilerParams(dimension_semantics=sem, vmem_limit_bytes=VMEM_LIMIT)


def _dot(a, b):
    return jnp.dot(a, b, preferred_element_type=F32)


def _dot_nt(a, b):
    return lax.dot_general(a, b, (((1,), (1,)), ((), ())), preferred_element_type=F32)


def _split_dot(x, w):
    hi = x.astype(BF16)
    lo = (x - hi.astype(F32)).astype(BF16)
    return _dot(hi, w) + _dot(lo, w)


def _sigmoid(x):
    return 1.0 / (1.0 + jnp.exp(-x))


def _lane_tile(x, n):
    return x if n == 1 else jnp.concatenate([x] * n, axis=1)


def _mm_kernel(x_ref, w_ref, *o_refs):
    acc = _dot(x_ref[...], w_ref[...])
    for o in o_refs:
        o[...] = acc.astype(o.dtype)


def _matmul(x, w, out_dtypes, name):
    m, k = x.shape
    n = w.shape[1]
    tm = min(m, 1024)
    tn = min(n, 512)
    outs = pl.pallas_call(
        _mm_kernel,
        grid=(m // tm, n // tn),
        in_specs=[pl.BlockSpec((tm, k), lambda i, j: (i, 0)),
                  pl.BlockSpec((k, tn), lambda i, j: (0, j))],
        out_specs=[pl.BlockSpec((tm, tn), lambda i, j: (i, j)) for _ in out_dtypes],
        out_shape=[jax.ShapeDtypeStruct((m, n), d) for d in out_dtypes],
        compiler_params=_cparams(("parallel", "parallel")),
        name=name,
    )(x, w)
    return outs


def _masked_softmax(s, valid):
    s = jnp.where(valid, s, NEG)
    m = jnp.max(s, axis=1, keepdims=True)
    e = jnp.where(valid, jnp.exp(s - m), 0.0)
    return e / jnp.maximum(jnp.sum(e, axis=1, keepdims=True), 1e-30)


def _cmp_attend(qg, ck, cv, bias, rows):
    s = _dot_nt(qg, ck) * SCALE + bias
    p = _masked_softmax(s, bias > 0.5 * NEG)
    o = _dot(p.astype(BF16), cv)
    imp = p[0:rows] + p[rows:2 * rows] + p[2 * rows:3 * rows] + p[3 * rows:4 * rows]
    nc = ck.shape[0]
    nb = nc // CMP_PER_SEL
    grp = (lax.broadcasted_iota(jnp.int32, (nc, nb), 0) // CMP_PER_SEL
           == lax.broadcasted_iota(jnp.int32, (nc, nb), 1))
    return o, _split_dot(imp, jnp.where(grp, 1.0, 0.0).astype(BF16))


def _select_blocks(imp, pos, k_top, groups=1):
    rows, width = imp.shape
    nb = width // groups
    lane = lax.broadcasted_iota(jnp.int32, (rows, width), 1)
    blk = lane % nb
    cur = pos // SEL_BLOCK
    score = jnp.where(blk * SEL_BLOCK <= pos, imp, -1.0)
    score = jnp.where(blk == 0, FORCE_SCORE, score)
    score = jnp.where(blk == cur, FORCE_SCORE, score)
    score = jnp.where(blk == cur - 1, FORCE_SCORE, score)
    cnt = jnp.zeros((rows, width), F32)
    for i in range(nb):
        col = score[:, i:i + 1]
        for g in range(1, groups):
            col = jnp.where(lane >= g * nb, score[:, g * nb + i:g * nb + i + 1], col)
        ge = jnp.where(col >= score, 1.0, 0.0)
        gt = jnp.where(col > score, 1.0, 0.0)
        cnt = cnt + jnp.where(blk > i, ge, gt)
    return jnp.where(cnt < k_top, 1.0, 0.0)


def _flash_init(m_ref, l_ref, acc_ref):
    m_ref[...] = jnp.full(m_ref.shape, NEG, F32)
    l_ref[...] = jnp.zeros(l_ref.shape, F32)
    acc_ref[...] = jnp.zeros(acc_ref.shape, F32)


def _flash_tile(s, v, m_ref, l_ref, acc_ref):
    m_prev = m_ref[...]
    m_new = jnp.maximum(m_prev, jnp.max(s, axis=1, keepdims=True))
    alpha = jnp.exp2(m_prev - m_new)
    p = jnp.exp2(s - _lane_tile(m_new, s.shape[1] // LANES))
    l_ref[...] = alpha * l_ref[...] + jnp.sum(p, axis=1, keepdims=True)
    acc_ref[...] = alpha * acc_ref[...] + _dot(p.astype(BF16), v)
    m_ref[...] = m_new


def _page_vec(pg, c, page):
    return pg[0, 0, pl.ds(c, page, stride=ROW_VECS), :]


def _pad_keys(x):
    pad = jnp.zeros((LANES - x.shape[0], x.shape[1]), F32)
    return jnp.concatenate([x, pad], axis=0).astype(BF16)


def _stack_heads(q_ref_slice, width):
    return jnp.concatenate([q_ref_slice[:, HEAD_DIM * r:HEAD_DIM * (r + 1)] for r in range(NSA_GROUP)], axis=0)


def _log_sigmoid(z):
    return jnp.minimum(z, 0.0) - jnp.log(1.0 + jnp.exp(-jnp.abs(z)))


def _strict_upper(n):
    return jnp.where(lax.broadcasted_iota(jnp.int32, (n, n), 0) > lax.broadcasted_iota(jnp.int32, (n, n), 1),
                     1.0, 0.0).astype(BF16)


def _prompt_cmp_kernel(q_ref, k0_ref, k1_ref, v0_ref, v1_ref, w_ref, bias_ref, o_ref, sel_ref, ck_ref, cv_ref,
                       *, tq, nc):
    qt = pl.program_id(1)

    @pl.when(qt == 0)
    def _():
        for g in range(NSA_KV_HEADS):
            for kv, src, dst in ((0, (k0_ref, k1_ref)[g], ck_ref), (1, (v0_ref, v1_ref)[g], cv_ref)):
                first = jnp.zeros((nc, HEAD_DIM), F32)
                second = jnp.zeros((nc, HEAD_DIM), F32)
                for j in range(CMP_STRIDE):
                    rows = src[0, pl.ds(j, nc, stride=CMP_STRIDE), :]
                    first = first + rows * w_ref[kv, j:j + 1, :]
                    second = second + rows * w_ref[kv, CMP_STRIDE + j:CMP_STRIDE + j + 1, :]
                dst[g] = (first + pltpu.roll(second, nc - 1, 0)).astype(BF16)

    nb = nc // CMP_PER_SEL
    pool = jnp.where(lax.broadcasted_iota(jnp.int32, (nc, nb), 0) // CMP_PER_SEL
                     == lax.broadcasted_iota(jnp.int32, (nc, nb), 1), 1.0, 0.0).astype(BF16)
    chunks = [(h, off) for h in range(NSA_HEADS) for off in range(0, tq, ROW_CHUNK)]

    def scores(h, off):
        bias = bias_ref[h, off:off + ROW_CHUNK, :]
        q = q_ref[0, off:off + ROW_CHUNK, h * HEAD_DIM:(h + 1) * HEAD_DIM]
        return _dot_nt(q, ck_ref[h // NSA_GROUP]) * SCALE + bias, bias

    imp = {(g, off): None for g in range(NSA_KV_HEADS) for off in range(0, tq, ROW_CHUNK)}
    nxt = scores(*chunks[0])
    for c, (h, off) in enumerate(chunks):
        s, bias = nxt
        if c + 1 < len(chunks):
            nxt = scores(*chunks[c + 1])
        p = _masked_softmax(s, bias > 0.5 * NEG)
        o_ref[0, off:off + ROW_CHUNK, h * HEAD_DIM:(h + 1) * HEAD_DIM] = _dot(p.astype(BF16), cv_ref[h // NSA_GROUP])
        key = (h // NSA_GROUP, off)
        imp[key] = p if imp[key] is None else imp[key] + p
    imp = jnp.concatenate(
        [jnp.concatenate([_split_dot(imp[(g, off)], pool) for off in range(0, tq, ROW_CHUNK)], axis=0)
         for g in range(NSA_KV_HEADS)], axis=1)
    pos = qt * tq + lax.broadcasted_iota(jnp.int32, imp.shape, 0)
    sel = _select_blocks(imp, pos, N_SEL, NSA_KV_HEADS)
    for g in range(NSA_KV_HEADS):
        sel_ref[0, g] = sel[:, g * nb:(g + 1) * nb]


def _prompt_cmp(q, kvf, w_cmp, bias_cmp, b, t):
    nc = t // CMP_STRIDE
    nb = nc // CMP_PER_SEL
    q3 = q.reshape(b, t, q.shape[-1])
    kv3 = kvf.reshape(b, t, kvf.shape[-1])
    return pl.pallas_call(
        functools.partial(_prompt_cmp_kernel, tq=TQ, nc=nc),
        grid=(b, t // TQ),
        in_specs=[pl.BlockSpec((1, TQ, NSA_WIDTH), lambda i, j: (i, j, 0)),
                  *[pl.BlockSpec((1, t, HEAD_DIM), lambda i, j, c=c: (i, 0, c)) for c in range(4)],
                  pl.BlockSpec((2, CMP_BLOCK, HEAD_DIM), lambda i, j: (0, 0, 0)),
                  pl.BlockSpec((NSA_HEADS, TQ, nc), lambda i, j: (0, j, 0))],
        out_specs=[pl.BlockSpec((1, TQ, NSA_WIDTH), lambda i, j: (i, j, 0)),
                   pl.BlockSpec((1, NSA_KV_HEADS, TQ, nb), lambda i, j: (i, 0, j, 0))],
        out_shape=[jax.ShapeDtypeStruct((b, t, NSA_WIDTH), F32),
                   jax.ShapeDtypeStruct((b, NSA_KV_HEADS, t, nb), F32)],
        scratch_shapes=[pltpu.VMEM((NSA_KV_HEADS, nc, HEAD_DIM), BF16),
                        pltpu.VMEM((NSA_KV_HEADS, nc, HEAD_DIM), BF16)],
        compiler_params=_cparams(("parallel", "arbitrary")),
        name="prompt_cmp",
    )(q3, kv3, kv3, kv3, kv3, w_cmp, bias_cmp)


def _prompt_selwin_kernel(q_ref, ks_ref, vs_ref, kw_ref, vw_ref, sel_ref, tb_ref, osel_ref, owin_ref,
                          m_ref, l_ref, acc_ref, *, tq, tk):
    i = pl.program_id(2)
    selb = sel_ref[0, 0].astype(BF16)
    nb = selb.shape[1]
    blocks_per_tile = tk // SEL_BLOCK
    erow = lax.broadcasted_iota(jnp.int32, (nb, tk), 0)
    ecol = lax.broadcasted_iota(jnp.int32, (nb, tk), 1) // SEL_BLOCK

    def tile(k_ref, v_ref, j, table, extra):
        start = pl.multiple_of(j * tk, tk)
        k = k_ref[0, pl.ds(start, tk), :]
        v = v_ref[0, pl.ds(start, tk), :]
        chunks = [(r, off) for r in range(NSA_GROUP) for off in range(0, tq, ROW_CHUNK)]

        def scores(r, off):
            bias = tb_ref[table, r, off:off + ROW_CHUNK, :]
            if extra is not None:
                bias = bias + extra[off:off + ROW_CHUNK]
            return _dot_nt(q_ref[0, off:off + ROW_CHUNK, r * HEAD_DIM:(r + 1) * HEAD_DIM], k) * SCALE2 + bias

        s_next = scores(*chunks[0])
        for c, (r, off) in enumerate(chunks):
            s = s_next
            if c + 1 < len(chunks):
                s_next = scores(*chunks[c + 1])
            rs = slice(r * tq + off, r * tq + off + ROW_CHUNK)
            _flash_tile(s, v, m_ref.at[rs], l_ref.at[rs], acc_ref.at[rs])

    def sel_tile(j, table):
        expand = jnp.where(erow == j * blocks_per_tile + ecol, 1.0, 0.0).astype(BF16)
        tile(ks_ref, vs_ref, j, table, (1.0 - _dot(selb, expand)) * NEG)

    def win_tile(j, table):
        tile(kw_ref, vw_ref, j, table, None)

    def finish(o_ref):
        for r in range(NSA_GROUP):
            rs = slice(r * tq, (r + 1) * tq)
            o_ref[0, :, r * HEAD_DIM:(r + 1) * HEAD_DIM] = acc_ref[rs] / l_ref[rs]

    _flash_init(m_ref, l_ref, acc_ref)

    def far_body(j, carry):
        sel_tile(j, 2)
        return carry

    lax.fori_loop(0, jnp.maximum(i - 1, 0), far_body, 0)

    @pl.when(i >= 1)
    def _():
        sel_tile(i - 1, 1)

    sel_tile(i, 0)
    finish(osel_ref)

    _flash_init(m_ref, l_ref, acc_ref)
    win_tile(i, 0)

    @pl.when(i >= 1)
    def _():
        win_tile(i - 1, 1)

    @pl.when(i >= 2)
    def _():
        win_tile(i - 2, 3)

    finish(owin_ref)


def _prompt_selwin(q, kvb, sel, tables, b, t):
    nb = sel.shape[-1]
    q3 = q.reshape(b, t, q.shape[-1])
    kv3 = kvb.reshape(b, t, kvb.shape[-1])
    gw = NSA_GROUP * HEAD_DIM
    kcol = lambda base: pl.BlockSpec((1, t, HEAD_DIM), lambda i, g, j: (i, 0, base + g))
    return pl.pallas_call(
        functools.partial(_prompt_selwin_kernel, tq=TQ, tk=TK),
        grid=(b, NSA_KV_HEADS, t // TQ),
        in_specs=[pl.BlockSpec((1, TQ, gw), lambda i, g, j: (i, j, g)),
                  kcol(4), kcol(6), kcol(8), kcol(10),
                  pl.BlockSpec((1, 1, TQ, nb), lambda i, g, j: (i, g, j, 0)),
                  pl.BlockSpec((4, NSA_GROUP, TQ, TK), lambda i, g, j: (0, g, 0, 0))],
        out_specs=[pl.BlockSpec((1, TQ, gw), lambda i, g, j: (i, j, g)),
                   pl.BlockSpec((1, TQ, gw), lambda i, g, j: (i, j, g))],
        out_shape=[jax.ShapeDtypeStruct((b, t, NSA_WIDTH), F32),
                   jax.ShapeDtypeStruct((b, t, NSA_WIDTH), F32)],
        scratch_shapes=[pltpu.VMEM((NSA_GROUP * TQ, HEAD_DIM), F32),
                        pltpu.VMEM((NSA_GROUP * TQ, HEAD_DIM), F32),
                        pltpu.VMEM((NSA_GROUP * TQ, HEAD_DIM), F32)],
        compiler_params=_cparams(("parallel", "parallel", "parallel")),
        name="prompt_selwin",
    )(q3, kv3, kv3, kv3, kv3, sel, tables)


def _sb_weights(z, carry, upper, valid):
    seg = upper.shape[0]
    ls = _log_sigmoid(z)
    l1m = ls - z
    if valid is not None:
        l1m = jnp.where(valid, l1m, 0.0)
    cums = []
    for sg in reversed(range(z.shape[1] // seg)):
        part = l1m[:, sg * seg:(sg + 1) * seg]
        cums.append(_split_dot(part, upper) + carry)
        carry = carry + jnp.sum(part, axis=1, keepdims=True)
    cum = cums[0] if len(cums) == 1 else jnp.concatenate(cums[::-1], axis=1)
    a = jnp.exp(ls + cum)
    if valid is not None:
        a = jnp.where(valid, a, 0.0)
    return a, carry


def _prompt_sb_kernel(q_ref, k_ref, v_ref, o_ref, acc_ref, carry_ref, *, tq, tk):
    i = pl.program_id(2)
    upper = _strict_upper(tk)
    acc_ref[...] = jnp.zeros(acc_ref.shape, F32)
    carry_ref[...] = jnp.zeros(carry_ref.shape, F32)

    def run(block, diag):
        start = pl.multiple_of(block * tq, tq)
        k = k_ref[0, pl.ds(start, tq), :]
        v = v_ref[0, pl.ds(start, tq), :]
        chunks = [slice(off, off + ROW_CHUNK) for off in range(0, tq, ROW_CHUNK)]
        accs = [acc_ref[rs, :] for rs in chunks]
        zs = [_dot_nt(q_ref[0, rs, :], k) * SCALE for rs in chunks]
        valids = [None] * len(chunks)
        if diag:
            valids = [(lax.broadcasted_iota(jnp.int32, (ROW_CHUNK, tq), 1)
                       < lax.broadcasted_iota(jnp.int32, (ROW_CHUNK, tq), 0) + rs.start) for rs in chunks]
        weights = [_sb_weights(z, carry_ref[rs, 0:1], upper, valid) for z, rs, valid in zip(zs, chunks, valids)]
        outs = [_dot(a.astype(BF16), v) for a, _ in weights]
        for rs, acc, o, (_, carry) in zip(chunks, accs, outs, weights):
            acc_ref[rs, :] = acc + o
            carry_ref[rs, :] = jnp.broadcast_to(carry, (ROW_CHUNK, HEAD_DIM))

    run(i, True)

    def body(jj, c):
        run(i - 1 - jj, False)
        return c

    lax.fori_loop(0, i, body, 0)
    o_ref[0] = acc_ref[...]


def _prompt_sb(q, kvb, b, t):
    q3 = q.reshape(b, t, q.shape[-1])
    kv3 = kvb.reshape(b, t, kvb.shape[-1])
    qbase = NSA_WIDTH // HEAD_DIM
    return pl.pallas_call(
        functools.partial(_prompt_sb_kernel, tq=TQ_SB, tk=TK),
        grid=(b, SB_HEADS, t // TQ_SB),
        in_specs=[pl.BlockSpec((1, TQ_SB, HEAD_DIM), lambda i, h, j: (i, j, qbase + h)),
                  pl.BlockSpec((1, t, HEAD_DIM), lambda i, h, j: (i, 0, 12 + h)),
                  pl.BlockSpec((1, t, HEAD_DIM), lambda i, h, j: (i, 0, 16 + h))],
        out_specs=pl.BlockSpec((1, TQ_SB, HEAD_DIM), lambda i, h, j: (i, j, h)),
        out_shape=jax.ShapeDtypeStruct((b, t, SB_WIDTH), F32),
        scratch_shapes=[pltpu.VMEM((TQ_SB, HEAD_DIM), F32), pltpu.VMEM((TQ_SB, HEAD_DIM), F32)],
        compiler_params=_cparams(("parallel", "parallel", "parallel")),
        name="prompt_sb",
    )(q3, kv3, kv3)


def _ssm_kernel(u_ref, h0_ref, wre_ref, wim_ref, cre_ref, cim_ref, d_ref, pre_ref, pim_ref,
                y_ref, hl_ref, hr_ref, hi_ref, *, t, tc):
    p8r = pre_ref[...]
    p8i = pim_ref[...]
    row = lax.broadcasted_iota(jnp.int32, (tc, SSM_TILE), 0) % SUBLANES

    def chunk(c, carry):
        cr, ci = carry
        start = pl.multiple_of(c * tc, tc)
        u = u_ref[0, pl.ds(start, tc), :]
        ub = u.astype(BF16)
        xr = _dot(ub, wre_ref[0])
        xi = _dot(ub, wim_ref[0])
        for sh in (1, 2, 4):
            ar = p8r[sh - 1:sh, :]
            ai = p8i[sh - 1:sh, :]
            sr = jnp.where(row >= sh, pltpu.roll(xr, sh, 0), 0.0)
            si = jnp.where(row >= sh, pltpu.roll(xi, sh, 0), 0.0)
            xr, xi = xr + ar * sr - ai * si, xi + ar * si + ai * sr
        for g in range(tc // SUBLANES):
            lo, hi = g * SUBLANES, (g + 1) * SUBLANES
            br = xr[lo:hi] + p8r * cr - p8i * ci
            bi = xi[lo:hi] + p8r * ci + p8i * cr
            hr_ref[lo:hi, :] = br
            hi_ref[lo:hi, :] = bi
            cr, ci = br[SUBLANES - 1:SUBLANES], bi[SUBLANES - 1:SUBLANES]
        y = (_dot(hr_ref[...].astype(BF16), cre_ref[0]) - _dot(hi_ref[...].astype(BF16), cim_ref[0])
             + d_ref[0] * u)
        y_ref[0, pl.ds(start, tc), :] = y
        return cr, ci

    cr, ci = lax.fori_loop(0, t // tc, chunk, (h0_ref[0, 0:1, :], h0_ref[0, 1:2, :]))
    hl_ref[0, 0:1, :] = cr
    hl_ref[0, 1:2, :] = ci


def _ssm(zc, h0, sw, b, t):
    wre, wim, cre, cim, dsk, pre, pim = sw
    tc = min(t, 256)
    u3 = zc.reshape(b, t, zc.shape[-1])
    ubase = (NSA_WIDTH + SB_WIDTH) // SSM_U_TILE
    wspec = lambda shp: pl.BlockSpec((1,) + shp, lambda i, k: (k, 0, 0))
    return pl.pallas_call(
        functools.partial(_ssm_kernel, t=t, tc=tc),
        grid=(b, SSM_TILES),
        in_specs=[pl.BlockSpec((1, t, SSM_U_TILE), lambda i, k: (i, 0, ubase + k)),
                  pl.BlockSpec((1, 2, SSM_TILE), lambda i, k: (i, 0, k)),
                  wspec((SSM_U_TILE, SSM_TILE)), wspec((SSM_U_TILE, SSM_TILE)),
                  wspec((SSM_TILE, SSM_U_TILE)), wspec((SSM_TILE, SSM_U_TILE)),
                  wspec((1, SSM_U_TILE)),
                  pl.BlockSpec((SUBLANES, SSM_TILE), lambda i, k: (0, k)),
                  pl.BlockSpec((SUBLANES, SSM_TILE), lambda i, k: (0, k))],
        out_specs=[pl.BlockSpec((1, t, SSM_U_TILE), lambda i, k: (i, 0, k)),
                   pl.BlockSpec((1, 2, SSM_TILE), lambda i, k: (i, 0, k))],
        out_shape=[jax.ShapeDtypeStruct((b, t, SSM_WIDTH), F32),
                   jax.ShapeDtypeStruct((b, 2, SSM_LANES), F32)],
        scratch_shapes=[pltpu.VMEM((tc, SSM_TILE), F32), pltpu.VMEM((tc, SSM_TILE), F32)],
        compiler_params=_cparams(("parallel", "parallel")),
        name="ssm_scan",
    )(u3, h0, wre, wim, cre, cim, dsk, pre, pim)


def _rms(h, gain):
    return h * lax.rsqrt(jnp.mean(h * h, axis=1, keepdims=True) + EPS) * gain


def _silu(z):
    return z * _sigmoid(z)


def _mix_kernel(ocmp_ref, osel_ref, owin_ref, gate_ref, ob_ref, yc_ref, za_ref, zb_ref, zc_ref, x_ref,
                wglu_ref, wout_ref, gain_ref, lng_ref, lnb_ref, y_ref, yb_ref, mixed_ref, *, alpha):
    gates = _sigmoid(gate_ref[...])
    for h in range(NSA_HEADS):
        sl = slice(h * HEAD_DIM, (h + 1) * HEAD_DIM)
        oa = (gates[:, 3 * h:3 * h + 1] * ocmp_ref[:, sl] + gates[:, 3 * h + 1:3 * h + 2] * osel_ref[:, sl]
              + gates[:, 3 * h + 2:3 * h + 3] * owin_ref[:, sl])
        mixed_ref[:, sl] = oa * _silu(za_ref[:, sl])
    mixed_ref[:, 0:NSA_WIDTH] = _rms(mixed_ref[:, 0:NSA_WIDTH], gain_ref[:, 0:NSA_WIDTH])
    b0, c0 = NSA_WIDTH, NSA_WIDTH + SB_WIDTH
    mixed_ref[:, b0:c0] = _rms(ob_ref[...] * _silu(zb_ref[...]), gain_ref[:, b0:c0])
    glu = _dot(yc_ref[...].astype(BF16), wglu_ref[...])
    yc = glu[:, 0:SSM_WIDTH] * _sigmoid(glu[:, SSM_WIDTH:2 * SSM_WIDTH])
    mixed_ref[:, c0:c0 + SSM_WIDTH] = _rms(yc * _silu(zc_ref[...]), gain_ref[:, c0:c0 + SSM_WIDTH])
    h = alpha * x_ref[...] + _dot(mixed_ref[...].astype(BF16), wout_ref[...])
    mu = jnp.mean(h, axis=1, keepdims=True)
    hc = h - mu
    var = jnp.mean(hc * hc, axis=1, keepdims=True)
    y = hc * lax.rsqrt(var + EPS) * lng_ref[...] + lnb_ref[...]
    y_ref[...] = y
    yb_ref[...] = y.astype(BF16)


def _mix_out(ocmp, osel, owin, gate, ob, yc, zc, x, wglu, wout, gain, lng, lnb, alpha):
    m, d = x.shape
    tm = min(m, 256)
    row = lambda w, c=0: pl.BlockSpec((tm, w), lambda i, c=c: (i, c))
    full = lambda a: pl.BlockSpec(a.shape, lambda i: (0,) * a.ndim)
    return pl.pallas_call(
        functools.partial(_mix_kernel, alpha=alpha),
        grid=(m // tm,),
        in_specs=[row(NSA_WIDTH), row(NSA_WIDTH), row(NSA_WIDTH), row(LANES), row(SB_WIDTH), row(SSM_WIDTH),
                  row(NSA_WIDTH, 0), row(SB_WIDTH, NSA_WIDTH // SB_WIDTH),
                  row(SSM_WIDTH, (NSA_WIDTH + SB_WIDTH + SSM_WIDTH) // SSM_WIDTH), row(d),
                  full(wglu), full(wout), full(gain), full(lng), full(lnb)],
        out_specs=[row(d), row(d)],
        out_shape=[jax.ShapeDtypeStruct((m, d), F32), jax.ShapeDtypeStruct((m, d), BF16)],
        scratch_shapes=[pltpu.VMEM((tm, d), F32)],
        compiler_params=_cparams(("parallel",)),
        name="mix_out",
    )(ocmp.reshape(m, -1), osel.reshape(m, -1), owin.reshape(m, -1), gate, ob.reshape(m, -1),
      yc.reshape(m, -1), zc, zc, zc, x, wglu, wout, gain, lng, lnb)


def _sample_compress_kernel(pt_ref, *refs):
    pages = refs[:PAGES_PER_STEP]
    w1_ref, w2_ref, first_ref, second_ref = refs[PAGES_PER_STEP:]
    page = pages[0].shape[2] // ROW_VECS
    per_page = page // CMP_STRIDE
    half = w1_ref.shape[1]
    pool = jnp.where(lax.broadcasted_iota(jnp.int32, (per_page, page), 1) // CMP_STRIDE
                     == lax.broadcasted_iota(jnp.int32, (per_page, page), 0), 1.0, 0.0).astype(BF16)
    for i, pg in enumerate(pages):
        rows = jnp.concatenate([_page_vec(pg, c, page) for c in range(2 * NSA_KV_HEADS)], axis=1)
        prod = jnp.concatenate([rows * w1_ref[...], rows * w2_ref[...]], axis=1)
        hi = prod.astype(BF16)
        lo = (prod - hi.astype(F32)).astype(BF16)
        sums = _dot(pool, hi) + _dot(pool, lo)
        first_ref[0, i * per_page:(i + 1) * per_page, :] = sums[:, 0:half]
        second_ref[0, i * per_page:(i + 1) * per_page, :] = sums[:, half:2 * half]


def _sample_compress(cache, layer, page_table, w_cmp):
    db, n_pages = page_table.shape
    page = cache.shape[2] // ROW_VECS
    per_page = page // CMP_STRIDE
    steps = n_pages // PAGES_PER_STEP
    nc = n_pages * per_page
    half = 4 * HEAD_DIM

    def pspec(i):
        return pl.BlockSpec((1, 1, page * ROW_VECS, HEAD_DIM),
                            lambda b, s, pt, i=i: (layer, pt[b, s * PAGES_PER_STEP + i], 0, 0))

    rows = PAGES_PER_STEP * per_page
    tiled = lambda w: jnp.concatenate([jnp.tile(w[kv], (per_page, 1)) for kv in (0, 0, 1, 1)], axis=1)
    w1, w2 = tiled(w_cmp[:, :CMP_STRIDE]), tiled(w_cmp[:, CMP_STRIDE:])
    return pl.pallas_call(
        _sample_compress_kernel,
        grid_spec=pltpu.PrefetchScalarGridSpec(
            num_scalar_prefetch=1,
            grid=(db, steps),
            in_specs=[pspec(i) for i in range(PAGES_PER_STEP)]
                     + [pl.BlockSpec((page, half), lambda b, s, pt: (0, 0)),
                        pl.BlockSpec((page, half), lambda b, s, pt: (0, 0))],
            out_specs=[pl.BlockSpec((1, rows, half), lambda b, s, pt: (b, s, 0)),
                       pl.BlockSpec((1, rows, half), lambda b, s, pt: (b, s, 0))]),
        out_shape=[jax.ShapeDtypeStruct((db, nc, half), F32), jax.ShapeDtypeStruct((db, nc, half), F32)],
        compiler_params=_cparams(("parallel", "parallel")),
        name="sample_compress",
    )(page_table, *([cache] * PAGES_PER_STEP), w1, w2)


def _sample_cmpwin_kernel(q_ref, first_ref, second_ref, bias_ref, win_ref, wbias_ref,
                          ocmp_ref, sel_ref, owin_ref, *, ts, nc, pos0):
    comp = first_ref[0] + pltpu.roll(second_ref[0], nc - 1, 0)
    nb = nc // CMP_PER_SEL
    pos = pos0 + lax.broadcasted_iota(jnp.int32, (ts, nb), 0)
    for g in range(NSA_KV_HEADS):
        qg = _stack_heads(q_ref[0, :, pl.ds(g * NSA_GROUP * HEAD_DIM, NSA_GROUP * HEAD_DIM)], HEAD_DIM)
        ck = comp[:, g * HEAD_DIM:(g + 1) * HEAD_DIM].astype(BF16)
        cv = comp[:, (2 + g) * HEAD_DIM:(3 + g) * HEAD_DIM].astype(BF16)
        bias = bias_ref[NSA_GROUP * g:NSA_GROUP * (g + 1)].reshape(NSA_GROUP * ts, nc)
        o, imp = _cmp_attend(qg, ck, cv, bias, ts)
        sel_ref[0, g] = _select_blocks(imp, pos, N_SEL - 1)
        kw = win_ref[0, :, g * HEAD_DIM:(g + 1) * HEAD_DIM].astype(BF16)
        vw = win_ref[0, :, (2 + g) * HEAD_DIM:(3 + g) * HEAD_DIM].astype(BF16)
        wb = wbias_ref[NSA_GROUP * g:NSA_GROUP * (g + 1)].reshape(NSA_GROUP * ts, kw.shape[0])
        sw = _dot_nt(qg, kw) * SCALE + wb
        ow = _dot(_masked_softmax(sw, wb > 0.5 * NEG).astype(BF16), vw)
        for r in range(NSA_GROUP):
            h = NSA_GROUP * g + r
            ocmp_ref[0, :, h * HEAD_DIM:(h + 1) * HEAD_DIM] = o[r * ts:(r + 1) * ts]
            owin_ref[0, :, h * HEAD_DIM:(h + 1) * HEAD_DIM] = ow[r * ts:(r + 1) * ts]


def _sample_cmpwin(q, first, second, bias_cmp, win_all, bias_win, db, ts, past_len):
    nc = first.shape[1]
    nb = nc // CMP_PER_SEL
    nw = win_all.shape[1]
    q3 = q.reshape(db, ts, q.shape[-1])
    return pl.pallas_call(
        functools.partial(_sample_cmpwin_kernel, ts=ts, nc=nc, pos0=past_len),
        grid=(db,),
        in_specs=[pl.BlockSpec((1, ts, NSA_WIDTH), lambda b: (b, 0, 0)),
                  pl.BlockSpec((1, nc, 4 * HEAD_DIM), lambda b: (b, 0, 0)),
                  pl.BlockSpec((1, nc, 4 * HEAD_DIM), lambda b: (b, 0, 0)),
                  pl.BlockSpec((NSA_HEADS, ts, nc), lambda b: (0, 0, 0)),
                  pl.BlockSpec((1, nw, 4 * HEAD_DIM), lambda b: (b, 0, 0)),
                  pl.BlockSpec((NSA_HEADS, ts, nw), lambda b: (0, 0, 0))],
        out_specs=[pl.BlockSpec((1, ts, NSA_WIDTH), lambda b: (b, 0, 0)),
                   pl.BlockSpec((1, NSA_KV_HEADS, ts, nb), lambda b: (b, 0, 0, 0)),
                   pl.BlockSpec((1, ts, NSA_WIDTH), lambda b: (b, 0, 0))],
        out_shape=[jax.ShapeDtypeStruct((db, ts, NSA_WIDTH), F32),
                   jax.ShapeDtypeStruct((db, NSA_KV_HEADS, ts, nb), F32),
                   jax.ShapeDtypeStruct((db, ts, NSA_WIDTH), F32)],
        compiler_params=_cparams(("parallel",)),
        name="sample_cmpwin",
    )(q3, first, second, bias_cmp, win_all, bias_win)


def _sample_sel_kernel(pt_ref, *refs, ts, page):
    pages = refs[:PAGES_PER_STEP]
    q_ref, sel_ref, new_ref, bias_ref, nbias_ref, o_ref, m_ref, l_ref, acc_ref = refs[PAGES_PER_STEP:]
    s_idx = pl.program_id(1)
    last = pl.num_programs(1) - 1
    rows = NSA_GROUP * ts
    nkeys = PAGES_PER_STEP * page
    blocks = nkeys // SEL_BLOCK

    @pl.when(s_idx == 0)
    def _():
        _flash_init(m_ref, l_ref, acc_ref)

    expand = jnp.where(lax.broadcasted_iota(jnp.int32, (blocks, nkeys), 0)
                       == lax.broadcasted_iota(jnp.int32, (blocks, nkeys), 1) // SEL_BLOCK, 1.0, 0.0).astype(BF16)
    for g in range(NSA_KV_HEADS):
        qg = _stack_heads(q_ref[0, :, pl.ds(g * NSA_GROUP * HEAD_DIM, NSA_GROUP * HEAD_DIM)], HEAD_DIM)
        k = jnp.concatenate([_page_vec(pg, 4 + g, page) for pg in pages], axis=0).astype(BF16)
        v = jnp.concatenate([_page_vec(pg, 6 + g, page) for pg in pages], axis=0).astype(BF16)
        bias = bias_ref[jnp.where(s_idx == last, 1, 0), NSA_GROUP * g:NSA_GROUP * (g + 1)].reshape(rows, nkeys)
        s = _dot_nt(qg, k) * SCALE2 + bias
        chosen = _dot(sel_ref[0, g, 0].astype(BF16), expand) > 0.5
        s = jnp.where(chosen[None], s.reshape(NSA_GROUP, ts, nkeys), NEG).reshape(rows, nkeys)
        sl = slice(g * rows, (g + 1) * rows)
        _flash_tile(s, v, m_ref.at[sl], l_ref.at[sl], acc_ref.at[sl])

        @pl.when(s_idx == last)
        def _():
            kn = _pad_keys(new_ref[0, :, g * HEAD_DIM:(g + 1) * HEAD_DIM])
            vn = _pad_keys(new_ref[0, :, (2 + g) * HEAD_DIM:(3 + g) * HEAD_DIM])
            nbias = nbias_ref[NSA_GROUP * g:NSA_GROUP * (g + 1)].reshape(rows, LANES)
            _flash_tile(_dot_nt(qg, kn) * SCALE2 + nbias, vn, m_ref.at[sl], l_ref.at[sl], acc_ref.at[sl])
            o = acc_ref[sl] / l_ref[sl]
            for r in range(NSA_GROUP):
                h = NSA_GROUP * g + r
                o_ref[0, :, h * HEAD_DIM:(h + 1) * HEAD_DIM] = o[r * ts:(r + 1) * ts]


def _sample_sel(cache, layer, page_table, q, sel_steps, kv_new, bias_steps, bias_new, db, ts):
    n_pages = page_table.shape[1]
    page = cache.shape[2] // ROW_VECS
    steps = n_pages // PAGES_PER_STEP
    half = 4 * HEAD_DIM
    nkeys = PAGES_PER_STEP * page
    blocks = nkeys // SEL_BLOCK
    q3 = q.reshape(db, ts, q.shape[-1])
    new3 = kv_new.reshape(db, ts, kv_new.shape[-1])

    def pspec(i):
        return pl.BlockSpec((1, 1, page * ROW_VECS, HEAD_DIM),
                            lambda b, s, pt, i=i: (layer, pt[b, s * PAGES_PER_STEP + i], 0, 0))

    return pl.pallas_call(
        functools.partial(_sample_sel_kernel, ts=ts, page=page),
        grid_spec=pltpu.PrefetchScalarGridSpec(
            num_scalar_prefetch=1,
            grid=(db, steps),
            in_specs=[pspec(i) for i in range(PAGES_PER_STEP)]
                     + [pl.BlockSpec((1, ts, NSA_WIDTH), lambda b, s, pt: (b, 0, 0)),
                        pl.BlockSpec((1, NSA_KV_HEADS, 1, ts, blocks), lambda b, s, pt: (b, 0, s, 0, 0)),
                        pl.BlockSpec((1, ts, half), lambda b, s, pt: (b, 0, 1)),
                        pl.BlockSpec((2, NSA_HEADS, ts, nkeys), lambda b, s, pt: (0, 0, 0, 0)),
                        pl.BlockSpec((NSA_HEADS, ts, LANES), lambda b, s, pt: (0, 0, 0))],
            out_specs=pl.BlockSpec((1, ts, NSA_WIDTH), lambda b, s, pt: (b, 0, 0)),
            scratch_shapes=[pltpu.VMEM((NSA_HEADS * ts, HEAD_DIM), F32),
                            pltpu.VMEM((NSA_HEADS * ts, HEAD_DIM), F32),
                            pltpu.VMEM((NSA_HEADS * ts, HEAD_DIM), F32)]),
        out_shape=jax.ShapeDtypeStruct((db, ts, NSA_WIDTH), F32),
        compiler_params=_cparams(("parallel", "arbitrary")),
        name="sample_sel",
    )(page_table, *([cache] * PAGES_PER_STEP), q3, sel_steps, new3, bias_steps, bias_new)


def _sample_sb_kernel(pt_ref, *refs, ts, page, seg):
    pages = refs[:PAGES_PER_STEP]
    q_ref, knew_ref, vnew_ref, o_ref, carry_ref, acc_ref = refs[PAGES_PER_STEP:]
    s_idx = pl.program_id(1)
    last = pl.num_programs(1) - 1
    upper = _strict_upper(seg)

    rows = SB_HEADS * ts
    heads = range(SB_HEADS)
    hs = lambda h: slice(h * HEAD_DIM, (h + 1) * HEAD_DIM)

    def attend(ks, vs, carry, up, valid):
        z = jnp.concatenate([_dot_nt(q_ref[0, :, hs(h)], ks[h]) for h in heads], axis=0) * SCALE
        a, carry = _sb_weights(z, carry, up, valid)
        return jnp.concatenate([_dot(a[h * ts:(h + 1) * ts].astype(BF16), vs[h]) for h in heads], axis=0), carry

    @pl.when(s_idx == 0)
    def _():
        kn = [_pad_keys(knew_ref[0, :, hs(h)]) for h in heads]
        vn = [_pad_keys(vnew_ref[0, :, hs(h)]) for h in heads]
        valid = (lax.broadcasted_iota(jnp.int32, (rows, LANES), 1)
                 < lax.broadcasted_iota(jnp.int32, (rows, LANES), 0) % ts)
        o, carry = attend(kn, vn, jnp.zeros((rows, 1), F32), _strict_upper(LANES), valid)
        acc_ref[...] = o
        carry_ref[...] = jnp.broadcast_to(carry, (rows, HEAD_DIM))

    ks = [jnp.concatenate([_page_vec(pg, h, page) for pg in pages], axis=0).astype(BF16) for h in heads]
    vs = [jnp.concatenate([_page_vec(pg, SB_HEADS + h, page) for pg in pages], axis=0).astype(BF16) for h in heads]
    o, carry = attend(ks, vs, carry_ref[:, 0:1], upper, None)
    acc_ref[...] = acc_ref[...] + o
    carry_ref[...] = jnp.broadcast_to(carry, (rows, HEAD_DIM))

    @pl.when(s_idx == last)
    def _():
        for h in heads:
            o_ref[0, :, hs(h)] = acc_ref[h * ts:(h + 1) * ts, :]


def _sample_sb(cache, layer, page_table, q, kv_new, db, ts):
    n_pages = page_table.shape[1]
    page = cache.shape[2] // ROW_VECS
    steps = n_pages // PAGES_PER_STEP
    q3 = q.reshape(db, ts, q.shape[-1])
    new3 = kv_new.reshape(db, ts, kv_new.shape[-1])

    def pspec(i):
        return pl.BlockSpec((1, 1, page * ROW_VECS, HEAD_DIM),
                            lambda b, s, pt, i=i: (layer, pt[b, (steps - 1 - s) * PAGES_PER_STEP + i], 0, 0))

    return pl.pallas_call(
        functools.partial(_sample_sb_kernel, ts=ts, page=page, seg=TK),
        grid_spec=pltpu.PrefetchScalarGridSpec(
            num_scalar_prefetch=1,
            grid=(db, steps),
            in_specs=[pspec(i) for i in range(PAGES_PER_STEP)]
                     + [pl.BlockSpec((1, ts, SB_WIDTH), lambda b, s, pt: (b, 0, NSA_WIDTH // SB_WIDTH)),
                        pl.BlockSpec((1, ts, SB_WIDTH), lambda b, s, pt: (b, 0, 3)),
                        pl.BlockSpec((1, ts, SB_WIDTH), lambda b, s, pt: (b, 0, 4))],
            out_specs=pl.BlockSpec((1, ts, SB_WIDTH), lambda b, s, pt: (b, 0, 0)),
            scratch_shapes=[pltpu.VMEM((SB_HEADS * ts, HEAD_DIM), F32),
                            pltpu.VMEM((SB_HEADS * ts, HEAD_DIM), F32)]),
        out_shape=jax.ShapeDtypeStruct((db, ts, SB_WIDTH), F32),
        compiler_params=_cparams(("parallel", "arbitrary")),
        name="sample_sb",
    )(page_table, *([cache] * PAGES_PER_STEP), q3, new3, new3)


def _t5_bucket(dist):
    n = jnp.maximum(dist, 0)
    exact = NUM_BUCKETS // 2
    nf = jnp.maximum(n, 1).astype(F32)
    large = exact + (jnp.log(nf / exact) / math.log(MAX_DISTANCE / exact) * (NUM_BUCKETS - exact)).astype(jnp.int32)
    return jnp.where(n < exact, n, jnp.minimum(large, NUM_BUCKETS - 1))


def _bias(rel_bias, delta, valid):
    bucket = _t5_bucket(delta)[None]
    tab = rel_bias.astype(F32).T.reshape((rel_bias.shape[1], NUM_BUCKETS) + (1,) * delta.ndim)
    b = jnp.zeros((rel_bias.shape[1],) + delta.shape, F32)
    for k in range(NUM_BUCKETS):
        b = jnp.where(bucket == k, tab[:, k], b)
    return jnp.where(valid[None], b, NEG)


def _prompt_tables(rel_bias, t):
    tt = jnp.arange(TQ)[:, None]
    ss = jnp.arange(TK)[None, :]
    d = tt - ss
    tables = jnp.stack([
        _bias(rel_bias, d, d >= 0),
        _bias(rel_bias, d + TK, d > -TK),
        _bias(rel_bias, d + 2 * TK, d > -2 * TK),
        _bias(rel_bias, d + 2 * TK, d + 2 * TK < WINDOW)
    ])
    nc = t // CMP_STRIDE
    dc = jnp.arange(t)[:, None] - (jnp.arange(nc)[None, :] * CMP_STRIDE + CMP_BLOCK - 1)
    return tables * LOG2E, _bias(rel_bias, dc, dc >= 0)


def _sample_tables(rel_bias, ts, past_len, page, nw, nw_pad):
    qpos = past_len + jnp.arange(ts)[:, None]
    nc = past_len // CMP_STRIDE
    dc = qpos - (jnp.arange(nc)[None, :] * CMP_STRIDE + CMP_BLOCK - 1)
    bias_cmp = _bias(rel_bias, dc, dc >= 0)
    widx = jnp.arange(nw_pad)[None, :]
    dw = qpos - (past_len - nw + widx)
    bias_win = _bias(rel_bias, dw, (dw >= 0) & (dw < WINDOW) & (widx < nw + ts))
    nkeys = PAGES_PER_STEP * page
    d_last = qpos - (past_len - nkeys + jnp.arange(nkeys)[None, :])
    d_far = d_last + nkeys
    bias_steps = jnp.stack([_bias(rel_bias, d_far, d_far >= 0), _bias(rel_bias, d_last, d_last >= 0)])
    nidx = jnp.arange(LANES)[None, :]
    dn = jnp.arange(ts)[:, None] - nidx
    return bias_cmp, bias_win, bias_steps * LOG2E, _bias(rel_bias, dn, (dn >= 0) & (nidx < ts)) * LOG2E


def _ssm_weights(lam_re, lam_im, log_dt, b_re, b_im, c_re, c_im, d_skip):
    lam_re, lam_im = lam_re.astype(F32), lam_im.astype(F32)
    dt = jnp.exp(log_dt.astype(F32))[:, None]
    mag = jnp.exp(lam_re * dt)
    a_re, a_im = mag * jnp.cos(lam_im * dt), mag * jnp.sin(lam_im * dt)
    den = lam_re * lam_re + lam_im * lam_im
    f_re = ((a_re - 1.0) * lam_re + a_im * lam_im) / den
    f_im = (a_im * lam_re - (a_re - 1.0) * lam_im) / den
    b_re, b_im = b_re.astype(F32), b_im.astype(F32)
    bb_re = f_re[..., None] * b_re - f_im[..., None] * b_im
    bb_im = f_re[..., None] * b_im + f_im[..., None] * b_re
    gpt = SSM_GROUPS // SSM_TILES
    eye = jnp.eye(gpt, dtype=F32)

    def in_mat(bb):
        return jnp.einsum('kgpn,gh->kgnhp', bb.reshape(SSM_TILES, gpt, SSM_STATE, SSM_CH), eye).reshape(
            SSM_TILES, SSM_U_TILE, SSM_TILE).astype(BF16)

    def out_mat(c):
        return jnp.einsum('kgnp,gh->khpgn', c.astype(F32).reshape(SSM_TILES, gpt, SSM_CH, SSM_STATE), eye).reshape(
            SSM_TILES, SSM_TILE, SSM_U_TILE).astype(BF16)

    pr, pi = a_re.reshape(1, SSM_LANES), a_im.reshape(1, SSM_LANES)
    while pr.shape[0] < SUBLANES:
        tr, ti = pr[-1:], pi[-1:]
        pr, pi = (jnp.concatenate([pr, pr * tr - pi * ti]), jnp.concatenate([pi, pr * ti + pi * tr]))
    return (in_mat(bb_re), in_mat(bb_im), out_mat(c_re), out_mat(c_im),
            d_skip.astype(F32).reshape(SSM_TILES, 1, SSM_U_TILE), pr, pi)


def _split_w_in(w_in):
    sizes = (NSA_WIDTH, 6 * NSA_KV_HEADS * HEAD_DIM, 3 * NSA_HEADS, NSA_WIDTH,
             SB_WIDTH, SB_WIDTH, SB_WIDTH, SB_WIDTH, SSM_WIDTH, SSM_WIDTH)
    offs = [0]
    for s in sizes:
        offs.append(offs[-1] + s)
    q_a, kv_a, g_a, z_a, q_b, k_b, v_b, z_b, u_c, z_c = (w_in[..., offs[i]:offs[i + 1]] for i in range(10))
    cat = lambda xs: jnp.concatenate(xs, axis=-1).astype(BF16)
    pad = jnp.zeros(g_a.shape[:-1] + (LANES - g_a.shape[-1],), g_a.dtype)
    return cat([q_a, q_b]), cat([kv_a, k_b, v_b]), cat([z_a, z_b, u_c, z_c]), cat([g_a, pad])


KV_TILE = 4 * HEAD_DIM


def _kv_proj_kernel(*refs, tm):
    x_ref, w_ref = refs[:2]
    kvf_ref, kvb_ref, nsa_ref, sb_ref = refs[-4:]
    j = pl.program_id(1)
    acc = _dot(x_ref[...], w_ref[...])
    kvf_ref[...] = acc
    kvb_ref[...] = acc.astype(BF16)

    def scatter(dst, base):
        for c in range(KV_TILE // HEAD_DIM):
            dst[0, pl.ds(base + c, tm, stride=ROW_VECS), :] = acc[:, c * HEAD_DIM:(c + 1) * HEAD_DIM]

    for step, dst, base in ((0, nsa_ref, 0), (1, nsa_ref, 4), (3, sb_ref, 0), (4, sb_ref, 4)):
        pl.when(j == step)(functools.partial(scatter, dst, base))


def _kv_proj(x, w, layer, depth, stacked):
    m, k = x.shape
    n = w.shape[1]
    tm = min(m, 1024)
    assert n == 5 * KV_TILE
    rows = pl.BlockSpec((1, tm * ROW_VECS, HEAD_DIM), lambda i, j: (layer, i, 0))
    stack_shape = jax.ShapeDtypeStruct((depth, m * ROW_VECS, HEAD_DIM), F32)
    return pl.pallas_call(
        functools.partial(_kv_proj_kernel, tm=tm),
        grid=(m // tm, n // KV_TILE),
        in_specs=[pl.BlockSpec((tm, k), lambda i, j: (i, 0)), pl.BlockSpec((k, KV_TILE), lambda i, j: (0, j)),
                  pl.BlockSpec(memory_space=pl.ANY), pl.BlockSpec(memory_space=pl.ANY)],
        out_specs=[pl.BlockSpec((tm, KV_TILE), lambda i, j: (i, j)), pl.BlockSpec((tm, KV_TILE), lambda i, j: (i, j)),
                   rows, rows],
        out_shape=[jax.ShapeDtypeStruct((m, n), F32), jax.ShapeDtypeStruct((m, n), BF16), stack_shape, stack_shape],
        input_output_aliases={2: 2, 3: 3},
        compiler_params=_cparams(("parallel", "arbitrary")),
        name="in_proj_kv",
    )(x, w, *stacked)


def _in_proj(xb, wl, layer, depth, stacked):
    wa, wb, wc, wd = wl
    (q,) = _matmul(xb, wa, (BF16,), "in_proj_q")
    kvf, kvb, nsa, sb = _kv_proj(xb, wb, layer, depth, stacked)
    (zc,) = _matmul(xb, wc, (F32,), "in_proj_z")
    (gate,) = _matmul(xb, wd, (F32,), "in_proj_gate")
    return q, kvf, kvb, zc, gate, (nsa, sb)


def kernel(x_prompt, x_sample, cache_nsa, cache_sb, state_win, state_ssm, page_table, rel_bias, w_in, w_cmp,
           ssm_lam_re, ssm_lam_im, ssm_log_dt, ssm_b_re, ssm_b_im, ssm_c_re, ssm_c_im, ssm_d, w_glu, mix_gain,
           w_out, ln_g, ln_b):
    depth = w_in.shape[0]
    b, t, d = x_prompt.shape
    db, ts, _ = x_sample.shape
    n_phys, page = cache_nsa.shape[1], cache_nsa.shape[2]
    n_pages = page_table.shape[1]
    past_len = n_pages * page
    nw = state_win.shape[2]
    alpha = (2 * depth) ** 0.25
    assert t % TQ == 0 and t % TQ_SB == 0 and TQ_SB % TK == 0 and WINDOW == 2 * TK and TQ == TK and n_pages % PAGES_PER_STEP == 0
    assert past_len % SEL_BLOCK == 0 and ts % SUBLANES == 0 and ts <= SEL_BLOCK and nw == WINDOW

    w_groups = _split_w_in(w_in)
    w_glu_b, w_out_b = w_glu.astype(BF16), w_out.astype(BF16)
    ssm_w = jax.vmap(_ssm_weights)(ssm_lam_re, ssm_lam_im, ssm_log_dt, ssm_b_re, ssm_b_im, ssm_c_re, ssm_c_im, ssm_d)
    tables_p, bias_cmp_p = _prompt_tables(rel_bias, t)
    nw_pad = -(-(nw + ts) // LANES) * LANES
    bias_cmp_s, bias_win_s, bias_steps_s, bias_new_s = _sample_tables(rel_bias, ts, past_len, page, nw, nw_pad)
    cache_nsa_r = cache_nsa.reshape(depth, n_phys, page * ROW_VECS, HEAD_DIM)
    cache_sb_r = cache_sb.reshape(depth, n_phys, page * ROW_VECS, HEAD_DIM)
    w_cmp = w_cmp.astype(F32)
    page_table = page_table.astype(jnp.int32)

    xp, xs = x_prompt.reshape(b * t, d), x_sample.reshape(db * ts, d)
    xpb, xsb = xp.astype(BF16), xs.astype(BF16)
    outs = [[] for _ in range(8)]
    rows_p = (jnp.zeros((depth, b * t * ROW_VECS, HEAD_DIM), F32),) * 2
    rows_s = (jnp.zeros((depth, db * ts * ROW_VECS, HEAD_DIM), F32),) * 2
    for l in range(depth):
        wl = tuple(w[l] for w in w_groups)
        sw = tuple(w[l] for w in ssm_w)
        gain, lng, lnb = mix_gain[l].reshape(1, d), ln_g[l].reshape(1, d), ln_b[l].reshape(1, d)

        q, kvf, kvb, zc, gate, rows_p = _in_proj(xpb, wl, l, depth, rows_p)
        o_cmp, sel = _prompt_cmp(q, kvf, w_cmp[l], bias_cmp_p, b, t)
        o_sel, o_win = _prompt_selwin(q, kvb, sel, tables_p, b, t)
        o_b = _prompt_sb(q, kvb, b, t)
        y_c, h_p = _ssm(zc, jnp.zeros((b, 2, SSM_LANES), F32), sw, b, t)
        xp, xpb = _mix_out(o_cmp, o_sel, o_win, gate, o_b, y_c, zc, xp, w_glu_b[l], w_out_b[l], gain, lng, lnb,
                           alpha)
        kv3 = kvf.reshape(b, t, -1)
        outs[4].append(kv3[:, max(t - WINDOW, 0):, 1024:1536].reshape(b, -1, 2, NSA_KV_HEADS, HEAD_DIM))
        outs[6].append(h_p.reshape(b, 2, SSM_GROUPS, SSM_STATE))

        q, kvf, kvb, zc, gate, rows_s = _in_proj(xsb, wl, l, depth, rows_s)
        kv3 = kvf.reshape(db, ts, -1)
        win_all = jnp.concatenate([state_win[l].reshape(db, nw, -1), kv3[:, :, 1024:1536],
                                   jnp.zeros((db, nw_pad - nw - ts, 4 * HEAD_DIM), F32)], axis=1)
        first, second = _sample_compress(cache_nsa_r, l, page_table, w_cmp[l])
        o_cmp, sel, o_win = _sample_cmpwin(q, first, second, bias_cmp_s, win_all, bias_win_s, db, ts, past_len)
        steps = n_pages // PAGES_PER_STEP
        sel_steps = jnp.moveaxis(sel.reshape(db, NSA_KV_HEADS, ts, steps, -1), 3, 2)
        o_sel = _sample_sel(cache_nsa_r, l, page_table, q, sel_steps, kvf, bias_steps_s, bias_new_s, db, ts)
        o_b = _sample_sb(cache_sb_r, l, page_table, q, kvf, db, ts)
        y_c, h_s = _ssm(zc, state_ssm[l].reshape(db, 2, SSM_LANES).astype(F32), sw, db, ts)
        xs, xsb = _mix_out(o_cmp, o_sel, o_win, gate, o_b, y_c, zc, xs, w_glu_b[l], w_out_b[l], gain, lng, lnb,
                           alpha)
        outs[5].append(win_all[:, ts:ts + nw].reshape(db, nw, 2, NSA_KV_HEADS, HEAD_DIM))
        outs[7].append(h_s.reshape(db, 2, SSM_GROUPS, SSM_STATE))

    win_p, win_s, ssm_p, ssm_s = (jnp.stack(outs[i]) for i in (4, 5, 6, 7))
    return (xp.reshape(b, t, d), xs.reshape(db, ts, d),
            rows_p[0].reshape(depth, b, t, 4, NSA_KV_HEADS, HEAD_DIM),
            rows_s[0].reshape(depth, db, ts, 4, NSA_KV_HEADS, HEAD_DIM),
            rows_p[1].reshape(depth, b, t, 2, SB_HEADS, HEAD_DIM),
            rows_s[1].reshape(depth, db, ts, 2, SB_HEADS, HEAD_DIM),
            win_p, win_s, ssm_p, ssm_s)
```

```python
import functools
import math

import jax
import jax.numpy as jnp
from jax import lax
from jax.experimental import pallas as pl
from jax.experimental.pallas import tpu as pltpu

F32 = jnp.float32
BF16 = jnp.bfloat16

HEAD_DIM = 128
NSA_KV_HEADS = 2
NSA_GROUP = 4
NSA_HEADS = NSA_KV_HEADS * NSA_GROUP
NSA_WIDTH = NSA_HEADS * HEAD_DIM
SB_HEADS = 4
SB_WIDTH = SB_HEADS * HEAD_DIM
SSM_CH = 16
SSM_GROUPS = 32
SSM_STATE = 64
SSM_WIDTH = SSM_CH * SSM_GROUPS
SSM_LANES = SSM_GROUPS * SSM_STATE
CMP_BLOCK = 32
CMP_STRIDE = 16
SEL_BLOCK = 64
CMP_PER_SEL = SEL_BLOCK // CMP_STRIDE
N_SEL = 16
WINDOW = 512
NUM_BUCKETS = 32
MAX_DISTANCE = 128
FORCE_SCORE = 1e4
EPS = 1e-5
SCALE = HEAD_DIM ** -0.5
LOG2E = math.log2(math.e)
SCALE2 = SCALE * LOG2E
NEG = -1e30

LANES = 128
SUBLANES = 8
VMEM_LIMIT = 48 * 1024 * 1024

TQ = 256
TK = 256
ROW_CHUNK = 128
TQ_SB = 512
PAGES_PER_STEP = 8
ROW_VECS = 8
SSM_TILE = 512
SSM_TILES = SSM_LANES // SSM_TILE
SSM_U_TILE = SSM_WIDTH // SSM_TILES


def _cparams(sem):
    return pltpu.CompilerParams(dimension_semantics=sem, vmem_limit_bytes=VMEM_LIMIT)


def _dot(a, b):
    return jnp.dot(a, b, preferred_element_type=F32)


def _dot_nt(a, b):
    return lax.dot_general(a, b, (((1,), (1,)), ((), ())), preferred_element_type=F32)


def _split_dot(x, w):
    hi = x.astype(BF16)
    lo = (x - hi.astype(F32)).astype(BF16)
    return _dot(hi, w) + _dot(lo, w)


def _sigmoid(x):
    return 1.0 / (1.0 + jnp.exp(-x))


def _lane_tile(x, n):
    return x if n == 1 else jnp.concatenate([x] * n, axis=1)


def _mm_kernel(x_ref, w_ref, *o_refs):
    acc = _dot(x_ref[...], w_ref[...])
    for o in o_refs:
        o[...] = acc.astype(o.dtype)


def _matmul(x, w, out_dtypes, name):
    m, k = x.shape
    n = w.shape[1]
    tm = min(m, 1024)
    tn = min(n, 512)
    outs = pl.pallas_call(
        _mm_kernel,
        grid=(m // tm, n // tn),
        in_specs=[pl.BlockSpec((tm, k), lambda i, j: (i, 0)),
                  pl.BlockSpec((k, tn), lambda i, j: (0, j))],
        out_specs=[pl.BlockSpec((tm, tn), lambda i, j: (i, j)) for _ in out_dtypes],
        out_shape=[jax.ShapeDtypeStruct((m, n), d) for d in out_dtypes],
        compiler_params=_cparams(("parallel", "parallel")),
        name=name,
    )(x, w)
    return outs


def _masked_softmax(s, valid):
    s = jnp.where(valid, s, NEG)
    m = jnp.max(s, axis=1, keepdims=True)
    e = jnp.where(valid, jnp.exp(s - m), 0.0)
    return e / jnp.maximum(jnp.sum(e, axis=1, keepdims=True), 1e-30)


def _masked_softmax2(s, valid):
    e = jnp.where(valid, jnp.exp2(s - jnp.max(s, axis=1, keepdims=True)), 0.0)
    return e * (1.0 / jnp.maximum(jnp.sum(e, axis=1, keepdims=True), 1e-30))


def _cmp_attend(qg, ck, cv, bias, rows):
    s = _dot_nt(qg, ck) * SCALE + bias
    p = _masked_softmax(s, bias > 0.5 * NEG)
    o = _dot(p.astype(BF16), cv)
    imp = p[0:rows] + p[rows:2 * rows] + p[2 * rows:3 * rows] + p[3 * rows:4 * rows]
    nc = ck.shape[0]
    nb = nc // CMP_PER_SEL
    grp = (lax.broadcasted_iota(jnp.int32, (nc, nb), 0) // CMP_PER_SEL
           == lax.broadcasted_iota(jnp.int32, (nc, nb), 1))
    return o, _split_dot(imp, jnp.where(grp, 1.0, 0.0).astype(BF16))


def _select_blocks(imp, pos, k_top, groups=1):
    rows, width = imp.shape
    nb = width // groups
    lane = lax.broadcasted_iota(jnp.int32, (rows, width), 1)
    blk = lane % nb
    cur = pos // SEL_BLOCK
    score = jnp.where(blk * SEL_BLOCK <= pos, imp, -1.0)
    score = jnp.where(blk == 0, FORCE_SCORE, score)
    score = jnp.where(blk == cur, FORCE_SCORE, score)
    score = jnp.where(blk == cur - 1, FORCE_SCORE, score)
    cnt = jnp.zeros((rows, width), F32)
    for i in range(nb):
        col = score[:, i:i + 1]
        for g in range(1, groups):
            col = jnp.where(lane >= g * nb, score[:, g * nb + i:g * nb + i + 1], col)
        ge = jnp.where(col >= score, 1.0, 0.0)
        gt = jnp.where(col > score, 1.0, 0.0)
        cnt = cnt + jnp.where(blk > i, ge, gt)
    return jnp.where(cnt < k_top, 1.0, 0.0)


def _flash_init(m_ref, l_ref, acc_ref):
    m_ref[...] = jnp.full(m_ref.shape, NEG, F32)
    l_ref[...] = jnp.zeros(l_ref.shape, F32)
    acc_ref[...] = jnp.zeros(acc_ref.shape, F32)


def _flash_tile(s, v, m_ref, l_ref, acc_ref):
    m, l, acc = _flash_update(s, v, m_ref[...], l_ref[...], acc_ref[...])
    m_ref[...] = m
    l_ref[...] = l
    acc_ref[...] = acc


def _flash_update(s, v, m_prev, l_prev, acc_prev):
    m_new = jnp.maximum(m_prev, jnp.max(s, axis=1, keepdims=True))
    alpha = jnp.exp2(m_prev - m_new)
    p = jnp.exp2(s - _lane_tile(m_new, s.shape[1] // LANES))
    return (m_new, alpha * l_prev + jnp.sum(p, axis=1, keepdims=True),
            alpha * acc_prev + _dot(p.astype(BF16), v))


def _page_vec(pg, c, page):
    return pg[0, 0, pl.ds(c, page, stride=ROW_VECS), :]


def _pad_keys(x):
    pad = jnp.zeros((LANES - x.shape[0], x.shape[1]), F32)
    return jnp.concatenate([x, pad], axis=0).astype(BF16)


def _stack_heads(q_ref_slice, width):
    return jnp.concatenate([q_ref_slice[:, HEAD_DIM * r:HEAD_DIM * (r + 1)] for r in range(NSA_GROUP)], axis=0)


def _log2_sigmoid(z2):
    return jnp.minimum(z2, 0.0) - jnp.log(1.0 + jnp.exp2(-jnp.abs(z2))) * LOG2E


def _strict_upper(n):
    return jnp.where(lax.broadcasted_iota(jnp.int32, (n, n), 0) > lax.broadcasted_iota(jnp.int32, (n, n), 1),
                     1.0, 0.0).astype(BF16)


def _prompt_cmp_kernel(q_ref, k0_ref, k1_ref, v0_ref, v1_ref, w_ref, bias_ref, o_ref, sel_ref, ck_ref, cv_ref,
                       *, tq, nc):
    qt = pl.program_id(1)

    @pl.when(qt == 0)
    def _():
        for g in range(NSA_KV_HEADS):
            for kv, src, dst in ((0, (k0_ref, k1_ref)[g], ck_ref), (1, (v0_ref, v1_ref)[g], cv_ref)):
                first = jnp.zeros((nc, HEAD_DIM), F32)
                second = jnp.zeros((nc, HEAD_DIM), F32)
                for j in range(CMP_STRIDE):
                    rows = src[0, pl.ds(j, nc, stride=CMP_STRIDE), :]
                    first = first + rows * w_ref[kv, j:j + 1, :]
                    second = second + rows * w_ref[kv, CMP_STRIDE + j:CMP_STRIDE + j + 1, :]
                dst[g] = (first + pltpu.roll(second, nc - 1, 0)).astype(BF16)

    nb = nc // CMP_PER_SEL
    pool = jnp.where(lax.broadcasted_iota(jnp.int32, (nc, nb), 0) // CMP_PER_SEL
                     == lax.broadcasted_iota(jnp.int32, (nc, nb), 1), 1.0, 0.0).astype(BF16)
    chunks = [(h, off) for h in range(NSA_HEADS) for off in range(0, tq, ROW_CHUNK)]

    def scores(h, off):
        bias = bias_ref[h, off:off + ROW_CHUNK, :]
        q = q_ref[0, off:off + ROW_CHUNK, h * HEAD_DIM:(h + 1) * HEAD_DIM]
        return _dot_nt(q, ck_ref[h // NSA_GROUP]) * SCALE2 + bias, bias

    imp = {(g, off): None for g in range(NSA_KV_HEADS) for off in range(0, tq, ROW_CHUNK)}
    nxt = scores(*chunks[0])
    for c, (h, off) in enumerate(chunks):
        s, bias = nxt
        if c + 1 < len(chunks):
            nxt = scores(*chunks[c + 1])
        p = _masked_softmax2(s, bias > 0.5 * NEG)
        o_ref[0, off:off + ROW_CHUNK, h * HEAD_DIM:(h + 1) * HEAD_DIM] = _dot(p.astype(BF16), cv_ref[h // NSA_GROUP])
        key = (h // NSA_GROUP, off)
        imp[key] = p if imp[key] is None else imp[key] + p
    imp = jnp.concatenate(
        [jnp.concatenate([_split_dot(imp[(g, off)], pool) for off in range(0, tq, ROW_CHUNK)], axis=0)
         for g in range(NSA_KV_HEADS)], axis=1)
    pos = qt * tq + lax.broadcasted_iota(jnp.int32, imp.shape, 0)
    sel = _select_blocks(imp, pos, N_SEL, NSA_KV_HEADS)
    for g in range(NSA_KV_HEADS):
        sel_ref[0, g] = sel[:, g * nb:(g + 1) * nb]


def _prompt_cmp(q, kvf, w_cmp, bias_cmp, b, t):
    nc = t // CMP_STRIDE
    nb = nc // CMP_PER_SEL
    q3 = q.reshape(b, t, q.shape[-1])
    kv3 = kvf.reshape(b, t, kvf.shape[-1])
    return pl.pallas_call(
        functools.partial(_prompt_cmp_kernel, tq=TQ, nc=nc),
        grid=(b, t // TQ),
        in_specs=[pl.BlockSpec((1, TQ, NSA_WIDTH), lambda i, j: (i, j, 0)),
                  *[pl.BlockSpec((1, t, HEAD_DIM), lambda i, j, c=c: (i, 0, c)) for c in range(4)],
                  pl.BlockSpec((2, CMP_BLOCK, HEAD_DIM), lambda i, j: (0, 0, 0)),
                  pl.BlockSpec((NSA_HEADS, TQ, nc), lambda i, j: (0, j, 0))],
        out_specs=[pl.BlockSpec((1, TQ, NSA_WIDTH), lambda i, j: (i, j, 0)),
                   pl.BlockSpec((1, NSA_KV_HEADS, TQ, nb), lambda i, j: (i, 0, j, 0))],
        out_shape=[jax.ShapeDtypeStruct((b, t, NSA_WIDTH), F32),
                   jax.ShapeDtypeStruct((b, NSA_KV_HEADS, t, nb), F32)],
        scratch_shapes=[pltpu.VMEM((NSA_KV_HEADS, nc, HEAD_DIM), BF16),
                        pltpu.VMEM((NSA_KV_HEADS, nc, HEAD_DIM), BF16)],
        compiler_params=_cparams(("parallel", "arbitrary")),
        name="prompt_cmp",
    )(q3, kv3, kv3, kv3, kv3, w_cmp, bias_cmp)


def _prompt_selwin_kernel(q_ref, ks_ref, vs_ref, kw_ref, vw_ref, sel_ref, tb_ref, osel_ref, owin_ref,
                          m_ref, l_ref, acc_ref, *, tq, tk):
    i = pl.program_id(2)
    selb = sel_ref[0, 0].astype(BF16)
    nb = selb.shape[1]
    blocks_per_tile = tk // SEL_BLOCK

    def tile(k_ref, v_ref, j, n, table, extra):
        start = pl.multiple_of(j * tk, tk)
        k = k_ref[0, pl.ds(start, n * tk), :]
        v = v_ref[0, pl.ds(start, n * tk), :]
        chunks = [(r, off) for r in range(NSA_GROUP) for off in range(0, tq, ROW_CHUNK)]

        def scores(r, off):
            s = _dot_nt(q_ref[0, off:off + ROW_CHUNK, r * HEAD_DIM:(r + 1) * HEAD_DIM], k) * SCALE2
            if table is not None:
                s = s + tb_ref[table, r, off:off + ROW_CHUNK, :]
            if extra is not None:
                s = s + extra[off:off + ROW_CHUNK]
            return s

        rows = [slice(r * tq + off, r * tq + off + ROW_CHUNK) for r, off in chunks]
        state = [(m_ref[rs], l_ref[rs], acc_ref[rs]) for rs in rows]
        s_next = scores(*chunks[0])
        for c in range(len(chunks)):
            s = s_next
            if c + 1 < len(chunks):
                s_next = scores(*chunks[c + 1])
            state[c] = _flash_update(s, v, *state[c])
        for rs, (m, l, acc) in zip(rows, state):
            m_ref[rs] = m
            l_ref[rs] = l
            acc_ref[rs] = acc

    def sel_tile(j, n, table):
        erow = lax.broadcasted_iota(jnp.int32, (nb, n * tk), 0)
        ecol = lax.broadcasted_iota(jnp.int32, (nb, n * tk), 1) // SEL_BLOCK
        expand = jnp.where(erow == j * blocks_per_tile + ecol, 1.0, 0.0).astype(BF16)
        tile(ks_ref, vs_ref, j, n, table, (1.0 - _dot(selb, expand)) * NEG)

    def win_tile(j, table):
        tile(kw_ref, vw_ref, j, 1, table, None)

    def finish(o_ref):
        for r in range(NSA_GROUP):
            rs = slice(r * tq, (r + 1) * tq)
            o_ref[0, :, r * HEAD_DIM:(r + 1) * HEAD_DIM] = acc_ref[rs] / l_ref[rs]

    _flash_init(m_ref, l_ref, acc_ref)

    n_far = jnp.maximum(i - 1, 0)

    def far_body(jj, carry):
        sel_tile(2 * jj, 2, None)
        return carry

    lax.fori_loop(0, n_far // 2, far_body, 0)

    @pl.when(n_far % 2 == 1)
    def _():
        sel_tile(i - 2, 1, None)

    @pl.when(i >= 1)
    def _():
        sel_tile(i - 1, 1, 1)

    sel_tile(i, 1, 0)
    finish(osel_ref)

    _flash_init(m_ref, l_ref, acc_ref)
    win_tile(i, 0)

    @pl.when(i >= 1)
    def _():
        win_tile(i - 1, 1)

    @pl.when(i >= 2)
    def _():
        win_tile(i - 2, 2)

    finish(owin_ref)


def _prompt_selwin(q, kvb, sel, tables, b, t):
    nb = sel.shape[-1]
    q3 = q.reshape(b, t, q.shape[-1])
    kv3 = kvb.reshape(b, t, kvb.shape[-1])
    gw = NSA_GROUP * HEAD_DIM
    kcol = lambda base: pl.BlockSpec((1, t, HEAD_DIM), lambda i, g, j: (i, 0, base + g))
    return pl.pallas_call(
        functools.partial(_prompt_selwin_kernel, tq=TQ, tk=TK),
        grid=(b, NSA_KV_HEADS, t // TQ),
        in_specs=[pl.BlockSpec((1, TQ, gw), lambda i, g, j: (i, j, g)),
                  kcol(4), kcol(6), kcol(8), kcol(10),
                  pl.BlockSpec((1, 1, TQ, nb), lambda i, g, j: (i, g, j, 0)),
                  pl.BlockSpec((3, NSA_GROUP, TQ, TK), lambda i, g, j: (0, g, 0, 0))],
        out_specs=[pl.BlockSpec((1, TQ, gw), lambda i, g, j: (i, j, g)),
                   pl.BlockSpec((1, TQ, gw), lambda i, g, j: (i, j, g))],
        out_shape=[jax.ShapeDtypeStruct((b, t, NSA_WIDTH), F32),
                   jax.ShapeDtypeStruct((b, t, NSA_WIDTH), F32)],
        scratch_shapes=[pltpu.VMEM((NSA_GROUP * TQ, HEAD_DIM), F32),
                        pltpu.VMEM((NSA_GROUP * TQ, HEAD_DIM), F32),
                        pltpu.VMEM((NSA_GROUP * TQ, HEAD_DIM), F32)],
        compiler_params=_cparams(("parallel", "parallel", "parallel")),
        name="prompt_selwin",
    )(q3, kv3, kv3, kv3, kv3, sel, tables)


def _sb_weights(z, carry, upper, valid):
    seg = upper.shape[0]
    ls = _log2_sigmoid(z)
    l1m = ls - z
    if valid is not None:
        l1m = jnp.where(valid, l1m, 0.0)
    cums = []
    for sg in reversed(range(z.shape[1] // seg)):
        part = l1m[:, sg * seg:(sg + 1) * seg]
        cums.append(_split_dot(part, upper) + carry)
        carry = carry + jnp.sum(part, axis=1, keepdims=True)
    cum = cums[0] if len(cums) == 1 else jnp.concatenate(cums[::-1], axis=1)
    a = jnp.exp2(ls + cum)
    if valid is not None:
        a = jnp.where(valid, a, 0.0)
    return a, carry


def _prompt_sb_kernel(q_ref, k_ref, v_ref, o_ref, acc_ref, carry_ref, *, tq, tk):
    i = pl.program_id(2)
    upper = _strict_upper(tk)
    acc_ref[...] = jnp.zeros(acc_ref.shape, F32)
    carry_ref[...] = jnp.zeros(carry_ref.shape, F32)

    def run(block, diag):
        start = pl.multiple_of(block * tq, tq)
        k = k_ref[0, pl.ds(start, tq), :]
        v = v_ref[0, pl.ds(start, tq), :]
        chunks = [slice(off, off + ROW_CHUNK) for off in range(0, tq, ROW_CHUNK)]
        accs = [acc_ref[rs, :] for rs in chunks]
        zs = [_dot_nt(q_ref[0, rs, :], k) * SCALE2 for rs in chunks]
        valids = [None] * len(chunks)
        if diag:
            valids = [(lax.broadcasted_iota(jnp.int32, (ROW_CHUNK, tq), 1)
                       < lax.broadcasted_iota(jnp.int32, (ROW_CHUNK, tq), 0) + rs.start) for rs in chunks]
        weights = [_sb_weights(z, carry_ref[rs, 0:1], upper, valid) for z, rs, valid in zip(zs, chunks, valids)]
        outs = [_dot(a.astype(BF16), v) for a, _ in weights]
        for rs, acc, o, (_, carry) in zip(chunks, accs, outs, weights):
            acc_ref[rs, :] = acc + o
            carry_ref[rs, :] = jnp.broadcast_to(carry, (ROW_CHUNK, HEAD_DIM))

    run(i, True)

    def body(jj, c):
        run(i - 1 - jj, False)
        return c

    lax.fori_loop(0, i, body, 0)
    o_ref[0] = acc_ref[...]


def _prompt_sb(q, kvb, b, t):
    q3 = q.reshape(b, t, q.shape[-1])
    kv3 = kvb.reshape(b, t, kvb.shape[-1])
    qbase = NSA_WIDTH // HEAD_DIM
    return pl.pallas_call(
        functools.partial(_prompt_sb_kernel, tq=TQ_SB, tk=TK),
        grid=(b, SB_HEADS, t // TQ_SB),
        in_specs=[pl.BlockSpec((1, TQ_SB, HEAD_DIM), lambda i, h, j: (i, j, qbase + h)),
                  pl.BlockSpec((1, t, HEAD_DIM), lambda i, h, j: (i, 0, 12 + h)),
                  pl.BlockSpec((1, t, HEAD_DIM), lambda i, h, j: (i, 0, 16 + h))],
        out_specs=pl.BlockSpec((1, TQ_SB, HEAD_DIM), lambda i, h, j: (i, j, h)),
        out_shape=jax.ShapeDtypeStruct((b, t, SB_WIDTH), F32),
        scratch_shapes=[pltpu.VMEM((TQ_SB, HEAD_DIM), F32), pltpu.VMEM((TQ_SB, HEAD_DIM), F32)],
        compiler_params=_cparams(("parallel", "parallel", "parallel")),
        name="prompt_sb",
    )(q3, kv3, kv3)


def _ssm_kernel(u_ref, h0_ref, wre_ref, wim_ref, cre_ref, cim_ref, d_ref, pre_ref, pim_ref,
                y_ref, hl_ref, hr_ref, hi_ref, *, t, tc):
    p8r = pre_ref[...]
    p8i = pim_ref[...]
    row = lax.broadcasted_iota(jnp.int32, (tc, SSM_TILE), 0) % SUBLANES

    def chunk(c, carry):
        cr, ci = carry
        start = pl.multiple_of(c * tc, tc)
        u = u_ref[0, pl.ds(start, tc), :]
        ub = u.astype(BF16)
        xr = _dot(ub, wre_ref[0])
        xi = _dot(ub, wim_ref[0])
        for sh in (1, 2, 4):
            ar = p8r[sh - 1:sh, :]
            ai = p8i[sh - 1:sh, :]
            sr = jnp.where(row >= sh, pltpu.roll(xr, sh, 0), 0.0)
            si = jnp.where(row >= sh, pltpu.roll(xi, sh, 0), 0.0)
            xr, xi = xr + ar * sr - ai * si, xi + ar * si + ai * sr
        for g in range(tc // SUBLANES):
            lo, hi = g * SUBLANES, (g + 1) * SUBLANES
            br = xr[lo:hi] + p8r * cr - p8i * ci
            bi = xi[lo:hi] + p8r * ci + p8i * cr
            hr_ref[lo:hi, :] = br
            hi_ref[lo:hi, :] = bi
            cr, ci = br[SUBLANES - 1:SUBLANES], bi[SUBLANES - 1:SUBLANES]
        y = (_dot(hr_ref[...].astype(BF16), cre_ref[0]) - _dot(hi_ref[...].astype(BF16), cim_ref[0])
             + d_ref[0] * u)
        y_ref[0, pl.ds(start, tc), :] = y
        return cr, ci

    cr, ci = lax.fori_loop(0, t // tc, chunk, (h0_ref[0, 0:1, :], h0_ref[0, 1:2, :]))
    hl_ref[0, 0:1, :] = cr
    hl_ref[0, 1:2, :] = ci


def _ssm(zc, h0, sw, b, t):
    wre, wim, cre, cim, dsk, pre, pim = sw
    tc = min(t, 256)
    u3 = zc.reshape(b, t, zc.shape[-1])
    ubase = (NSA_WIDTH + SB_WIDTH) // SSM_U_TILE
    wspec = lambda shp: pl.BlockSpec((1,) + shp, lambda i, k: (k, 0, 0))
    return pl.pallas_call(
        functools.partial(_ssm_kernel, t=t, tc=tc),
        grid=(b, SSM_TILES),
        in_specs=[pl.BlockSpec((1, t, SSM_U_TILE), lambda i, k: (i, 0, ubase + k)),
                  pl.BlockSpec((1, 2, SSM_TILE), lambda i, k: (i, 0, k)),
                  wspec((SSM_U_TILE, SSM_TILE)), wspec((SSM_U_TILE, SSM_TILE)),
                  wspec((SSM_TILE, SSM_U_TILE)), wspec((SSM_TILE, SSM_U_TILE)),
                  wspec((1, SSM_U_TILE)),
                  pl.BlockSpec((SUBLANES, SSM_TILE), lambda i, k: (0, k)),
                  pl.BlockSpec((SUBLANES, SSM_TILE), lambda i, k: (0, k))],
        out_specs=[pl.BlockSpec((1, t, SSM_U_TILE), lambda i, k: (i, 0, k)),
                   pl.BlockSpec((1, 2, SSM_TILE), lambda i, k: (i, 0, k))],
        out_shape=[jax.ShapeDtypeStruct((b, t, SSM_WIDTH), F32),
                   jax.ShapeDtypeStruct((b, 2, SSM_LANES), F32)],
        scratch_shapes=[pltpu.VMEM((tc, SSM_TILE), F32), pltpu.VMEM((tc, SSM_TILE), F32)],
        compiler_params=_cparams(("parallel", "parallel")),
        name="ssm_scan",
    )(u3, h0, wre, wim, cre, cim, dsk, pre, pim)


def _rms(h, gain):
    return h * lax.rsqrt(jnp.mean(h * h, axis=1, keepdims=True) + EPS) * gain


def _silu(z):
    return z * _sigmoid(z)


def _mix_kernel(ocmp_ref, osel_ref, owin_ref, gate_ref, ob_ref, yc_ref, za_ref, zb_ref, zc_ref, x_ref,
                wglu_ref, wout_ref, gain_ref, lng_ref, lnb_ref, y_ref, yb_ref, mixed_ref, *, alpha):
    gates = _sigmoid(gate_ref[...])
    for h in range(NSA_HEADS):
        sl = slice(h * HEAD_DIM, (h + 1) * HEAD_DIM)
        oa = (gates[:, 3 * h:3 * h + 1] * ocmp_ref[:, sl] + gates[:, 3 * h + 1:3 * h + 2] * osel_ref[:, sl]
              + gates[:, 3 * h + 2:3 * h + 3] * owin_ref[:, sl])
        mixed_ref[:, sl] = oa * _silu(za_ref[:, sl])
    mixed_ref[:, 0:NSA_WIDTH] = _rms(mixed_ref[:, 0:NSA_WIDTH], gain_ref[:, 0:NSA_WIDTH])
    b0, c0 = NSA_WIDTH, NSA_WIDTH + SB_WIDTH
    mixed_ref[:, b0:c0] = _rms(ob_ref[...] * _silu(zb_ref[...]), gain_ref[:, b0:c0])
    glu = _dot(yc_ref[...].astype(BF16), wglu_ref[...])
    yc = glu[:, 0:SSM_WIDTH] * _sigmoid(glu[:, SSM_WIDTH:2 * SSM_WIDTH])
    mixed_ref[:, c0:c0 + SSM_WIDTH] = _rms(yc * _silu(zc_ref[...]), gain_ref[:, c0:c0 + SSM_WIDTH])
    h = alpha * x_ref[...] + _dot(mixed_ref[...].astype(BF16), wout_ref[...])
    mu = jnp.mean(h, axis=1, keepdims=True)
    hc = h - mu
    var = jnp.mean(hc * hc, axis=1, keepdims=True)
    y = hc * lax.rsqrt(var + EPS) * lng_ref[...] + lnb_ref[...]
    y_ref[...] = y
    yb_ref[...] = y.astype(BF16)


def _mix_out(ocmp, osel, owin, gate, ob, yc, zc, x, wglu, wout, gain, lng, lnb, alpha):
    m, d = x.shape
    tm = min(m, 256)
    row = lambda w, c=0: pl.BlockSpec((tm, w), lambda i, c=c: (i, c))
    full = lambda a: pl.BlockSpec(a.shape, lambda i: (0,) * a.ndim)
    return pl.pallas_call(
        functools.partial(_mix_kernel, alpha=alpha),
        grid=(m // tm,),
        in_specs=[row(NSA_WIDTH), row(NSA_WIDTH), row(NSA_WIDTH), row(LANES), row(SB_WIDTH), row(SSM_WIDTH),
                  row(NSA_WIDTH, 0), row(SB_WIDTH, NSA_WIDTH // SB_WIDTH),
                  row(SSM_WIDTH, (NSA_WIDTH + SB_WIDTH + SSM_WIDTH) // SSM_WIDTH), row(d),
                  full(wglu), full(wout), full(gain), full(lng), full(lnb)],
        out_specs=[row(d), row(d)],
        out_shape=[jax.ShapeDtypeStruct((m, d), F32), jax.ShapeDtypeStruct((m, d), BF16)],
        scratch_shapes=[pltpu.VMEM((tm, d), F32)],
        compiler_params=_cparams(("parallel",)),
        name="mix_out",
    )(ocmp.reshape(m, -1), osel.reshape(m, -1), owin.reshape(m, -1), gate, ob.reshape(m, -1),
      yc.reshape(m, -1), zc, zc, zc, x, wglu, wout, gain, lng, lnb)


def _sample_compress_kernel(pt_ref, *refs):
    pages = refs[:PAGES_PER_STEP]
    w1_ref, w2_ref, first_ref, second_ref = refs[PAGES_PER_STEP:]
    page = pages[0].shape[2] // ROW_VECS
    per_page = page // CMP_STRIDE
    half = w1_ref.shape[1]
    pool = jnp.where(lax.broadcasted_iota(jnp.int32, (per_page, page), 1) // CMP_STRIDE
                     == lax.broadcasted_iota(jnp.int32, (per_page, page), 0), 1.0, 0.0).astype(BF16)
    for i, pg in enumerate(pages):
        rows = jnp.concatenate([_page_vec(pg, c, page) for c in range(2 * NSA_KV_HEADS)], axis=1)
        prod = jnp.concatenate([rows * w1_ref[...], rows * w2_ref[...]], axis=1)
        hi = prod.astype(BF16)
        lo = (prod - hi.astype(F32)).astype(BF16)
        sums = _dot(pool, hi) + _dot(pool, lo)
        first_ref[0, i * per_page:(i + 1) * per_page, :] = sums[:, 0:half]
        second_ref[0, i * per_page:(i + 1) * per_page, :] = sums[:, half:2 * half]


def _sample_compress(cache, layer, page_table, w_cmp):
    db, n_pages = page_table.shape
    page = cache.shape[2] // ROW_VECS
    per_page = page // CMP_STRIDE
    steps = n_pages // PAGES_PER_STEP
    nc = n_pages * per_page
    half = 4 * HEAD_DIM

    def pspec(i):
        return pl.BlockSpec((1, 1, page * ROW_VECS, HEAD_DIM),
                            lambda b, s, pt, i=i: (layer, pt[b, s * PAGES_PER_STEP + i], 0, 0))

    rows = PAGES_PER_STEP * per_page
    tiled = lambda w: jnp.concatenate([jnp.tile(w[kv], (per_page, 1)) for kv in (0, 0, 1, 1)], axis=1)
    w1, w2 = tiled(w_cmp[:, :CMP_STRIDE]), tiled(w_cmp[:, CMP_STRIDE:])
    return pl.pallas_call(
        _sample_compress_kernel,
        grid_spec=pltpu.PrefetchScalarGridSpec(
            num_scalar_prefetch=1,
            grid=(db, steps),
            in_specs=[pspec(i) for i in range(PAGES_PER_STEP)]
                     + [pl.BlockSpec((page, half), lambda b, s, pt: (0, 0)),
                        pl.BlockSpec((page, half), lambda b, s, pt: (0, 0))],
            out_specs=[pl.BlockSpec((1, rows, half), lambda b, s, pt: (b, s, 0)),
                       pl.BlockSpec((1, rows, half), lambda b, s, pt: (b, s, 0))]),
        out_shape=[jax.ShapeDtypeStruct((db, nc, half), F32), jax.ShapeDtypeStruct((db, nc, half), F32)],
        compiler_params=_cparams(("parallel", "parallel")),
        name="sample_compress",
    )(page_table, *([cache] * PAGES_PER_STEP), w1, w2)


def _sample_cmpwin_kernel(q_ref, first_ref, second_ref, bias_ref, win_ref, wbias_ref,
                          ocmp_ref, sel_ref, owin_ref, *, ts, nc, pos0):
    comp = first_ref[0] + pltpu.roll(second_ref[0], nc - 1, 0)
    nb = nc // CMP_PER_SEL
    pos = pos0 + lax.broadcasted_iota(jnp.int32, (ts, nb), 0)
    for g in range(NSA_KV_HEADS):
        qg = _stack_heads(q_ref[0, :, pl.ds(g * NSA_GROUP * HEAD_DIM, NSA_GROUP * HEAD_DIM)], HEAD_DIM)
        ck = comp[:, g * HEAD_DIM:(g + 1) * HEAD_DIM].astype(BF16)
        cv = comp[:, (2 + g) * HEAD_DIM:(3 + g) * HEAD_DIM].astype(BF16)
        bias = bias_ref[NSA_GROUP * g:NSA_GROUP * (g + 1)].reshape(NSA_GROUP * ts, nc)
        o, imp = _cmp_attend(qg, ck, cv, bias, ts)
        sel_ref[0, g] = _select_blocks(imp, pos, N_SEL - 1)
        kw = win_ref[0, :, g * HEAD_DIM:(g + 1) * HEAD_DIM].astype(BF16)
        vw = win_ref[0, :, (2 + g) * HEAD_DIM:(3 + g) * HEAD_DIM].astype(BF16)
        wb = wbias_ref[NSA_GROUP * g:NSA_GROUP * (g + 1)].reshape(NSA_GROUP * ts, kw.shape[0])
        sw = _dot_nt(qg, kw) * SCALE + wb
        ow = _dot(_masked_softmax(sw, wb > 0.5 * NEG).astype(BF16), vw)
        for r in range(NSA_GROUP):
            h = NSA_GROUP * g + r
            ocmp_ref[0, :, h * HEAD_DIM:(h + 1) * HEAD_DIM] = o[r * ts:(r + 1) * ts]
            owin_ref[0, :, h * HEAD_DIM:(h + 1) * HEAD_DIM] = ow[r * ts:(r + 1) * ts]


def _sample_cmpwin(q, first, second, bias_cmp, win_all, bias_win, db, ts, past_len):
    nc = first.shape[1]
    nb = nc // CMP_PER_SEL
    nw = win_all.shape[1]
    q3 = q.reshape(db, ts, q.shape[-1])
    return pl.pallas_call(
        functools.partial(_sample_cmpwin_kernel, ts=ts, nc=nc, pos0=past_len),
        grid=(db,),
        in_specs=[pl.BlockSpec((1, ts, NSA_WIDTH), lambda b: (b, 0, 0)),
                  pl.BlockSpec((1, nc, 4 * HEAD_DIM), lambda b: (b, 0, 0)),
                  pl.BlockSpec((1, nc, 4 * HEAD_DIM), lambda b: (b, 0, 0)),
                  pl.BlockSpec((NSA_HEADS, ts, nc), lambda b: (0, 0, 0)),
                  pl.BlockSpec((1, nw, 4 * HEAD_DIM), lambda b: (b, 0, 0)),
                  pl.BlockSpec((NSA_HEADS, ts, nw), lambda b: (0, 0, 0))],
        out_specs=[pl.BlockSpec((1, ts, NSA_WIDTH), lambda b: (b, 0, 0)),
                   pl.BlockSpec((1, NSA_KV_HEADS, ts, nb), lambda b: (b, 0, 0, 0)),
                   pl.BlockSpec((1, ts, NSA_WIDTH), lambda b: (b, 0, 0))],
        out_shape=[jax.ShapeDtypeStruct((db, ts, NSA_WIDTH), F32),
                   jax.ShapeDtypeStruct((db, NSA_KV_HEADS, ts, nb), F32),
                   jax.ShapeDtypeStruct((db, ts, NSA_WIDTH), F32)],
        compiler_params=_cparams(("parallel",)),
        name="sample_cmpwin",
    )(q3, first, second, bias_cmp, win_all, bias_win)


def _sample_sel_kernel(pt_ref, *refs, ts, page):
    pages = refs[:PAGES_PER_STEP]
    q_ref, sel_ref, new_ref, bias_ref, nbias_ref, o_ref, m_ref, l_ref, acc_ref = refs[PAGES_PER_STEP:]
    s_idx = pl.program_id(1)
    last = pl.num_programs(1) - 1
    rows = NSA_GROUP * ts
    nkeys = PAGES_PER_STEP * page
    blocks = nkeys // SEL_BLOCK

    @pl.when(s_idx == 0)
    def _():
        _flash_init(m_ref, l_ref, acc_ref)

    expand = jnp.where(lax.broadcasted_iota(jnp.int32, (blocks, nkeys), 0)
                       == lax.broadcasted_iota(jnp.int32, (blocks, nkeys), 1) // SEL_BLOCK, 1.0, 0.0).astype(BF16)
    for g in range(NSA_KV_HEADS):
        qg = _stack_heads(q_ref[0, :, pl.ds(g * NSA_GROUP * HEAD_DIM, NSA_GROUP * HEAD_DIM)], HEAD_DIM)
        k = jnp.concatenate([_page_vec(pg, 4 + g, page) for pg in pages], axis=0).astype(BF16)
        v = jnp.concatenate([_page_vec(pg, 6 + g, page) for pg in pages], axis=0).astype(BF16)
        bias = bias_ref[jnp.where(s_idx == last, 1, 0), NSA_GROUP * g:NSA_GROUP * (g + 1)].reshape(rows, nkeys)
        s = _dot_nt(qg, k) * SCALE2 + bias
        chosen = _dot(sel_ref[0, g, 0].astype(BF16), expand) > 0.5
        s = jnp.where(chosen[None], s.reshape(NSA_GROUP, ts, nkeys), NEG).reshape(rows, nkeys)
        sl = slice(g * rows, (g + 1) * rows)
        _flash_tile(s, v, m_ref.at[sl], l_ref.at[sl], acc_ref.at[sl])

        @pl.when(s_idx == last)
        def _():
            kn = _pad_keys(new_ref[0, :, g * HEAD_DIM:(g + 1) * HEAD_DIM])
            vn = _pad_keys(new_ref[0, :, (2 + g) * HEAD_DIM:(3 + g) * HEAD_DIM])
            nbias = nbias_ref[NSA_GROUP * g:NSA_GROUP * (g + 1)].reshape(rows, LANES)
            _flash_tile(_dot_nt(qg, kn) * SCALE2 + nbias, vn, m_ref.at[sl], l_ref.at[sl], acc_ref.at[sl])
            o = acc_ref[sl] / l_ref[sl]
            for r in range(NSA_GROUP):
                h = NSA_GROUP * g + r
                o_ref[0, :, h * HEAD_DIM:(h + 1) * HEAD_DIM] = o[r * ts:(r + 1) * ts]


def _sample_sel(cache, layer, page_table, q, sel_steps, kv_new, bias_steps, bias_new, db, ts):
    n_pages = page_table.shape[1]
    page = cache.shape[2] // ROW_VECS
    steps = n_pages // PAGES_PER_STEP
    half = 4 * HEAD_DIM
    nkeys = PAGES_PER_STEP * page
    blocks = nkeys // SEL_BLOCK
    q3 = q.reshape(db, ts, q.shape[-1])
    new3 = kv_new.reshape(db, ts, kv_new.shape[-1])

    def pspec(i):
        return pl.BlockSpec((1, 1, page * ROW_VECS, HEAD_DIM),
                            lambda b, s, pt, i=i: (layer, pt[b, s * PAGES_PER_STEP + i], 0, 0))

    return pl.pallas_call(
        functools.partial(_sample_sel_kernel, ts=ts, page=page),
        grid_spec=pltpu.PrefetchScalarGridSpec(
            num_scalar_prefetch=1,
            grid=(db, steps),
            in_specs=[pspec(i) for i in range(PAGES_PER_STEP)]
                     + [pl.BlockSpec((1, ts, NSA_WIDTH), lambda b, s, pt: (b, 0, 0)),
                        pl.BlockSpec((1, NSA_KV_HEADS, 1, ts, blocks), lambda b, s, pt: (b, 0, s, 0, 0)),
                        pl.BlockSpec((1, ts, half), lambda b, s, pt: (b, 0, 1)),
                        pl.BlockSpec((2, NSA_HEADS, ts, nkeys), lambda b, s, pt: (0, 0, 0, 0)),
                        pl.BlockSpec((NSA_HEADS, ts, LANES), lambda b, s, pt: (0, 0, 0))],
            out_specs=pl.BlockSpec((1, ts, NSA_WIDTH), lambda b, s, pt: (b, 0, 0)),
            scratch_shapes=[pltpu.VMEM((NSA_HEADS * ts, HEAD_DIM), F32),
                            pltpu.VMEM((NSA_HEADS * ts, HEAD_DIM), F32),
                            pltpu.VMEM((NSA_HEADS * ts, HEAD_DIM), F32)]),
        out_shape=jax.ShapeDtypeStruct((db, ts, NSA_WIDTH), F32),
        compiler_params=_cparams(("parallel", "arbitrary")),
        name="sample_sel",
    )(page_table, *([cache] * PAGES_PER_STEP), q3, sel_steps, new3, bias_steps, bias_new)


def _sample_sb_kernel(pt_ref, *refs, ts, page, seg):
    pages = refs[:PAGES_PER_STEP]
    q_ref, knew_ref, vnew_ref, o_ref, carry_ref, acc_ref = refs[PAGES_PER_STEP:]
    s_idx = pl.program_id(1)
    last = pl.num_programs(1) - 1
    upper = _strict_upper(seg)

    rows = SB_HEADS * ts
    heads = range(SB_HEADS)
    hs = lambda h: slice(h * HEAD_DIM, (h + 1) * HEAD_DIM)

    def attend(ks, vs, carry, up, valid):
        z = jnp.concatenate([_dot_nt(q_ref[0, :, hs(h)], ks[h]) for h in heads], axis=0) * SCALE2
        a, carry = _sb_weights(z, carry, up, valid)
        return jnp.concatenate([_dot(a[h * ts:(h + 1) * ts].astype(BF16), vs[h]) for h in heads], axis=0), carry

    @pl.when(s_idx == 0)
    def _():
        kn = [_pad_keys(knew_ref[0, :, hs(h)]) for h in heads]
        vn = [_pad_keys(vnew_ref[0, :, hs(h)]) for h in heads]
        valid = (lax.broadcasted_iota(jnp.int32, (rows, LANES), 1)
                 < lax.broadcasted_iota(jnp.int32, (rows, LANES), 0) % ts)
        o, carry = attend(kn, vn, jnp.zeros((rows, 1), F32), _strict_upper(LANES), valid)
        acc_ref[...] = o
        carry_ref[...] = jnp.broadcast_to(carry, (rows, HEAD_DIM))

    ks = [jnp.concatenate([_page_vec(pg, h, page) for pg in pages], axis=0).astype(BF16) for h in heads]
    vs = [jnp.concatenate([_page_vec(pg, SB_HEADS + h, page) for pg in pages], axis=0).astype(BF16) for h in heads]
    o, carry = attend(ks, vs, carry_ref[:, 0:1], upper, None)
    acc_ref[...] = acc_ref[...] + o
    carry_ref[...] = jnp.broadcast_to(carry, (rows, HEAD_DIM))

    @pl.when(s_idx == last)
    def _():
        for h in heads:
            o_ref[0, :, hs(h)] = acc_ref[h * ts:(h + 1) * ts, :]


def _sample_sb(cache, layer, page_table, q, kv_new, db, ts):
    n_pages = page_table.shape[1]
    page = cache.shape[2] // ROW_VECS
    steps = n_pages // PAGES_PER_STEP
    q3 = q.reshape(db, ts, q.shape[-1])
    new3 = kv_new.reshape(db, ts, kv_new.shape[-1])

    def pspec(i):
        return pl.BlockSpec((1, 1, page * ROW_VECS, HEAD_DIM),
                            lambda b, s, pt, i=i: (layer, pt[b, (steps - 1 - s) * PAGES_PER_STEP + i], 0, 0))

    return pl.pallas_call(
        functools.partial(_sample_sb_kernel, ts=ts, page=page, seg=TK),
        grid_spec=pltpu.PrefetchScalarGridSpec(
            num_scalar_prefetch=1,
            grid=(db, steps),
            in_specs=[pspec(i) for i in range(PAGES_PER_STEP)]
                     + [pl.BlockSpec((1, ts, SB_WIDTH), lambda b, s, pt: (b, 0, NSA_WIDTH // SB_WIDTH)),
                        pl.BlockSpec((1, ts, SB_WIDTH), lambda b, s, pt: (b, 0, 3)),
                        pl.BlockSpec((1, ts, SB_WIDTH), lambda b, s, pt: (b, 0, 4))],
            out_specs=pl.BlockSpec((1, ts, SB_WIDTH), lambda b, s, pt: (b, 0, 0)),
            scratch_shapes=[pltpu.VMEM((SB_HEADS * ts, HEAD_DIM), F32),
                            pltpu.VMEM((SB_HEADS * ts, HEAD_DIM), F32)]),
        out_shape=jax.ShapeDtypeStruct((db, ts, SB_WIDTH), F32),
        compiler_params=_cparams(("parallel", "arbitrary")),
        name="sample_sb",
    )(page_table, *([cache] * PAGES_PER_STEP), q3, new3, new3)


def _t5_bucket(dist):
    n = jnp.maximum(dist, 0)
    exact = NUM_BUCKETS // 2
    nf = jnp.maximum(n, 1).astype(F32)
    large = exact + (jnp.log(nf / exact) / math.log(MAX_DISTANCE / exact) * (NUM_BUCKETS - exact)).astype(jnp.int32)
    return jnp.where(n < exact, n, jnp.minimum(large, NUM_BUCKETS - 1))


def _bias(rel_bias, delta, valid):
    bucket = _t5_bucket(delta)[None]
    tab = rel_bias.astype(F32).T.reshape((rel_bias.shape[1], NUM_BUCKETS) + (1,) * delta.ndim)
    b = jnp.zeros((rel_bias.shape[1],) + delta.shape, F32)
    for k in range(NUM_BUCKETS):
        b = jnp.where(bucket == k, tab[:, k], b)
    return jnp.where(valid[None], b, NEG)


def _prompt_tables(rel_bias, t):
    tt = jnp.arange(TQ)[:, None]
    ss = jnp.arange(TK)[None, :]
    d = tt - ss
    assert 2 * TK - TQ >= MAX_DISTANCE
    far = _bias(rel_bias, d + 2 * TK, d > -2 * TK)
    tables = jnp.stack([
        _bias(rel_bias, d, d >= 0) - far,
        _bias(rel_bias, d + TK, d > -TK) - far,
        jnp.where(d + 2 * TK < WINDOW, 0.0, NEG) + 0.0 * far,
    ])
    nc = t // CMP_STRIDE
    dc = jnp.arange(t)[:, None] - (jnp.arange(nc)[None, :] * CMP_STRIDE + CMP_BLOCK - 1)
    return tables * LOG2E, _bias(rel_bias, dc, dc >= 0) * LOG2E


def _sample_tables(rel_bias, ts, past_len, page, nw, nw_pad):
    qpos = past_len + jnp.arange(ts)[:, None]
    nc = past_len // CMP_STRIDE
    dc = qpos - (jnp.arange(nc)[None, :] * CMP_STRIDE + CMP_BLOCK - 1)
    bias_cmp = _bias(rel_bias, dc, dc >= 0)
    widx = jnp.arange(nw_pad)[None, :]
    dw = qpos - (past_len - nw + widx)
    bias_win = _bias(rel_bias, dw, (dw >= 0) & (dw < WINDOW) & (widx < nw + ts))
    nkeys = PAGES_PER_STEP * page
    d_last = qpos - (past_len - nkeys + jnp.arange(nkeys)[None, :])
    d_far = d_last + nkeys
    bias_steps = jnp.stack([_bias(rel_bias, d_far, d_far >= 0), _bias(rel_bias, d_last, d_last >= 0)])
    nidx = jnp.arange(LANES)[None, :]
    dn = jnp.arange(ts)[:, None] - nidx
    return bias_cmp, bias_win, bias_steps * LOG2E, _bias(rel_bias, dn, (dn >= 0) & (nidx < ts)) * LOG2E


def _ssm_weights(lam_re, lam_im, log_dt, b_re, b_im, c_re, c_im, d_skip):
    lam_re, lam_im = lam_re.astype(F32), lam_im.astype(F32)
    dt = jnp.exp(log_dt.astype(F32))[:, None]
    mag = jnp.exp(lam_re * dt)
    a_re, a_im = mag * jnp.cos(lam_im * dt), mag * jnp.sin(lam_im * dt)
    den = lam_re * lam_re + lam_im * lam_im
    f_re = ((a_re - 1.0) * lam_re + a_im * lam_im) / den
    f_im = (a_im * lam_re - (a_re - 1.0) * lam_im) / den
    b_re, b_im = b_re.astype(F32), b_im.astype(F32)
    bb_re = f_re[..., None] * b_re - f_im[..., None] * b_im
    bb_im = f_re[..., None] * b_im + f_im[..., None] * b_re
    gpt = SSM_GROUPS // SSM_TILES
    eye = jnp.eye(gpt, dtype=F32)

    def in_mat(bb):
        return jnp.einsum('kgpn,gh->kgnhp', bb.reshape(SSM_TILES, gpt, SSM_STATE, SSM_CH), eye).reshape(
            SSM_TILES, SSM_U_TILE, SSM_TILE).astype(BF16)

    def out_mat(c):
        return jnp.einsum('kgnp,gh->khpgn', c.astype(F32).reshape(SSM_TILES, gpt, SSM_CH, SSM_STATE), eye).reshape(
            SSM_TILES, SSM_TILE, SSM_U_TILE).astype(BF16)

    pr, pi = a_re.reshape(1, SSM_LANES), a_im.reshape(1, SSM_LANES)
    while pr.shape[0] < SUBLANES:
        tr, ti = pr[-1:], pi[-1:]
        pr, pi = (jnp.concatenate([pr, pr * tr - pi * ti]), jnp.concatenate([pi, pr * ti + pi * tr]))
    return (in_mat(bb_re), in_mat(bb_im), out_mat(c_re), out_mat(c_im),
            d_skip.astype(F32).reshape(SSM_TILES, 1, SSM_U_TILE), pr, pi)


def _split_w_in(w_in):
    sizes = (NSA_WIDTH, 6 * NSA_KV_HEADS * HEAD_DIM, 3 * NSA_HEADS, NSA_WIDTH,
             SB_WIDTH, SB_WIDTH, SB_WIDTH, SB_WIDTH, SSM_WIDTH, SSM_WIDTH)
    offs = [0]
    for s in sizes:
        offs.append(offs[-1] + s)
    q_a, kv_a, g_a, z_a, q_b, k_b, v_b, z_b, u_c, z_c = (w_in[..., offs[i]:offs[i + 1]] for i in range(10))
    cat = lambda xs: jnp.concatenate(xs, axis=-1).astype(BF16)
    pad = jnp.zeros(g_a.shape[:-1] + (LANES - g_a.shape[-1],), g_a.dtype)
    return cat([q_a, q_b]), cat([kv_a, k_b, v_b]), cat([z_a, z_b, u_c, z_c]), cat([g_a, pad])


KV_TILE = 4 * HEAD_DIM


def _kv_proj_kernel(*refs, tm):
    x_ref, w_ref = refs[:2]
    kvf_ref, kvb_ref, nsa_ref, sb_ref = refs[-4:]
    j = pl.program_id(1)
    acc = _dot(x_ref[...], w_ref[...])
    kvf_ref[...] = acc
    kvb_ref[...] = acc.astype(BF16)

    def scatter(dst, base):
        for c in range(KV_TILE // HEAD_DIM):
            dst[0, pl.ds(base + c, tm, stride=ROW_VECS), :] = acc[:, c * HEAD_DIM:(c + 1) * HEAD_DIM]

    for step, dst, base in ((0, nsa_ref, 0), (1, nsa_ref, 4), (3, sb_ref, 0), (4, sb_ref, 4)):
        pl.when(j == step)(functools.partial(scatter, dst, base))


def _kv_proj(x, w, layer, depth, stacked):
    m, k = x.shape
    n = w.shape[1]
    tm = min(m, 1024)
    assert n == 5 * KV_TILE
    rows = pl.BlockSpec((1, tm * ROW_VECS, HEAD_DIM), lambda i, j: (layer, i, 0))
    stack_shape = jax.ShapeDtypeStruct((depth, m * ROW_VECS, HEAD_DIM), F32)
    return pl.pallas_call(
        functools.partial(_kv_proj_kernel, tm=tm),
        grid=(m // tm, n // KV_TILE),
        in_specs=[pl.BlockSpec((tm, k), lambda i, j: (i, 0)), pl.BlockSpec((k, KV_TILE), lambda i, j: (0, j)),
                  pl.BlockSpec(memory_space=pl.ANY), pl.BlockSpec(memory_space=pl.ANY)],
        out_specs=[pl.BlockSpec((tm, KV_TILE), lambda i, j: (i, j)), pl.BlockSpec((tm, KV_TILE), lambda i, j: (i, j)),
                   rows, rows],
        out_shape=[jax.ShapeDtypeStruct((m, n), F32), jax.ShapeDtypeStruct((m, n), BF16), stack_shape, stack_shape],
        input_output_aliases={2: 2, 3: 3},
        compiler_params=_cparams(("parallel", "arbitrary")),
        name="in_proj_kv",
    )(x, w, *stacked)


def _in_proj(xb, wl, layer, depth, stacked):
    wa, wb, wc, wd = wl
    (q,) = _matmul(xb, wa, (BF16,), "in_proj_q")
    kvf, kvb, nsa, sb = _kv_proj(xb, wb, layer, depth, stacked)
    (zc,) = _matmul(xb, wc, (F32,), "in_proj_z")
    (gate,) = _matmul(xb, wd, (F32,), "in_proj_gate")
    return q, kvf, kvb, zc, gate, (nsa, sb)


def kernel(x_prompt, x_sample, cache_nsa, cache_sb, state_win, state_ssm, page_table, rel_bias, w_in, w_cmp,
           ssm_lam_re, ssm_lam_im, ssm_log_dt, ssm_b_re, ssm_b_im, ssm_c_re, ssm_c_im, ssm_d, w_glu, mix_gain,
           w_out, ln_g, ln_b):
    depth = w_in.shape[0]
    b, t, d = x_prompt.shape
    db, ts, _ = x_sample.shape
    n_phys, page = cache_nsa.shape[1], cache_nsa.shape[2]
    n_pages = page_table.shape[1]
    past_len = n_pages * page
    nw = state_win.shape[2]
    alpha = (2 * depth) ** 0.25
    assert t % TQ == 0 and t % TQ_SB == 0 and TQ_SB % TK == 0 and WINDOW == 2 * TK and TQ == TK and n_pages % PAGES_PER_STEP == 0
    assert past_len % SEL_BLOCK == 0 and ts % SUBLANES == 0 and ts <= SEL_BLOCK and nw == WINDOW

    w_groups = _split_w_in(w_in)
    w_glu_b, w_out_b = w_glu.astype(BF16), w_out.astype(BF16)
    ssm_w = jax.vmap(_ssm_weights)(ssm_lam_re, ssm_lam_im, ssm_log_dt, ssm_b_re, ssm_b_im, ssm_c_re, ssm_c_im, ssm_d)
    tables_p, bias_cmp_p = _prompt_tables(rel_bias, t)
    nw_pad = -(-(nw + ts) // LANES) * LANES
    bias_cmp_s, bias_win_s, bias_steps_s, bias_new_s = _sample_tables(rel_bias, ts, past_len, page, nw, nw_pad)
    cache_nsa_r = cache_nsa.reshape(depth, n_phys, page * ROW_VECS, HEAD_DIM)
    cache_sb_r = cache_sb.reshape(depth, n_phys, page * ROW_VECS, HEAD_DIM)
    w_cmp = w_cmp.astype(F32)
    page_table = page_table.astype(jnp.int32)

    xp, xs = x_prompt.reshape(b * t, d), x_sample.reshape(db * ts, d)
    xpb, xsb = xp.astype(BF16), xs.astype(BF16)
    outs = [[] for _ in range(8)]
    rows_p = (jnp.zeros((depth, b * t * ROW_VECS, HEAD_DIM), F32),) * 2
    rows_s = (jnp.zeros((depth, db * ts * ROW_VECS, HEAD_DIM), F32),) * 2
    for l in range(depth):
        wl = tuple(w[l] for w in w_groups)
        sw = tuple(w[l] for w in ssm_w)
        gain, lng, lnb = mix_gain[l].reshape(1, d), ln_g[l].reshape(1, d), ln_b[l].reshape(1, d)

        q, kvf, kvb, zc, gate, rows_p = _in_proj(xpb, wl, l, depth, rows_p)
        o_cmp, sel = _prompt_cmp(q, kvf, w_cmp[l], bias_cmp_p, b, t)
        o_sel, o_win = _prompt_selwin(q, kvb, sel, tables_p, b, t)
        o_b = _prompt_sb(q, kvb, b, t)
        y_c, h_p = _ssm(zc, jnp.zeros((b, 2, SSM_LANES), F32), sw, b, t)
        xp, xpb = _mix_out(o_cmp, o_sel, o_win, gate, o_b, y_c, zc, xp, w_glu_b[l], w_out_b[l], gain, lng, lnb,
                           alpha)
        kv3 = kvf.reshape(b, t, -1)
        outs[4].append(kv3[:, max(t - WINDOW, 0):, 1024:1536].reshape(b, -1, 2, NSA_KV_HEADS, HEAD_DIM))
        outs[6].append(h_p.reshape(b, 2, SSM_GROUPS, SSM_STATE))

        q, kvf, kvb, zc, gate, rows_s = _in_proj(xsb, wl, l, depth, rows_s)
        kv3 = kvf.reshape(db, ts, -1)
        win_all = jnp.concatenate([state_win[l].reshape(db, nw, -1), kv3[:, :, 1024:1536],
                                   jnp.zeros((db, nw_pad - nw - ts, 4 * HEAD_DIM), F32)], axis=1)
        first, second = _sample_compress(cache_nsa_r, l, page_table, w_cmp[l])
        o_cmp, sel, o_win = _sample_cmpwin(q, first, second, bias_cmp_s, win_all, bias_win_s, db, ts, past_len)
        steps = n_pages // PAGES_PER_STEP
        sel_steps = jnp.moveaxis(sel.reshape(db, NSA_KV_HEADS, ts, steps, -1), 3, 2)
        o_sel = _sample_sel(cache_nsa_r, l, page_table, q, sel_steps, kvf, bias_steps_s, bias_new_s, db, ts)
        o_b = _sample_sb(cache_sb_r, l, page_table, q, kvf, db, ts)
        y_c, h_s = _ssm(zc, state_ssm[l].reshape(db, 2, SSM_LANES).astype(F32), sw, db, ts)
        xs, xsb = _mix_out(o_cmp, o_sel, o_win, gate, o_b, y_c, zc, xs, w_glu_b[l], w_out_b[l], gain, lng, lnb,
                           alpha)
        outs[5].append(win_all[:, ts:ts + nw].reshape(db, nw, 2, NSA_KV_HEADS, HEAD_DIM))
        outs[7].append(h_s.reshape(db, 2, SSM_GROUPS, SSM_STATE))

    win_p, win_s, ssm_p, ssm_s = (jnp.stack(outs[i]) for i in (4, 5, 6, 7))
    return (xp.reshape(b, t, d), xs.reshape(db, ts, d),
            rows_p[0].reshape(depth, b, t, 4, NSA_KV_HEADS, HEAD_DIM),
            rows_s[0].reshape(depth, db, ts, 4, NSA_KV_HEADS, HEAD_DIM),
            rows_p[1].reshape(depth, b, t, 2, SB_HEADS, HEAD_DIM),
            rows_s[1].reshape(depth, db, ts, 2, SB_HEADS, HEAD_DIM),
            win_p, win_s, ssm_p, ssm_s)
```

```python
import functools
import math

import jax
import jax.numpy as jnp
from jax import lax
from jax.experimental import pallas as pl
from jax.experimental.pallas import tpu as pltpu

F32 = jnp.float32
BF16 = jnp.bfloat16

HEAD_DIM = 128
NSA_KV_HEADS = 2
NSA_GROUP = 4
NSA_HEADS = NSA_KV_HEADS * NSA_GROUP
NSA_WIDTH = NSA_HEADS * HEAD_DIM
SB_HEADS = 4
SB_WIDTH = SB_HEADS * HEAD_DIM
SSM_CH = 16
SSM_GROUPS = 32
SSM_STATE = 64
SSM_WIDTH = SSM_CH * SSM_GROUPS
SSM_LANES = SSM_GROUPS * SSM_STATE
CMP_BLOCK = 32
CMP_STRIDE = 16
SEL_BLOCK = 64
CMP_PER_SEL = SEL_BLOCK // CMP_STRIDE
N_SEL = 16
WINDOW = 512
NUM_BUCKETS = 32
MAX_DISTANCE = 128
FORCE_SCORE = 1e4
EPS = 1e-5
SCALE = HEAD_DIM ** -0.5
LOG2E = math.log2(math.e)
SCALE2 = SCALE * LOG2E
NEG = -1e30

LANES = 128
SUBLANES = 8
VMEM_LIMIT = 48 * 1024 * 1024

TQ = 256
TK = 256
ROW_CHUNK = 256
SB_ROW_CHUNK = 128
TQ_SB = 512
PAGES_PER_STEP = 8
ROW_VECS = 8
SSM_TILE = 512
SSM_TILES = SSM_LANES // SSM_TILE
SSM_U_TILE = SSM_WIDTH // SSM_TILES


def _cparams(sem):
    return pltpu.CompilerParams(dimension_semantics=sem, vmem_limit_bytes=VMEM_LIMIT)


def _dot(a, b):
    return jnp.dot(a, b, preferred_element_type=F32)


def _dot_nt(a, b):
    return lax.dot_general(a, b, (((1,), (1,)), ((), ())), preferred_element_type=F32)


def _split_dot(x, w):
    hi = x.astype(BF16)
    lo = (x - hi.astype(F32)).astype(BF16)
    return _dot(hi, w) + _dot(lo, w)


def _sigmoid(x):
    return 1.0 / (1.0 + jnp.exp(-x))


def _lane_tile(x, n):
    return x if n == 1 else jnp.concatenate([x] * n, axis=1)


def _mm_kernel(x_ref, w_ref, *o_refs):
    acc = _dot(x_ref[...], w_ref[...])
    for o in o_refs:
        o[...] = acc.astype(o.dtype)


def _matmul(x, w, out_dtypes, name):
    m, k = x.shape
    n = w.shape[1]
    tm = min(m, 1024)
    tn = min(n, 512)
    outs = pl.pallas_call(
        _mm_kernel,
        grid=(m // tm, n // tn),
        in_specs=[pl.BlockSpec((tm, k), lambda i, j: (i, 0)),
                  pl.BlockSpec((k, tn), lambda i, j: (0, j))],
        out_specs=[pl.BlockSpec((tm, tn), lambda i, j: (i, j)) for _ in out_dtypes],
        out_shape=[jax.ShapeDtypeStruct((m, n), d) for d in out_dtypes],
        compiler_params=_cparams(("parallel", "parallel")),
        name=name,
    )(x, w)
    return outs


def _masked_softmax(s, valid):
    s = jnp.where(valid, s, NEG)
    m = jnp.max(s, axis=1, keepdims=True)
    e = jnp.where(valid, jnp.exp(s - m), 0.0)
    return e / jnp.maximum(jnp.sum(e, axis=1, keepdims=True), 1e-30)


def _masked_softmax2(s, valid):
    e = jnp.where(valid, jnp.exp2(s - jnp.max(s, axis=1, keepdims=True)), 0.0)
    return e * (1.0 / jnp.maximum(jnp.sum(e, axis=1, keepdims=True), 1e-30))


def _cmp_attend(qg, ck, cv, bias, rows):
    s = _dot_nt(qg, ck) * SCALE + bias
    p = _masked_softmax(s, bias > 0.5 * NEG)
    o = _dot(p.astype(BF16), cv)
    imp = p[0:rows] + p[rows:2 * rows] + p[2 * rows:3 * rows] + p[3 * rows:4 * rows]
    nc = ck.shape[0]
    nb = nc // CMP_PER_SEL
    grp = (lax.broadcasted_iota(jnp.int32, (nc, nb), 0) // CMP_PER_SEL
           == lax.broadcasted_iota(jnp.int32, (nc, nb), 1))
    return o, _split_dot(imp, jnp.where(grp, 1.0, 0.0).astype(BF16))


def _select_blocks(imp, pos, k_top, groups=1):
    rows, width = imp.shape
    nb = width // groups
    lane = lax.broadcasted_iota(jnp.int32, (rows, width), 1)
    blk = lane % nb
    cur = pos // SEL_BLOCK
    score = jnp.where(blk * SEL_BLOCK <= pos, imp, -1.0)
    score = jnp.where(blk == 0, FORCE_SCORE, score)
    score = jnp.where(blk == cur, FORCE_SCORE, score)
    score = jnp.where(blk == cur - 1, FORCE_SCORE, score)
    cnt = jnp.zeros((rows, width), F32)
    for i in range(nb):
        col = score[:, i:i + 1]
        for g in range(1, groups):
            col = jnp.where(lane >= g * nb, score[:, g * nb + i:g * nb + i + 1], col)
        ge = jnp.where(col >= score, 1.0, 0.0)
        gt = jnp.where(col > score, 1.0, 0.0)
        cnt = cnt + jnp.where(blk > i, ge, gt)
    return jnp.where(cnt < k_top, 1.0, 0.0)


def _flash_init(m_ref, l_ref, acc_ref):
    m_ref[...] = jnp.full(m_ref.shape, NEG, F32)
    l_ref[...] = jnp.zeros(l_ref.shape, F32)
    acc_ref[...] = jnp.zeros(acc_ref.shape, F32)


def _flash_tile(s, v, m_ref, l_ref, acc_ref):
    m, l, acc = _flash_update(s, v, m_ref[...], l_ref[...], acc_ref[...])
    m_ref[...] = m
    l_ref[...] = l
    acc_ref[...] = acc


def _flash_update(s, v, m_prev, l_prev, acc_prev):
    m_new = jnp.maximum(m_prev, jnp.max(s, axis=1, keepdims=True))
    alpha = jnp.exp2(m_prev - m_new)
    p = jnp.exp2(s - _lane_tile(m_new, s.shape[1] // LANES))
    return (m_new, alpha * l_prev + jnp.sum(p, axis=1, keepdims=True),
            alpha * acc_prev + _dot(p.astype(BF16), v))


def _page_vec(pg, c, page):
    return pg[0, 0, pl.ds(c, page, stride=ROW_VECS), :]


def _pad_keys(x):
    pad = jnp.zeros((LANES - x.shape[0], x.shape[1]), F32)
    return jnp.concatenate([x, pad], axis=0).astype(BF16)


def _stack_heads(q_ref_slice, width):
    return jnp.concatenate([q_ref_slice[:, HEAD_DIM * r:HEAD_DIM * (r + 1)] for r in range(NSA_GROUP)], axis=0)


def _log2_sigmoid(z2):
    return jnp.minimum(z2, 0.0) - jnp.log(1.0 + jnp.exp2(-jnp.abs(z2))) * LOG2E


def _strict_upper(n):
    return jnp.where(lax.broadcasted_iota(jnp.int32, (n, n), 0) > lax.broadcasted_iota(jnp.int32, (n, n), 1),
                     1.0, 0.0).astype(BF16)


def _prompt_cmp_kernel(q_ref, k0_ref, k1_ref, v0_ref, v1_ref, w_ref, bias_ref, o_ref, sel_ref, ck_ref, cv_ref,
                       *, tq, nc):
    qt = pl.program_id(1)

    @pl.when(qt == 0)
    def _():
        for g in range(NSA_KV_HEADS):
            for kv, src, dst in ((0, (k0_ref, k1_ref)[g], ck_ref), (1, (v0_ref, v1_ref)[g], cv_ref)):
                first = jnp.zeros((nc, HEAD_DIM), F32)
                second = jnp.zeros((nc, HEAD_DIM), F32)
                for j in range(CMP_STRIDE):
                    rows = src[0, pl.ds(j, nc, stride=CMP_STRIDE), :]
                    first = first + rows * w_ref[kv, j:j + 1, :]
                    second = second + rows * w_ref[kv, CMP_STRIDE + j:CMP_STRIDE + j + 1, :]
                dst[g] = (first + pltpu.roll(second, nc - 1, 0)).astype(BF16)

    nb = nc // CMP_PER_SEL
    pool = jnp.where(lax.broadcasted_iota(jnp.int32, (nc, nb), 0) // CMP_PER_SEL
                     == lax.broadcasted_iota(jnp.int32, (nc, nb), 1), 1.0, 0.0).astype(BF16)
    chunks = [(h, off) for h in range(NSA_HEADS) for off in range(0, tq, ROW_CHUNK)]

    def scores(h, off):
        bias = bias_ref[h, off:off + ROW_CHUNK, :]
        q = q_ref[0, off:off + ROW_CHUNK, h * HEAD_DIM:(h + 1) * HEAD_DIM]
        return _dot_nt(q, ck_ref[h // NSA_GROUP]) * SCALE2 + bias, bias

    imp = {(g, off): None for g in range(NSA_KV_HEADS) for off in range(0, tq, ROW_CHUNK)}
    nxt = scores(*chunks[0])
    for c, (h, off) in enumerate(chunks):
        s, bias = nxt
        if c + 1 < len(chunks):
            nxt = scores(*chunks[c + 1])
        p = _masked_softmax2(s, bias > 0.5 * NEG)
        o_ref[0, off:off + ROW_CHUNK, h * HEAD_DIM:(h + 1) * HEAD_DIM] = _dot(p.astype(BF16), cv_ref[h // NSA_GROUP])
        key = (h // NSA_GROUP, off)
        imp[key] = p if imp[key] is None else imp[key] + p
    imp = jnp.concatenate(
        [jnp.concatenate([_split_dot(imp[(g, off)], pool) for off in range(0, tq, ROW_CHUNK)], axis=0)
         for g in range(NSA_KV_HEADS)], axis=1)
    pos = qt * tq + lax.broadcasted_iota(jnp.int32, imp.shape, 0)
    sel = _select_blocks(imp, pos, N_SEL, NSA_KV_HEADS)
    for g in range(NSA_KV_HEADS):
        sel_ref[0, g] = sel[:, g * nb:(g + 1) * nb]


def _prompt_cmp(q, kvf, w_cmp, bias_cmp, b, t):
    nc = t // CMP_STRIDE
    nb = nc // CMP_PER_SEL
    q3 = q.reshape(b, t, q.shape[-1])
    kv3 = kvf.reshape(b, t, kvf.shape[-1])
    return pl.pallas_call(
        functools.partial(_prompt_cmp_kernel, tq=TQ, nc=nc),
        grid=(b, t // TQ),
        in_specs=[pl.BlockSpec((1, TQ, NSA_WIDTH), lambda i, j: (i, j, 0)),
                  *[pl.BlockSpec((1, t, HEAD_DIM), lambda i, j, c=c: (i, 0, c)) for c in range(4)],
                  pl.BlockSpec((2, CMP_BLOCK, HEAD_DIM), lambda i, j: (0, 0, 0)),
                  pl.BlockSpec((NSA_HEADS, TQ, nc), lambda i, j: (0, j, 0))],
        out_specs=[pl.BlockSpec((1, TQ, NSA_WIDTH), lambda i, j: (i, j, 0)),
                   pl.BlockSpec((1, NSA_KV_HEADS, TQ, nb), lambda i, j: (i, 0, j, 0))],
        out_shape=[jax.ShapeDtypeStruct((b, t, NSA_WIDTH), F32),
                   jax.ShapeDtypeStruct((b, NSA_KV_HEADS, t, nb), F32)],
        scratch_shapes=[pltpu.VMEM((NSA_KV_HEADS, nc, HEAD_DIM), BF16),
                        pltpu.VMEM((NSA_KV_HEADS, nc, HEAD_DIM), BF16)],
        compiler_params=_cparams(("parallel", "arbitrary")),
        name="prompt_cmp",
    )(q3, kv3, kv3, kv3, kv3, w_cmp, bias_cmp)


def _prompt_selwin_kernel(q_ref, ks_ref, vs_ref, kw_ref, vw_ref, sel_ref, tb_ref, osel_ref, owin_ref,
                          m_ref, l_ref, acc_ref, *, tq, tk):
    i = pl.program_id(2)
    selb = sel_ref[0, 0].astype(BF16)
    nb = selb.shape[1]
    blocks_per_tile = tk // SEL_BLOCK

    def tile(k_ref, v_ref, j, n, table, extra):
        start = pl.multiple_of(j * tk, tk)
        k = k_ref[0, pl.ds(start, n * tk), :]
        v = v_ref[0, pl.ds(start, n * tk), :]
        chunks = [(r, off) for r in range(NSA_GROUP) for off in range(0, tq, ROW_CHUNK)]

        def scores(r, off):
            s = _dot_nt(q_ref[0, off:off + ROW_CHUNK, r * HEAD_DIM:(r + 1) * HEAD_DIM], k) * SCALE2
            if table is not None:
                s = s + tb_ref[table, r, off:off + ROW_CHUNK, :]
            if extra is not None:
                s = s + extra[off:off + ROW_CHUNK]
            return s

        rows = [slice(r * tq + off, r * tq + off + ROW_CHUNK) for r, off in chunks]
        state = [(m_ref[rs], l_ref[rs], acc_ref[rs]) for rs in rows]
        s_next = scores(*chunks[0])
        for c in range(len(chunks)):
            s = s_next
            if c + 1 < len(chunks):
                s_next = scores(*chunks[c + 1])
            state[c] = _flash_update(s, v, *state[c])
        for rs, (m, l, acc) in zip(rows, state):
            m_ref[rs] = m
            l_ref[rs] = l
            acc_ref[rs] = acc

    def sel_tile(j, n, table):
        erow = lax.broadcasted_iota(jnp.int32, (nb, n * tk), 0)
        ecol = lax.broadcasted_iota(jnp.int32, (nb, n * tk), 1) // SEL_BLOCK
        expand = jnp.where(erow == j * blocks_per_tile + ecol, 1.0, 0.0).astype(BF16)
        tile(ks_ref, vs_ref, j, n, table, (1.0 - _dot(selb, expand)) * NEG)

    def win_tile(j, table):
        tile(kw_ref, vw_ref, j, 1, table, None)

    def finish(o_ref):
        for r in range(NSA_GROUP):
            rs = slice(r * tq, (r + 1) * tq)
            o_ref[0, :, r * HEAD_DIM:(r + 1) * HEAD_DIM] = acc_ref[rs] / l_ref[rs]

    _flash_init(m_ref, l_ref, acc_ref)

    n_far = jnp.maximum(i - 1, 0)

    def far_body(jj, carry):
        sel_tile(2 * jj, 2, None)
        return carry

    lax.fori_loop(0, n_far // 2, far_body, 0)

    @pl.when(n_far % 2 == 1)
    def _():
        sel_tile(i - 2, 1, None)

    @pl.when(i >= 1)
    def _():
        sel_tile(i - 1, 1, 1)

    sel_tile(i, 1, 0)
    finish(osel_ref)

    _flash_init(m_ref, l_ref, acc_ref)
    win_tile(i, 0)

    @pl.when(i >= 1)
    def _():
        win_tile(i - 1, 1)

    @pl.when(i >= 2)
    def _():
        win_tile(i - 2, 2)

    finish(owin_ref)


def _prompt_selwin(q, kvb, sel, tables, b, t):
    nb = sel.shape[-1]
    q3 = q.reshape(b, t, q.shape[-1])
    kv3 = kvb.reshape(b, t, kvb.shape[-1])
    gw = NSA_GROUP * HEAD_DIM
    kcol = lambda base: pl.BlockSpec((1, t, HEAD_DIM), lambda i, g, j: (i, 0, base + g))
    return pl.pallas_call(
        functools.partial(_prompt_selwin_kernel, tq=TQ, tk=TK),
        grid=(b, NSA_KV_HEADS, t // TQ),
        in_specs=[pl.BlockSpec((1, TQ, gw), lambda i, g, j: (i, j, g)),
                  kcol(4), kcol(6), kcol(8), kcol(10),
                  pl.BlockSpec((1, 1, TQ, nb), lambda i, g, j: (i, g, j, 0)),
                  pl.BlockSpec((3, NSA_GROUP, TQ, TK), lambda i, g, j: (0, g, 0, 0))],
        out_specs=[pl.BlockSpec((1, TQ, gw), lambda i, g, j: (i, j, g)),
                   pl.BlockSpec((1, TQ, gw), lambda i, g, j: (i, j, g))],
        out_shape=[jax.ShapeDtypeStruct((b, t, NSA_WIDTH), F32),
                   jax.ShapeDtypeStruct((b, t, NSA_WIDTH), F32)],
        scratch_shapes=[pltpu.VMEM((NSA_GROUP * TQ, HEAD_DIM), F32),
                        pltpu.VMEM((NSA_GROUP * TQ, HEAD_DIM), F32),
                        pltpu.VMEM((NSA_GROUP * TQ, HEAD_DIM), F32)],
        compiler_params=_cparams(("parallel", "parallel", "parallel")),
        name="prompt_selwin",
    )(q3, kv3, kv3, kv3, kv3, sel, tables)


def _sb_weights(z, carry, upper, valid):
    seg = upper.shape[0]
    ls = _log2_sigmoid(z)
    l1m = ls - z
    if valid is not None:
        l1m = jnp.where(valid, l1m, 0.0)
    cums = []
    for sg in reversed(range(z.shape[1] // seg)):
        part = l1m[:, sg * seg:(sg + 1) * seg]
        cums.append(_split_dot(part, upper) + carry)
        carry = carry + jnp.sum(part, axis=1, keepdims=True)
    cum = cums[0] if len(cums) == 1 else jnp.concatenate(cums[::-1], axis=1)
    a = jnp.exp2(ls + cum)
    if valid is not None:
        a = jnp.where(valid, a, 0.0)
    return a, carry


def _prompt_sb_kernel(q_ref, k_ref, v_ref, o_ref, acc_ref, carry_ref, *, tq, tk):
    i = pl.program_id(2)
    upper = _strict_upper(tk)
    acc_ref[...] = jnp.zeros(acc_ref.shape, F32)
    carry_ref[...] = jnp.zeros(carry_ref.shape, F32)

    def run(block, diag):
        start = pl.multiple_of(block * tq, tq)
        k = k_ref[0, pl.ds(start, tq), :]
        v = v_ref[0, pl.ds(start, tq), :]
        chunks = [slice(off, off + SB_ROW_CHUNK) for off in range(0, tq, SB_ROW_CHUNK)]
        accs = [acc_ref[rs, :] for rs in chunks]
        zs = [_dot_nt(q_ref[0, rs, :], k) * SCALE2 for rs in chunks]
        valids = [None] * len(chunks)
        if diag:
            valids = [(lax.broadcasted_iota(jnp.int32, (SB_ROW_CHUNK, tq), 1)
                       < lax.broadcasted_iota(jnp.int32, (SB_ROW_CHUNK, tq), 0) + rs.start) for rs in chunks]
        weights = [_sb_weights(z, carry_ref[rs, 0:1], upper, valid) for z, rs, valid in zip(zs, chunks, valids)]
        outs = [_dot(a.astype(BF16), v) for a, _ in weights]
        for rs, acc, o, (_, carry) in zip(chunks, accs, outs, weights):
            acc_ref[rs, :] = acc + o
            carry_ref[rs, :] = jnp.broadcast_to(carry, (SB_ROW_CHUNK, HEAD_DIM))

    run(i, True)

    def body(jj, c):
        run(i - 1 - jj, False)
        return c

    lax.fori_loop(0, i, body, 0)
    o_ref[0] = acc_ref[...]


def _prompt_sb(q, kvb, b, t):
    q3 = q.reshape(b, t, q.shape[-1])
    kv3 = kvb.reshape(b, t, kvb.shape[-1])
    qbase = NSA_WIDTH // HEAD_DIM
    return pl.pallas_call(
        functools.partial(_prompt_sb_kernel, tq=TQ_SB, tk=TK),
        grid=(b, SB_HEADS, t // TQ_SB),
        in_specs=[pl.BlockSpec((1, TQ_SB, HEAD_DIM), lambda i, h, j: (i, j, qbase + h)),
                  pl.BlockSpec((1, t, HEAD_DIM), lambda i, h, j: (i, 0, 12 + h)),
                  pl.BlockSpec((1, t, HEAD_DIM), lambda i, h, j: (i, 0, 16 + h))],
        out_specs=pl.BlockSpec((1, TQ_SB, HEAD_DIM), lambda i, h, j: (i, j, h)),
        out_shape=jax.ShapeDtypeStruct((b, t, SB_WIDTH), F32),
        scratch_shapes=[pltpu.VMEM((TQ_SB, HEAD_DIM), F32), pltpu.VMEM((TQ_SB, HEAD_DIM), F32)],
        compiler_params=_cparams(("parallel", "parallel", "parallel")),
        name="prompt_sb",
    )(q3, kv3, kv3)


def _ssm_kernel(u_ref, h0_ref, wre_ref, wim_ref, cre_ref, cim_ref, d_ref, pre_ref, pim_ref,
                y_ref, hl_ref, hr_ref, hi_ref, *, t, tc):
    p8r = pre_ref[...]
    p8i = pim_ref[...]
    row = lax.broadcasted_iota(jnp.int32, (tc, SSM_TILE), 0) % SUBLANES

    def chunk(c, carry):
        cr, ci = carry
        start = pl.multiple_of(c * tc, tc)
        u = u_ref[0, pl.ds(start, tc), :]
        ub = u.astype(BF16)
        xr = _dot(ub, wre_ref[0])
        xi = _dot(ub, wim_ref[0])
        for sh in (1, 2, 4):
            ar = p8r[sh - 1:sh, :]
            ai = p8i[sh - 1:sh, :]
            sr = jnp.where(row >= sh, pltpu.roll(xr, sh, 0), 0.0)
            si = jnp.where(row >= sh, pltpu.roll(xi, sh, 0), 0.0)
            xr, xi = xr + ar * sr - ai * si, xi + ar * si + ai * sr
        for g in range(tc // SUBLANES):
            lo, hi = g * SUBLANES, (g + 1) * SUBLANES
            br = xr[lo:hi] + p8r * cr - p8i * ci
            bi = xi[lo:hi] + p8r * ci + p8i * cr
            hr_ref[lo:hi, :] = br
            hi_ref[lo:hi, :] = bi
            cr, ci = br[SUBLANES - 1:SUBLANES], bi[SUBLANES - 1:SUBLANES]
        y = (_dot(hr_ref[...].astype(BF16), cre_ref[0]) - _dot(hi_ref[...].astype(BF16), cim_ref[0])
             + d_ref[0] * u)
        y_ref[0, pl.ds(start, tc), :] = y
        return cr, ci

    cr, ci = lax.fori_loop(0, t // tc, chunk, (h0_ref[0, 0:1, :], h0_ref[0, 1:2, :]))
    hl_ref[0, 0:1, :] = cr
    hl_ref[0, 1:2, :] = ci


def _ssm(zc, h0, sw, b, t):
    wre, wim, cre, cim, dsk, pre, pim = sw
    tc = min(t, 256)
    u3 = zc.reshape(b, t, zc.shape[-1])
    ubase = (NSA_WIDTH + SB_WIDTH) // SSM_U_TILE
    wspec = lambda shp: pl.BlockSpec((1,) + shp, lambda i, k: (k, 0, 0))
    return pl.pallas_call(
        functools.partial(_ssm_kernel, t=t, tc=tc),
        grid=(b, SSM_TILES),
        in_specs=[pl.BlockSpec((1, t, SSM_U_TILE), lambda i, k: (i, 0, ubase + k)),
                  pl.BlockSpec((1, 2, SSM_TILE), lambda i, k: (i, 0, k)),
                  wspec((SSM_U_TILE, SSM_TILE)), wspec((SSM_U_TILE, SSM_TILE)),
                  wspec((SSM_TILE, SSM_U_TILE)), wspec((SSM_TILE, SSM_U_TILE)),
                  wspec((1, SSM_U_TILE)),
                  pl.BlockSpec((SUBLANES, SSM_TILE), lambda i, k: (0, k)),
                  pl.BlockSpec((SUBLANES, SSM_TILE), lambda i, k: (0, k))],
        out_specs=[pl.BlockSpec((1, t, SSM_U_TILE), lambda i, k: (i, 0, k)),
                   pl.BlockSpec((1, 2, SSM_TILE), lambda i, k: (i, 0, k))],
        out_shape=[jax.ShapeDtypeStruct((b, t, SSM_WIDTH), F32),
                   jax.ShapeDtypeStruct((b, 2, SSM_LANES), F32)],
        scratch_shapes=[pltpu.VMEM((tc, SSM_TILE), F32), pltpu.VMEM((tc, SSM_TILE), F32)],
        compiler_params=_cparams(("parallel", "parallel")),
        name="ssm_scan",
    )(u3, h0, wre, wim, cre, cim, dsk, pre, pim)


def _rms(h, gain):
    return h * lax.rsqrt(jnp.mean(h * h, axis=1, keepdims=True) + EPS) * gain


def _silu(z):
    return z * _sigmoid(z)


def _mix_kernel(ocmp_ref, osel_ref, owin_ref, gate_ref, ob_ref, yc_ref, za_ref, zb_ref, zc_ref, x_ref,
                wglu_ref, wout_ref, gain_ref, lng_ref, lnb_ref, y_ref, yb_ref, mixed_ref, *, alpha):
    gates = _sigmoid(gate_ref[...])
    for h in range(NSA_HEADS):
        sl = slice(h * HEAD_DIM, (h + 1) * HEAD_DIM)
        oa = (gates[:, 3 * h:3 * h + 1] * ocmp_ref[:, sl] + gates[:, 3 * h + 1:3 * h + 2] * osel_ref[:, sl]
              + gates[:, 3 * h + 2:3 * h + 3] * owin_ref[:, sl])
        mixed_ref[:, sl] = oa * _silu(za_ref[:, sl])
    mixed_ref[:, 0:NSA_WIDTH] = _rms(mixed_ref[:, 0:NSA_WIDTH], gain_ref[:, 0:NSA_WIDTH])
    b0, c0 = NSA_WIDTH, NSA_WIDTH + SB_WIDTH
    mixed_ref[:, b0:c0] = _rms(ob_ref[...] * _silu(zb_ref[...]), gain_ref[:, b0:c0])
    glu = _dot(yc_ref[...].astype(BF16), wglu_ref[...])
    yc = glu[:, 0:SSM_WIDTH] * _sigmoid(glu[:, SSM_WIDTH:2 * SSM_WIDTH])
    mixed_ref[:, c0:c0 + SSM_WIDTH] = _rms(yc * _silu(zc_ref[...]), gain_ref[:, c0:c0 + SSM_WIDTH])
    h = alpha * x_ref[...] + _dot(mixed_ref[...].astype(BF16), wout_ref[...])
    mu = jnp.mean(h, axis=1, keepdims=True)
    hc = h - mu
    var = jnp.mean(hc * hc, axis=1, keepdims=True)
    y = hc * lax.rsqrt(var + EPS) * lng_ref[...] + lnb_ref[...]
    y_ref[...] = y
    yb_ref[...] = y.astype(BF16)


def _mix_out(ocmp, osel, owin, gate, ob, yc, zc, x, wglu, wout, gain, lng, lnb, alpha):
    m, d = x.shape
    tm = min(m, 256)
    row = lambda w, c=0: pl.BlockSpec((tm, w), lambda i, c=c: (i, c))
    full = lambda a: pl.BlockSpec(a.shape, lambda i: (0,) * a.ndim)
    return pl.pallas_call(
        functools.partial(_mix_kernel, alpha=alpha),
        grid=(m // tm,),
        in_specs=[row(NSA_WIDTH), row(NSA_WIDTH), row(NSA_WIDTH), row(LANES), row(SB_WIDTH), row(SSM_WIDTH),
                  row(NSA_WIDTH, 0), row(SB_WIDTH, NSA_WIDTH // SB_WIDTH),
                  row(SSM_WIDTH, (NSA_WIDTH + SB_WIDTH + SSM_WIDTH) // SSM_WIDTH), row(d),
                  full(wglu), full(wout), full(gain), full(lng), full(lnb)],
        out_specs=[row(d), row(d)],
        out_shape=[jax.ShapeDtypeStruct((m, d), F32), jax.ShapeDtypeStruct((m, d), BF16)],
        scratch_shapes=[pltpu.VMEM((tm, d), F32)],
        compiler_params=_cparams(("parallel",)),
        name="mix_out",
    )(ocmp.reshape(m, -1), osel.reshape(m, -1), owin.reshape(m, -1), gate, ob.reshape(m, -1),
      yc.reshape(m, -1), zc, zc, zc, x, wglu, wout, gain, lng, lnb)


def _sample_compress_kernel(pt_ref, *refs):
    pages = refs[:PAGES_PER_STEP]
    w1_ref, w2_ref, first_ref, second_ref = refs[PAGES_PER_STEP:]
    page = pages[0].shape[2] // ROW_VECS
    per_page = page // CMP_STRIDE
    half = w1_ref.shape[1]
    pool = jnp.where(lax.broadcasted_iota(jnp.int32, (per_page, page), 1) // CMP_STRIDE
                     == lax.broadcasted_iota(jnp.int32, (per_page, page), 0), 1.0, 0.0).astype(BF16)
    for i, pg in enumerate(pages):
        rows = jnp.concatenate([_page_vec(pg, c, page) for c in range(2 * NSA_KV_HEADS)], axis=1)
        prod = jnp.concatenate([rows * w1_ref[...], rows * w2_ref[...]], axis=1)
        hi = prod.astype(BF16)
        lo = (prod - hi.astype(F32)).astype(BF16)
        sums = _dot(pool, hi) + _dot(pool, lo)
        first_ref[0, i * per_page:(i + 1) * per_page, :] = sums[:, 0:half]
        second_ref[0, i * per_page:(i + 1) * per_page, :] = sums[:, half:2 * half]


def _sample_compress(cache, layer, page_table, w_cmp):
    db, n_pages = page_table.shape
    page = cache.shape[2] // ROW_VECS
    per_page = page // CMP_STRIDE
    steps = n_pages // PAGES_PER_STEP
    nc = n_pages * per_page
    half = 4 * HEAD_DIM

    def pspec(i):
        return pl.BlockSpec((1, 1, page * ROW_VECS, HEAD_DIM),
                            lambda b, s, pt, i=i: (layer, pt[b, s * PAGES_PER_STEP + i], 0, 0))

    rows = PAGES_PER_STEP * per_page
    tiled = lambda w: jnp.concatenate([jnp.tile(w[kv], (per_page, 1)) for kv in (0, 0, 1, 1)], axis=1)
    w1, w2 = tiled(w_cmp[:, :CMP_STRIDE]), tiled(w_cmp[:, CMP_STRIDE:])
    return pl.pallas_call(
        _sample_compress_kernel,
        grid_spec=pltpu.PrefetchScalarGridSpec(
            num_scalar_prefetch=1,
            grid=(db, steps),
            in_specs=[pspec(i) for i in range(PAGES_PER_STEP)]
                     + [pl.BlockSpec((page, half), lambda b, s, pt: (0, 0)),
                        pl.BlockSpec((page, half), lambda b, s, pt: (0, 0))],
            out_specs=[pl.BlockSpec((1, rows, half), lambda b, s, pt: (b, s, 0)),
                       pl.BlockSpec((1, rows, half), lambda b, s, pt: (b, s, 0))]),
        out_shape=[jax.ShapeDtypeStruct((db, nc, half), F32), jax.ShapeDtypeStruct((db, nc, half), F32)],
        compiler_params=_cparams(("parallel", "parallel")),
        name="sample_compress",
    )(page_table, *([cache] * PAGES_PER_STEP), w1, w2)


def _sample_cmpwin_kernel(q_ref, first_ref, second_ref, bias_ref, win_ref, wbias_ref,
                          ocmp_ref, sel_ref, owin_ref, *, ts, nc, pos0):
    comp = first_ref[0] + pltpu.roll(second_ref[0], nc - 1, 0)
    nb = nc // CMP_PER_SEL
    pos = pos0 + lax.broadcasted_iota(jnp.int32, (ts, nb), 0)
    for g in range(NSA_KV_HEADS):
        qg = _stack_heads(q_ref[0, :, pl.ds(g * NSA_GROUP * HEAD_DIM, NSA_GROUP * HEAD_DIM)], HEAD_DIM)
        ck = comp[:, g * HEAD_DIM:(g + 1) * HEAD_DIM].astype(BF16)
        cv = comp[:, (2 + g) * HEAD_DIM:(3 + g) * HEAD_DIM].astype(BF16)
        bias = bias_ref[NSA_GROUP * g:NSA_GROUP * (g + 1)].reshape(NSA_GROUP * ts, nc)
        o, imp = _cmp_attend(qg, ck, cv, bias, ts)
        sel_ref[0, g] = _select_blocks(imp, pos, N_SEL - 1)
        kw = win_ref[0, :, g * HEAD_DIM:(g + 1) * HEAD_DIM].astype(BF16)
        vw = win_ref[0, :, (2 + g) * HEAD_DIM:(3 + g) * HEAD_DIM].astype(BF16)
        wb = wbias_ref[NSA_GROUP * g:NSA_GROUP * (g + 1)].reshape(NSA_GROUP * ts, kw.shape[0])
        sw = _dot_nt(qg, kw) * SCALE + wb
        ow = _dot(_masked_softmax(sw, wb > 0.5 * NEG).astype(BF16), vw)
        for r in range(NSA_GROUP):
            h = NSA_GROUP * g + r
            ocmp_ref[0, :, h * HEAD_DIM:(h + 1) * HEAD_DIM] = o[r * ts:(r + 1) * ts]
            owin_ref[0, :, h * HEAD_DIM:(h + 1) * HEAD_DIM] = ow[r * ts:(r + 1) * ts]


def _sample_cmpwin(q, first, second, bias_cmp, win_all, bias_win, db, ts, past_len):
    nc = first.shape[1]
    nb = nc // CMP_PER_SEL
    nw = win_all.shape[1]
    q3 = q.reshape(db, ts, q.shape[-1])
    return pl.pallas_call(
        functools.partial(_sample_cmpwin_kernel, ts=ts, nc=nc, pos0=past_len),
        grid=(db,),
        in_specs=[pl.BlockSpec((1, ts, NSA_WIDTH), lambda b: (b, 0, 0)),
                  pl.BlockSpec((1, nc, 4 * HEAD_DIM), lambda b: (b, 0, 0)),
                  pl.BlockSpec((1, nc, 4 * HEAD_DIM), lambda b: (b, 0, 0)),
                  pl.BlockSpec((NSA_HEADS, ts, nc), lambda b: (0, 0, 0)),
                  pl.BlockSpec((1, nw, 4 * HEAD_DIM), lambda b: (b, 0, 0)),
                  pl.BlockSpec((NSA_HEADS, ts, nw), lambda b: (0, 0, 0))],
        out_specs=[pl.BlockSpec((1, ts, NSA_WIDTH), lambda b: (b, 0, 0)),
                   pl.BlockSpec((1, NSA_KV_HEADS, ts, nb), lambda b: (b, 0, 0, 0)),
                   pl.BlockSpec((1, ts, NSA_WIDTH), lambda b: (b, 0, 0))],
        out_shape=[jax.ShapeDtypeStruct((db, ts, NSA_WIDTH), F32),
                   jax.ShapeDtypeStruct((db, NSA_KV_HEADS, ts, nb), F32),
                   jax.ShapeDtypeStruct((db, ts, NSA_WIDTH), F32)],
        compiler_params=_cparams(("parallel",)),
        name="sample_cmpwin",
    )(q3, first, second, bias_cmp, win_all, bias_win)


def _sample_sel_kernel(pt_ref, *refs, ts, page):
    pages = refs[:PAGES_PER_STEP]
    q_ref, sel_ref, new_ref, bias_ref, nbias_ref, o_ref, m_ref, l_ref, acc_ref = refs[PAGES_PER_STEP:]
    s_idx = pl.program_id(1)
    last = pl.num_programs(1) - 1
    rows = NSA_GROUP * ts
    nkeys = PAGES_PER_STEP * page
    blocks = nkeys // SEL_BLOCK

    @pl.when(s_idx == 0)
    def _():
        _flash_init(m_ref, l_ref, acc_ref)

    expand = jnp.where(lax.broadcasted_iota(jnp.int32, (blocks, nkeys), 0)
                       == lax.broadcasted_iota(jnp.int32, (blocks, nkeys), 1) // SEL_BLOCK, 1.0, 0.0).astype(BF16)
    for g in range(NSA_KV_HEADS):
        qg = _stack_heads(q_ref[0, :, pl.ds(g * NSA_GROUP * HEAD_DIM, NSA_GROUP * HEAD_DIM)], HEAD_DIM)
        k = jnp.concatenate([_page_vec(pg, 4 + g, page) for pg in pages], axis=0).astype(BF16)
        v = jnp.concatenate([_page_vec(pg, 6 + g, page) for pg in pages], axis=0).astype(BF16)
        bias = bias_ref[jnp.where(s_idx == last, 1, 0), NSA_GROUP * g:NSA_GROUP * (g + 1)].reshape(rows, nkeys)
        s = _dot_nt(qg, k) * SCALE2 + bias
        chosen = _dot(sel_ref[0, g, 0].astype(BF16), expand) > 0.5
        s = jnp.where(chosen[None], s.reshape(NSA_GROUP, ts, nkeys), NEG).reshape(rows, nkeys)
        sl = slice(g * rows, (g + 1) * rows)
        _flash_tile(s, v, m_ref.at[sl], l_ref.at[sl], acc_ref.at[sl])

        @pl.when(s_idx == last)
        def _():
            kn = _pad_keys(new_ref[0, :, g * HEAD_DIM:(g + 1) * HEAD_DIM])
            vn = _pad_keys(new_ref[0, :, (2 + g) * HEAD_DIM:(3 + g) * HEAD_DIM])
            nbias = nbias_ref[NSA_GROUP * g:NSA_GROUP * (g + 1)].reshape(rows, LANES)
            _flash_tile(_dot_nt(qg, kn) * SCALE2 + nbias, vn, m_ref.at[sl], l_ref.at[sl], acc_ref.at[sl])
            o = acc_ref[sl] / l_ref[sl]
            for r in range(NSA_GROUP):
                h = NSA_GROUP * g + r
                o_ref[0, :, h * HEAD_DIM:(h + 1) * HEAD_DIM] = o[r * ts:(r + 1) * ts]


def _sample_sel(cache, layer, page_table, q, sel_steps, kv_new, bias_steps, bias_new, db, ts):
    n_pages = page_table.shape[1]
    page = cache.shape[2] // ROW_VECS
    steps = n_pages // PAGES_PER_STEP
    half = 4 * HEAD_DIM
    nkeys = PAGES_PER_STEP * page
    blocks = nkeys // SEL_BLOCK
    q3 = q.reshape(db, ts, q.shape[-1])
    new3 = kv_new.reshape(db, ts, kv_new.shape[-1])

    def pspec(i):
        return pl.BlockSpec((1, 1, page * ROW_VECS, HEAD_DIM),
                            lambda b, s, pt, i=i: (layer, pt[b, s * PAGES_PER_STEP + i], 0, 0))

    return pl.pallas_call(
        functools.partial(_sample_sel_kernel, ts=ts, page=page),
        grid_spec=pltpu.PrefetchScalarGridSpec(
            num_scalar_prefetch=1,
            grid=(db, steps),
            in_specs=[pspec(i) for i in range(PAGES_PER_STEP)]
                     + [pl.BlockSpec((1, ts, NSA_WIDTH), lambda b, s, pt: (b, 0, 0)),
                        pl.BlockSpec((1, NSA_KV_HEADS, 1, ts, blocks), lambda b, s, pt: (b, 0, s, 0, 0)),
                        pl.BlockSpec((1, ts, half), lambda b, s, pt: (b, 0, 1)),
                        pl.BlockSpec((2, NSA_HEADS, ts, nkeys), lambda b, s, pt: (0, 0, 0, 0)),
                        pl.BlockSpec((NSA_HEADS, ts, LANES), lambda b, s, pt: (0, 0, 0))],
            out_specs=pl.BlockSpec((1, ts, NSA_WIDTH), lambda b, s, pt: (b, 0, 0)),
            scratch_shapes=[pltpu.VMEM((NSA_HEADS * ts, HEAD_DIM), F32),
                            pltpu.VMEM((NSA_HEADS * ts, HEAD_DIM), F32),
                            pltpu.VMEM((NSA_HEADS * ts, HEAD_DIM), F32)]),
        out_shape=jax.ShapeDtypeStruct((db, ts, NSA_WIDTH), F32),
        compiler_params=_cparams(("parallel", "arbitrary")),
        name="sample_sel",
    )(page_table, *([cache] * PAGES_PER_STEP), q3, sel_steps, new3, bias_steps, bias_new)


def _sample_sb_kernel(pt_ref, *refs, ts, page, seg):
    pages = refs[:PAGES_PER_STEP]
    q_ref, knew_ref, vnew_ref, o_ref, carry_ref, acc_ref = refs[PAGES_PER_STEP:]
    s_idx = pl.program_id(1)
    last = pl.num_programs(1) - 1
    upper = _strict_upper(seg)

    rows = SB_HEADS * ts
    heads = range(SB_HEADS)
    hs = lambda h: slice(h * HEAD_DIM, (h + 1) * HEAD_DIM)

    def attend(ks, vs, carry, up, valid):
        z = jnp.concatenate([_dot_nt(q_ref[0, :, hs(h)], ks[h]) for h in heads], axis=0) * SCALE2
        a, carry = _sb_weights(z, carry, up, valid)
        return jnp.concatenate([_dot(a[h * ts:(h + 1) * ts].astype(BF16), vs[h]) for h in heads], axis=0), carry

    @pl.when(s_idx == 0)
    def _():
        kn = [_pad_keys(knew_ref[0, :, hs(h)]) for h in heads]
        vn = [_pad_keys(vnew_ref[0, :, hs(h)]) for h in heads]
        valid = (lax.broadcasted_iota(jnp.int32, (rows, LANES), 1)
                 < lax.broadcasted_iota(jnp.int32, (rows, LANES), 0) % ts)
        o, carry = attend(kn, vn, jnp.zeros((rows, 1), F32), _strict_upper(LANES), valid)
        acc_ref[...] = o
        carry_ref[...] = jnp.broadcast_to(carry, (rows, HEAD_DIM))

    ks = [jnp.concatenate([_page_vec(pg, h, page) for pg in pages], axis=0).astype(BF16) for h in heads]
    vs = [jnp.concatenate([_page_vec(pg, SB_HEADS + h, page) for pg in pages], axis=0).astype(BF16) for h in heads]
    o, carry = attend(ks, vs, carry_ref[:, 0:1], upper, None)
    acc_ref[...] = acc_ref[...] + o
    carry_ref[...] = jnp.broadcast_to(carry, (rows, HEAD_DIM))

    @pl.when(s_idx == last)
    def _():
        for h in heads:
            o_ref[0, :, hs(h)] = acc_ref[h * ts:(h + 1) * ts, :]


def _sample_sb(cache, layer, page_table, q, kv_new, db, ts):
    n_pages = page_table.shape[1]
    page = cache.shape[2] // ROW_VECS
    steps = n_pages // PAGES_PER_STEP
    q3 = q.reshape(db, ts, q.shape[-1])
    new3 = kv_new.reshape(db, ts, kv_new.shape[-1])

    def pspec(i):
        return pl.BlockSpec((1, 1, page * ROW_VECS, HEAD_DIM),
                            lambda b, s, pt, i=i: (layer, pt[b, (steps - 1 - s) * PAGES_PER_STEP + i], 0, 0))

    return pl.pallas_call(
        functools.partial(_sample_sb_kernel, ts=ts, page=page, seg=TK),
        grid_spec=pltpu.PrefetchScalarGridSpec(
            num_scalar_prefetch=1,
            grid=(db, steps),
            in_specs=[pspec(i) for i in range(PAGES_PER_STEP)]
                     + [pl.BlockSpec((1, ts, SB_WIDTH), lambda b, s, pt: (b, 0, NSA_WIDTH // SB_WIDTH)),
                        pl.BlockSpec((1, ts, SB_WIDTH), lambda b, s, pt: (b, 0, 3)),
                        pl.BlockSpec((1, ts, SB_WIDTH), lambda b, s, pt: (b, 0, 4))],
            out_specs=pl.BlockSpec((1, ts, SB_WIDTH), lambda b, s, pt: (b, 0, 0)),
            scratch_shapes=[pltpu.VMEM((SB_HEADS * ts, HEAD_DIM), F32),
                            pltpu.VMEM((SB_HEADS * ts, HEAD_DIM), F32)]),
        out_shape=jax.ShapeDtypeStruct((db, ts, SB_WIDTH), F32),
        compiler_params=_cparams(("parallel", "arbitrary")),
        name="sample_sb",
    )(page_table, *([cache] * PAGES_PER_STEP), q3, new3, new3)


def _t5_bucket(dist):
    n = jnp.maximum(dist, 0)
    exact = NUM_BUCKETS // 2
    nf = jnp.maximum(n, 1).astype(F32)
    large = exact + (jnp.log(nf / exact) / math.log(MAX_DISTANCE / exact) * (NUM_BUCKETS - exact)).astype(jnp.int32)
    return jnp.where(n < exact, n, jnp.minimum(large, NUM_BUCKETS - 1))


def _bias(rel_bias, delta, valid):
    bucket = _t5_bucket(delta)[None]
    tab = rel_bias.astype(F32).T.reshape((rel_bias.shape[1], NUM_BUCKETS) + (1,) * delta.ndim)
    b = jnp.zeros((rel_bias.shape[1],) + delta.shape, F32)
    for k in range(NUM_BUCKETS):
        b = jnp.where(bucket == k, tab[:, k], b)
    return jnp.where(valid[None], b, NEG)


def _prompt_tables(rel_bias, t):
    tt = jnp.arange(TQ)[:, None]
    ss = jnp.arange(TK)[None, :]
    d = tt - ss
    assert 2 * TK - TQ >= MAX_DISTANCE
    far = _bias(rel_bias, d + 2 * TK, d > -2 * TK)
    tables = jnp.stack([
        _bias(rel_bias, d, d >= 0) - far,
        _bias(rel_bias, d + TK, d > -TK) - far,
        jnp.where(d + 2 * TK < WINDOW, 0.0, NEG) + 0.0 * far,
    ])
    nc = t // CMP_STRIDE
    dc = jnp.arange(t)[:, None] - (jnp.arange(nc)[None, :] * CMP_STRIDE + CMP_BLOCK - 1)
    return tables * LOG2E, _bias(rel_bias, dc, dc >= 0) * LOG2E


def _sample_tables(rel_bias, ts, past_len, page, nw, nw_pad):
    qpos = past_len + jnp.arange(ts)[:, None]
    nc = past_len // CMP_STRIDE
    dc = qpos - (jnp.arange(nc)[None, :] * CMP_STRIDE + CMP_BLOCK - 1)
    bias_cmp = _bias(rel_bias, dc, dc >= 0)
    widx = jnp.arange(nw_pad)[None, :]
    dw = qpos - (past_len - nw + widx)
    bias_win = _bias(rel_bias, dw, (dw >= 0) & (dw < WINDOW) & (widx < nw + ts))
    nkeys = PAGES_PER_STEP * page
    d_last = qpos - (past_len - nkeys + jnp.arange(nkeys)[None, :])
    d_far = d_last + nkeys
    bias_steps = jnp.stack([_bias(rel_bias, d_far, d_far >= 0), _bias(rel_bias, d_last, d_last >= 0)])
    nidx = jnp.arange(LANES)[None, :]
    dn = jnp.arange(ts)[:, None] - nidx
    return bias_cmp, bias_win, bias_steps * LOG2E, _bias(rel_bias, dn, (dn >= 0) & (nidx < ts)) * LOG2E


def _ssm_weights(lam_re, lam_im, log_dt, b_re, b_im, c_re, c_im, d_skip):
    lam_re, lam_im = lam_re.astype(F32), lam_im.astype(F32)
    dt = jnp.exp(log_dt.astype(F32))[:, None]
    mag = jnp.exp(lam_re * dt)
    a_re, a_im = mag * jnp.cos(lam_im * dt), mag * jnp.sin(lam_im * dt)
    den = lam_re * lam_re + lam_im * lam_im
    f_re = ((a_re - 1.0) * lam_re + a_im * lam_im) / den
    f_im = (a_im * lam_re - (a_re - 1.0) * lam_im) / den
    b_re, b_im = b_re.astype(F32), b_im.astype(F32)
    bb_re = f_re[..., None] * b_re - f_im[..., None] * b_im
    bb_im = f_re[..., None] * b_im + f_im[..., None] * b_re
    gpt = SSM_GROUPS // SSM_TILES
    eye = jnp.eye(gpt, dtype=F32)

    def in_mat(bb):
        return jnp.einsum('kgpn,gh->kgnhp', bb.reshape(SSM_TILES, gpt, SSM_STATE, SSM_CH), eye).reshape(
            SSM_TILES, SSM_U_TILE, SSM_TILE).astype(BF16)

    def out_mat(c):
        return jnp.einsum('kgnp,gh->khpgn', c.astype(F32).reshape(SSM_TILES, gpt, SSM_CH, SSM_STATE), eye).reshape(
            SSM_TILES, SSM_TILE, SSM_U_TILE).astype(BF16)

    pr, pi = a_re.reshape(1, SSM_LANES), a_im.reshape(1, SSM_LANES)
    while pr.shape[0] < SUBLANES:
        tr, ti = pr[-1:], pi[-1:]
        pr, pi = (jnp.concatenate([pr, pr * tr - pi * ti]), jnp.concatenate([pi, pr * ti + pi * tr]))
    return (in_mat(bb_re), in_mat(bb_im), out_mat(c_re), out_mat(c_im),
            d_skip.astype(F32).reshape(SSM_TILES, 1, SSM_U_TILE), pr, pi)


def _split_w_in(w_in):
    sizes = (NSA_WIDTH, 6 * NSA_KV_HEADS * HEAD_DIM, 3 * NSA_HEADS, NSA_WIDTH,
             SB_WIDTH, SB_WIDTH, SB_WIDTH, SB_WIDTH, SSM_WIDTH, SSM_WIDTH)
    offs = [0]
    for s in sizes:
        offs.append(offs[-1] + s)
    q_a, kv_a, g_a, z_a, q_b, k_b, v_b, z_b, u_c, z_c = (w_in[..., offs[i]:offs[i + 1]] for i in range(10))
    cat = lambda xs: jnp.concatenate(xs, axis=-1).astype(BF16)
    pad = jnp.zeros(g_a.shape[:-1] + (LANES - g_a.shape[-1],), g_a.dtype)
    return cat([q_a, q_b]), cat([kv_a, k_b, v_b]), cat([z_a, z_b, u_c, z_c]), cat([g_a, pad])


KV_TILE = 4 * HEAD_DIM


def _kv_proj_kernel(*refs, tm):
    x_ref, w_ref = refs[:2]
    kvf_ref, kvb_ref, nsa_ref, sb_ref = refs[-4:]
    j = pl.program_id(1)
    acc = _dot(x_ref[...], w_ref[...])
    kvf_ref[...] = acc
    kvb_ref[...] = acc.astype(BF16)

    def scatter(dst, base):
        for c in range(KV_TILE // HEAD_DIM):
            dst[0, pl.ds(base + c, tm, stride=ROW_VECS), :] = acc[:, c * HEAD_DIM:(c + 1) * HEAD_DIM]

    for step, dst, base in ((0, nsa_ref, 0), (1, nsa_ref, 4), (3, sb_ref, 0), (4, sb_ref, 4)):
        pl.when(j == step)(functools.partial(scatter, dst, base))


def _kv_proj(x, w, layer, depth, stacked):
    m, k = x.shape
    n = w.shape[1]
    tm = min(m, 1024)
    assert n == 5 * KV_TILE
    rows = pl.BlockSpec((1, tm * ROW_VECS, HEAD_DIM), lambda i, j: (layer, i, 0))
    stack_shape = jax.ShapeDtypeStruct((depth, m * ROW_VECS, HEAD_DIM), F32)
    return pl.pallas_call(
        functools.partial(_kv_proj_kernel, tm=tm),
        grid=(m // tm, n // KV_TILE),
        in_specs=[pl.BlockSpec((tm, k), lambda i, j: (i, 0)), pl.BlockSpec((k, KV_TILE), lambda i, j: (0, j)),
                  pl.BlockSpec(memory_space=pl.ANY), pl.BlockSpec(memory_space=pl.ANY)],
        out_specs=[pl.BlockSpec((tm, KV_TILE), lambda i, j: (i, j)), pl.BlockSpec((tm, KV_TILE), lambda i, j: (i, j)),
                   rows, rows],
        out_shape=[jax.ShapeDtypeStruct((m, n), F32), jax.ShapeDtypeStruct((m, n), BF16), stack_shape, stack_shape],
        input_output_aliases={2: 2, 3: 3},
        compiler_params=_cparams(("parallel", "arbitrary")),
        name="in_proj_kv",
    )(x, w, *stacked)


def _in_proj(xb, wl, layer, depth, stacked):
    wa, wb, wc, wd = wl
    (q,) = _matmul(xb, wa, (BF16,), "in_proj_q")
    kvf, kvb, nsa, sb = _kv_proj(xb, wb, layer, depth, stacked)
    (zc,) = _matmul(xb, wc, (F32,), "in_proj_z")
    (gate,) = _matmul(xb, wd, (F32,), "in_proj_gate")
    return q, kvf, kvb, zc, gate, (nsa, sb)


def kernel(x_prompt, x_sample, cache_nsa, cache_sb, state_win, state_ssm, page_table, rel_bias, w_in, w_cmp,
           ssm_lam_re, ssm_lam_im, ssm_log_dt, ssm_b_re, ssm_b_im, ssm_c_re, ssm_c_im, ssm_d, w_glu, mix_gain,
           w_out, ln_g, ln_b):
    depth = w_in.shape[0]
    b, t, d = x_prompt.shape
    db, ts, _ = x_sample.shape
    n_phys, page = cache_nsa.shape[1], cache_nsa.shape[2]
    n_pages = page_table.shape[1]
    past_len = n_pages * page
    nw = state_win.shape[2]
    alpha = (2 * depth) ** 0.25
    assert t % TQ == 0 and t % TQ_SB == 0 and TQ_SB % TK == 0 and WINDOW == 2 * TK and TQ == TK and n_pages % PAGES_PER_STEP == 0
    assert past_len % SEL_BLOCK == 0 and ts % SUBLANES == 0 and ts <= SEL_BLOCK and nw == WINDOW

    w_groups = _split_w_in(w_in)
    w_glu_b, w_out_b = w_glu.astype(BF16), w_out.astype(BF16)
    ssm_w = jax.vmap(_ssm_weights)(ssm_lam_re, ssm_lam_im, ssm_log_dt, ssm_b_re, ssm_b_im, ssm_c_re, ssm_c_im, ssm_d)
    tables_p, bias_cmp_p = _prompt_tables(rel_bias, t)
    nw_pad = -(-(nw + ts) // LANES) * LANES
    bias_cmp_s, bias_win_s, bias_steps_s, bias_new_s = _sample_tables(rel_bias, ts, past_len, page, nw, nw_pad)
    cache_nsa_r = cache_nsa.reshape(depth, n_phys, page * ROW_VECS, HEAD_DIM)
    cache_sb_r = cache_sb.reshape(depth, n_phys, page * ROW_VECS, HEAD_DIM)
    w_cmp = w_cmp.astype(F32)
    page_table = page_table.astype(jnp.int32)

    xp, xs = x_prompt.reshape(b * t, d), x_sample.reshape(db * ts, d)
    xpb, xsb = xp.astype(BF16), xs.astype(BF16)
    outs = [[] for _ in range(8)]
    rows_p = (jnp.zeros((depth, b * t * ROW_VECS, HEAD_DIM), F32),) * 2
    rows_s = (jnp.zeros((depth, db * ts * ROW_VECS, HEAD_DIM), F32),) * 2
    for l in range(depth):
        wl = tuple(w[l] for w in w_groups)
        sw = tuple(w[l] for w in ssm_w)
        gain, lng, lnb = mix_gain[l].reshape(1, d), ln_g[l].reshape(1, d), ln_b[l].reshape(1, d)

        q, kvf, kvb, zc, gate, rows_p = _in_proj(xpb, wl, l, depth, rows_p)
        o_cmp, sel = _prompt_cmp(q, kvf, w_cmp[l], bias_cmp_p, b, t)
        o_sel, o_win = _prompt_selwin(q, kvb, sel, tables_p, b, t)
        o_b = _prompt_sb(q, kvb, b, t)
        y_c, h_p = _ssm(zc, jnp.zeros((b, 2, SSM_LANES), F32), sw, b, t)
        xp, xpb = _mix_out(o_cmp, o_sel, o_win, gate, o_b, y_c, zc, xp, w_glu_b[l], w_out_b[l], gain, lng, lnb,
                           alpha)
        kv3 = kvf.reshape(b, t, -1)
        outs[4].append(kv3[:, max(t - WINDOW, 0):, 1024:1536].reshape(b, -1, 2, NSA_KV_HEADS, HEAD_DIM))
        outs[6].append(h_p.reshape(b, 2, SSM_GROUPS, SSM_STATE))

        q, kvf, kvb, zc, gate, rows_s = _in_proj(xsb, wl, l, depth, rows_s)
        kv3 = kvf.reshape(db, ts, -1)
        win_all = jnp.concatenate([state_win[l].reshape(db, nw, -1), kv3[:, :, 1024:1536],
                                   jnp.zeros((db, nw_pad - nw - ts, 4 * HEAD_DIM), F32)], axis=1)
        first, second = _sample_compress(cache_nsa_r, l, page_table, w_cmp[l])
        o_cmp, sel, o_win = _sample_cmpwin(q, first, second, bias_cmp_s, win_all, bias_win_s, db, ts, past_len)
        steps = n_pages // PAGES_PER_STEP
        sel_steps = jnp.moveaxis(sel.reshape(db, NSA_KV_HEADS, ts, steps, -1), 3, 2)
        o_sel = _sample_sel(cache_nsa_r, l, page_table, q, sel_steps, kvf, bias_steps_s, bias_new_s, db, ts)
        o_b = _sample_sb(cache_sb_r, l, page_table, q, kvf, db, ts)
        y_c, h_s = _ssm(zc, state_ssm[l].reshape(db, 2, SSM_LANES).astype(F32), sw, db, ts)
        xs, xsb = _mix_out(o_cmp, o_sel, o_win, gate, o_b, y_c, zc, xs, w_glu_b[l], w_out_b[l], gain, lng, lnb,
                           alpha)
        outs[5].append(win_all[:, ts:ts + nw].reshape(db, nw, 2, NSA_KV_HEADS, HEAD_DIM))
        outs[7].append(h_s.reshape(db, 2, SSM_GROUPS, SSM_STATE))

    win_p, win_s, ssm_p, ssm_s = (jnp.stack(outs[i]) for i in (4, 5, 6, 7))
    return (xp.reshape(b, t, d), xs.reshape(db, ts, d),
            rows_p[0].reshape(depth, b, t, 4, NSA_KV_HEADS, HEAD_DIM),
            rows_s[0].reshape(depth, db, ts, 4, NSA_KV_HEADS, HEAD_DIM),
            rows_p[1].reshape(depth, b, t, 2, SB_HEADS, HEAD_DIM),
            rows_s[1].reshape(depth, db, ts, 2, SB_HEADS, HEAD_DIM),
            win_p, win_s, ssm_p, ssm_s)
```

```python
import functools
import math

import jax
import jax.numpy as jnp
from jax import lax
from jax.experimental import pallas as pl
from jax.experimental.pallas import tpu as pltpu

F32 = jnp.float32
BF16 = jnp.bfloat16

HEAD_DIM = 128
NSA_KV_HEADS = 2
NSA_GROUP = 4
NSA_HEADS = NSA_KV_HEADS * NSA_GROUP
NSA_WIDTH = NSA_HEADS * HEAD_DIM
SB_HEADS = 4
SB_WIDTH = SB_HEADS * HEAD_DIM
SSM_CH = 16
SSM_GROUPS = 32
SSM_STATE = 64
SSM_WIDTH = SSM_CH * SSM_GROUPS
SSM_LANES = SSM_GROUPS * SSM_STATE
CMP_BLOCK = 32
CMP_STRIDE = 16
SEL_BLOCK = 64
CMP_PER_SEL = SEL_BLOCK // CMP_STRIDE
N_SEL = 16
WINDOW = 512
NUM_BUCKETS = 32
MAX_DISTANCE = 128
FORCE_SCORE = 1e4
EPS = 1e-5
SCALE = HEAD_DIM ** -0.5
LOG2E = math.log2(math.e)
SCALE2 = SCALE * LOG2E
NEG = -1e30

LANES = 128
SUBLANES = 8
VMEM_LIMIT = 48 * 1024 * 1024

TQ = 256
TK = 256
ROW_CHUNK = 256
SB_ROW_CHUNK = 256
TQ_SB = 1024
PAGES_PER_STEP = 8
ROW_VECS = 8
SSM_TILE = 512
SSM_TILES = SSM_LANES // SSM_TILE
SSM_U_TILE = SSM_WIDTH // SSM_TILES


def _cparams(sem):
    return pltpu.CompilerParams(dimension_semantics=sem, vmem_limit_bytes=VMEM_LIMIT)


def _dot(a, b):
    return jnp.dot(a, b, preferred_element_type=F32)


def _dot_nt(a, b):
    return lax.dot_general(a, b, (((1,), (1,)), ((), ())), preferred_element_type=F32)


def _split_dot(x, w):
    hi = x.astype(BF16)
    lo = (x - hi.astype(F32)).astype(BF16)
    return _dot(hi, w) + _dot(lo, w)


def _sigmoid(x):
    return 1.0 / (1.0 + jnp.exp(-x))


def _lane_tile(x, n):
    return x if n == 1 else jnp.concatenate([x] * n, axis=1)


def _mm_kernel(x_ref, w_ref, *o_refs):
    acc = _dot(x_ref[...], w_ref[...])
    for o in o_refs:
        o[...] = acc.astype(o.dtype)


def _matmul(x, w, out_dtypes, name):
    m, k = x.shape
    n = w.shape[1]
    tm = min(m, 1024)
    tn = min(n, 512)
    outs = pl.pallas_call(
        _mm_kernel,
        grid=(m // tm, n // tn),
        in_specs=[pl.BlockSpec((tm, k), lambda i, j: (i, 0)),
                  pl.BlockSpec((k, tn), lambda i, j: (0, j))],
        out_specs=[pl.BlockSpec((tm, tn), lambda i, j: (i, j)) for _ in out_dtypes],
        out_shape=[jax.ShapeDtypeStruct((m, n), d) for d in out_dtypes],
        compiler_params=_cparams(("parallel", "parallel")),
        name=name,
    )(x, w)
    return outs


def _masked_softmax(s, valid):
    s = jnp.where(valid, s, NEG)
    m = jnp.max(s, axis=1, keepdims=True)
    e = jnp.where(valid, jnp.exp(s - m), 0.0)
    return e / jnp.maximum(jnp.sum(e, axis=1, keepdims=True), 1e-30)


def _masked_softmax2(s, valid):
    e = jnp.where(valid, jnp.exp2(s - jnp.max(s, axis=1, keepdims=True)), 0.0)
    return e * (1.0 / jnp.maximum(jnp.sum(e, axis=1, keepdims=True), 1e-30))


def _cmp_attend(qg, ck, cv, bias, rows):
    s = _dot_nt(qg, ck) * SCALE + bias
    p = _masked_softmax(s, bias > 0.5 * NEG)
    o = _dot(p.astype(BF16), cv)
    imp = p[0:rows] + p[rows:2 * rows] + p[2 * rows:3 * rows] + p[3 * rows:4 * rows]
    nc = ck.shape[0]
    nb = nc // CMP_PER_SEL
    grp = (lax.broadcasted_iota(jnp.int32, (nc, nb), 0) // CMP_PER_SEL
           == lax.broadcasted_iota(jnp.int32, (nc, nb), 1))
    return o, _split_dot(imp, jnp.where(grp, 1.0, 0.0).astype(BF16))


def _select_blocks(imp, pos, k_top, groups=1):
    rows, width = imp.shape
    nb = width // groups
    lane = lax.broadcasted_iota(jnp.int32, (rows, width), 1)
    blk = lane % nb
    cur = pos // SEL_BLOCK
    score = jnp.where(blk * SEL_BLOCK <= pos, imp, -1.0)
    score = jnp.where(blk == 0, FORCE_SCORE, score)
    score = jnp.where(blk == cur, FORCE_SCORE, score)
    score = jnp.where(blk == cur - 1, FORCE_SCORE, score)
    cnt = jnp.zeros((rows, width), F32)
    for i in range(nb):
        col = score[:, i:i + 1]
        for g in range(1, groups):
            col = jnp.where(lane >= g * nb, score[:, g * nb + i:g * nb + i + 1], col)
        ge = jnp.where(col >= score, 1.0, 0.0)
        gt = jnp.where(col > score, 1.0, 0.0)
        cnt = cnt + jnp.where(blk > i, ge, gt)
    return jnp.where(cnt < k_top, 1.0, 0.0)


def _flash_init(m_ref, l_ref, acc_ref):
    m_ref[...] = jnp.full(m_ref.shape, NEG, F32)
    l_ref[...] = jnp.zeros(l_ref.shape, F32)
    acc_ref[...] = jnp.zeros(acc_ref.shape, F32)


def _flash_tile(s, v, m_ref, l_ref, acc_ref):
    m, l, acc = _flash_update(s, v, m_ref[...], l_ref[...], acc_ref[...])
    m_ref[...] = m
    l_ref[...] = l
    acc_ref[...] = acc


def _flash_update(s, v, m_prev, l_prev, acc_prev):
    m_new = jnp.maximum(m_prev, jnp.max(s, axis=1, keepdims=True))
    alpha = jnp.exp2(m_prev - m_new)
    p = jnp.exp2(s - _lane_tile(m_new, s.shape[1] // LANES))
    return (m_new, alpha * l_prev + jnp.sum(p, axis=1, keepdims=True),
            alpha * acc_prev + _dot(p.astype(BF16), v))


def _page_vec(pg, c, page):
    return pg[0, 0, pl.ds(c, page, stride=ROW_VECS), :]


def _pad_keys(x):
    pad = jnp.zeros((LANES - x.shape[0], x.shape[1]), F32)
    return jnp.concatenate([x, pad], axis=0).astype(BF16)


def _stack_heads(q_ref_slice, width):
    return jnp.concatenate([q_ref_slice[:, HEAD_DIM * r:HEAD_DIM * (r + 1)] for r in range(NSA_GROUP)], axis=0)


def _log2_sigmoid(z2):
    return jnp.minimum(z2, 0.0) - jnp.log(1.0 + jnp.exp2(-jnp.abs(z2))) * LOG2E


def _strict_upper(n):
    return jnp.where(lax.broadcasted_iota(jnp.int32, (n, n), 0) > lax.broadcasted_iota(jnp.int32, (n, n), 1),
                     1.0, 0.0).astype(BF16)


def _prompt_cmp_kernel(q_ref, k0_ref, k1_ref, v0_ref, v1_ref, w_ref, bias_ref, o_ref, sel_ref, ck_ref, cv_ref,
                       *, tq, nc):
    qt = pl.program_id(1)

    @pl.when(qt == 0)
    def _():
        for g in range(NSA_KV_HEADS):
            for kv, src, dst in ((0, (k0_ref, k1_ref)[g], ck_ref), (1, (v0_ref, v1_ref)[g], cv_ref)):
                first = jnp.zeros((nc, HEAD_DIM), F32)
                second = jnp.zeros((nc, HEAD_DIM), F32)
                for j in range(CMP_STRIDE):
                    rows = src[0, pl.ds(j, nc, stride=CMP_STRIDE), :]
                    first = first + rows * w_ref[kv, j:j + 1, :]
                    second = second + rows * w_ref[kv, CMP_STRIDE + j:CMP_STRIDE + j + 1, :]
                dst[g] = (first + pltpu.roll(second, nc - 1, 0)).astype(BF16)

    nb = nc // CMP_PER_SEL
    pool = jnp.where(lax.broadcasted_iota(jnp.int32, (nc, nb), 0) // CMP_PER_SEL
                     == lax.broadcasted_iota(jnp.int32, (nc, nb), 1), 1.0, 0.0).astype(BF16)
    chunks = [(h, off) for h in range(NSA_HEADS) for off in range(0, tq, ROW_CHUNK)]

    def scores(h, off):
        bias = bias_ref[h, off:off + ROW_CHUNK, :]
        q = q_ref[0, off:off + ROW_CHUNK, h * HEAD_DIM:(h + 1) * HEAD_DIM]
        return _dot_nt(q, ck_ref[h // NSA_GROUP]) * SCALE2 + bias, bias

    imp = {(g, off): None for g in range(NSA_KV_HEADS) for off in range(0, tq, ROW_CHUNK)}
    nxt = scores(*chunks[0])
    for c, (h, off) in enumerate(chunks):
        s, bias = nxt
        if c + 1 < len(chunks):
            nxt = scores(*chunks[c + 1])
        p = _masked_softmax2(s, bias > 0.5 * NEG)
        o_ref[0, off:off + ROW_CHUNK, h * HEAD_DIM:(h + 1) * HEAD_DIM] = _dot(p.astype(BF16), cv_ref[h // NSA_GROUP])
        key = (h // NSA_GROUP, off)
        imp[key] = p if imp[key] is None else imp[key] + p
    imp = jnp.concatenate(
        [jnp.concatenate([_split_dot(imp[(g, off)], pool) for off in range(0, tq, ROW_CHUNK)], axis=0)
         for g in range(NSA_KV_HEADS)], axis=1)
    pos = qt * tq + lax.broadcasted_iota(jnp.int32, imp.shape, 0)
    sel = _select_blocks(imp, pos, N_SEL, NSA_KV_HEADS)
    for g in range(NSA_KV_HEADS):
        sel_ref[0, g] = sel[:, g * nb:(g + 1) * nb]


def _prompt_cmp(q, kvf, w_cmp, bias_cmp, b, t):
    nc = t // CMP_STRIDE
    nb = nc // CMP_PER_SEL
    q3 = q.reshape(b, t, q.shape[-1])
    kv3 = kvf.reshape(b, t, kvf.shape[-1])
    return pl.pallas_call(
        functools.partial(_prompt_cmp_kernel, tq=TQ, nc=nc),
        grid=(b, t // TQ),
        in_specs=[pl.BlockSpec((1, TQ, NSA_WIDTH), lambda i, j: (i, j, 0)),
                  *[pl.BlockSpec((1, t, HEAD_DIM), lambda i, j, c=c: (i, 0, c)) for c in range(4)],
                  pl.BlockSpec((2, CMP_BLOCK, HEAD_DIM), lambda i, j: (0, 0, 0)),
                  pl.BlockSpec((NSA_HEADS, TQ, nc), lambda i, j: (0, j, 0))],
        out_specs=[pl.BlockSpec((1, TQ, NSA_WIDTH), lambda i, j: (i, j, 0)),
                   pl.BlockSpec((1, NSA_KV_HEADS, TQ, nb), lambda i, j: (i, 0, j, 0))],
        out_shape=[jax.ShapeDtypeStruct((b, t, NSA_WIDTH), F32),
                   jax.ShapeDtypeStruct((b, NSA_KV_HEADS, t, nb), F32)],
        scratch_shapes=[pltpu.VMEM((NSA_KV_HEADS, nc, HEAD_DIM), BF16),
                        pltpu.VMEM((NSA_KV_HEADS, nc, HEAD_DIM), BF16)],
        compiler_params=_cparams(("parallel", "arbitrary")),
        name="prompt_cmp",
    )(q3, kv3, kv3, kv3, kv3, w_cmp, bias_cmp)


def _prompt_selwin_kernel(q_ref, ks_ref, vs_ref, kw_ref, vw_ref, sel_ref, tb_ref, osel_ref, owin_ref,
                          m_ref, l_ref, acc_ref, *, tq, tk):
    i = pl.program_id(2)
    selb = sel_ref[0, 0].astype(BF16)
    nb = selb.shape[1]
    blocks_per_tile = tk // SEL_BLOCK

    def tile(k_ref, v_ref, j, n, table, extra):
        start = pl.multiple_of(j * tk, tk)
        k = k_ref[0, pl.ds(start, n * tk), :]
        v = v_ref[0, pl.ds(start, n * tk), :]
        chunks = [(r, off) for r in range(NSA_GROUP) for off in range(0, tq, ROW_CHUNK)]

        def scores(r, off):
            s = _dot_nt(q_ref[0, off:off + ROW_CHUNK, r * HEAD_DIM:(r + 1) * HEAD_DIM], k) * SCALE2
            if table is not None:
                s = s + tb_ref[table, r, off:off + ROW_CHUNK, :]
            if extra is not None:
                s = s + extra[off:off + ROW_CHUNK]
            return s

        rows = [slice(r * tq + off, r * tq + off + ROW_CHUNK) for r, off in chunks]
        state = [(m_ref[rs], l_ref[rs], acc_ref[rs]) for rs in rows]
        s_next = scores(*chunks[0])
        for c in range(len(chunks)):
            s = s_next
            if c + 1 < len(chunks):
                s_next = scores(*chunks[c + 1])
            state[c] = _flash_update(s, v, *state[c])
        for rs, (m, l, acc) in zip(rows, state):
            m_ref[rs] = m
            l_ref[rs] = l
            acc_ref[rs] = acc

    def sel_tile(j, n, table):
        erow = lax.broadcasted_iota(jnp.int32, (nb, n * tk), 0)
        ecol = lax.broadcasted_iota(jnp.int32, (nb, n * tk), 1) // SEL_BLOCK
        expand = jnp.where(erow == j * blocks_per_tile + ecol, 1.0, 0.0).astype(BF16)
        tile(ks_ref, vs_ref, j, n, table, (1.0 - _dot(selb, expand)) * NEG)

    def win_tile(j, table):
        tile(kw_ref, vw_ref, j, 1, table, None)

    def finish(o_ref):
        for r in range(NSA_GROUP):
            rs = slice(r * tq, (r + 1) * tq)
            o_ref[0, :, r * HEAD_DIM:(r + 1) * HEAD_DIM] = acc_ref[rs] / l_ref[rs]

    _flash_init(m_ref, l_ref, acc_ref)

    n_far = jnp.maximum(i - 1, 0)
    quads = n_far // 4

    def far_body(jj, carry):
        sel_tile(4 * jj, 4, None)
        return carry

    lax.fori_loop(0, quads, far_body, 0)

    @pl.when(n_far % 4 >= 2)
    def _():
        sel_tile(4 * quads, 2, None)

    @pl.when(n_far % 2 == 1)
    def _():
        sel_tile(i - 2, 1, None)

    @pl.when(i >= 1)
    def _():
        sel_tile(i - 1, 1, 1)

    sel_tile(i, 1, 0)
    finish(osel_ref)

    _flash_init(m_ref, l_ref, acc_ref)
    win_tile(i, 0)

    @pl.when(i >= 1)
    def _():
        win_tile(i - 1, 1)

    @pl.when(i >= 2)
    def _():
        win_tile(i - 2, 2)

    finish(owin_ref)


def _prompt_selwin(q, kvb, sel, tables, b, t):
    nb = sel.shape[-1]
    q3 = q.reshape(b, t, q.shape[-1])
    kv3 = kvb.reshape(b, t, kvb.shape[-1])
    gw = NSA_GROUP * HEAD_DIM
    kcol = lambda base: pl.BlockSpec((1, t, HEAD_DIM), lambda i, g, j: (i, 0, base + g))
    return pl.pallas_call(
        functools.partial(_prompt_selwin_kernel, tq=TQ, tk=TK),
        grid=(b, NSA_KV_HEADS, t // TQ),
        in_specs=[pl.BlockSpec((1, TQ, gw), lambda i, g, j: (i, j, g)),
                  kcol(4), kcol(6), kcol(8), kcol(10),
                  pl.BlockSpec((1, 1, TQ, nb), lambda i, g, j: (i, g, j, 0)),
                  pl.BlockSpec((3, NSA_GROUP, TQ, TK), lambda i, g, j: (0, g, 0, 0))],
        out_specs=[pl.BlockSpec((1, TQ, gw), lambda i, g, j: (i, j, g)),
                   pl.BlockSpec((1, TQ, gw), lambda i, g, j: (i, j, g))],
        out_shape=[jax.ShapeDtypeStruct((b, t, NSA_WIDTH), F32),
                   jax.ShapeDtypeStruct((b, t, NSA_WIDTH), F32)],
        scratch_shapes=[pltpu.VMEM((NSA_GROUP * TQ, HEAD_DIM), F32),
                        pltpu.VMEM((NSA_GROUP * TQ, HEAD_DIM), F32),
                        pltpu.VMEM((NSA_GROUP * TQ, HEAD_DIM), F32)],
        compiler_params=_cparams(("parallel", "parallel", "parallel")),
        name="prompt_selwin",
    )(q3, kv3, kv3, kv3, kv3, sel, tables)


def _sb_weights(z, carry, upper, valid):
    seg = upper.shape[0]
    ls = _log2_sigmoid(z)
    l1m = ls - z
    if valid is not None:
        l1m = jnp.where(valid, l1m, 0.0)
    cums = []
    for sg in reversed(range(z.shape[1] // seg)):
        part = l1m[:, sg * seg:(sg + 1) * seg]
        cums.append(_split_dot(part, upper) + carry)
        carry = carry + jnp.sum(part, axis=1, keepdims=True)
    cum = cums[0] if len(cums) == 1 else jnp.concatenate(cums[::-1], axis=1)
    a = jnp.exp2(ls + cum)
    if valid is not None:
        a = jnp.where(valid, a, 0.0)
    return a, carry


def _prompt_sb_kernel(q_ref, k_ref, v_ref, o_ref, acc_ref, carry_ref, *, tq, tk):
    i = pl.program_id(2)
    upper = _strict_upper(tk)
    acc_ref[...] = jnp.zeros(acc_ref.shape, F32)
    carry_ref[...] = jnp.zeros(carry_ref.shape, F32)

    def run(block, diag):
        start = pl.multiple_of(block * tq, tq)
        k = k_ref[0, pl.ds(start, tq), :]
        v = v_ref[0, pl.ds(start, tq), :]
        chunks = [slice(off, off + SB_ROW_CHUNK) for off in range(0, tq, SB_ROW_CHUNK)]
        accs = [acc_ref[rs, :] for rs in chunks]
        zs = [_dot_nt(q_ref[0, rs, :], k) * SCALE2 for rs in chunks]
        valids = [None] * len(chunks)
        if diag:
            valids = [(lax.broadcasted_iota(jnp.int32, (SB_ROW_CHUNK, tq), 1)
                       < lax.broadcasted_iota(jnp.int32, (SB_ROW_CHUNK, tq), 0) + rs.start) for rs in chunks]
        weights = [_sb_weights(z, carry_ref[rs, 0:1], upper, valid) for z, rs, valid in zip(zs, chunks, valids)]
        outs = [_dot(a.astype(BF16), v) for a, _ in weights]
        for rs, acc, o, (_, carry) in zip(chunks, accs, outs, weights):
            acc_ref[rs, :] = acc + o
            carry_ref[rs, :] = jnp.broadcast_to(carry, (SB_ROW_CHUNK, HEAD_DIM))

    run(i, True)

    def body(jj, c):
        run(i - 1 - jj, False)
        return c

    lax.fori_loop(0, i, body, 0)
    o_ref[0] = acc_ref[...]


def _prompt_sb(q, kvb, b, t):
    q3 = q.reshape(b, t, q.shape[-1])
    kv3 = kvb.reshape(b, t, kvb.shape[-1])
    qbase = NSA_WIDTH // HEAD_DIM
    return pl.pallas_call(
        functools.partial(_prompt_sb_kernel, tq=TQ_SB, tk=TK),
        grid=(b, SB_HEADS, t // TQ_SB),
        in_specs=[pl.BlockSpec((1, TQ_SB, HEAD_DIM), lambda i, h, j: (i, j, qbase + h)),
                  pl.BlockSpec((1, t, HEAD_DIM), lambda i, h, j: (i, 0, 12 + h)),
                  pl.BlockSpec((1, t, HEAD_DIM), lambda i, h, j: (i, 0, 16 + h))],
        out_specs=pl.BlockSpec((1, TQ_SB, HEAD_DIM), lambda i, h, j: (i, j, h)),
        out_shape=jax.ShapeDtypeStruct((b, t, SB_WIDTH), F32),
        scratch_shapes=[pltpu.VMEM((TQ_SB, HEAD_DIM), F32), pltpu.VMEM((TQ_SB, HEAD_DIM), F32)],
        compiler_params=_cparams(("parallel", "parallel", "parallel")),
        name="prompt_sb",
    )(q3, kv3, kv3)


def _ssm_kernel(u_ref, h0_ref, wre_ref, wim_ref, cre_ref, cim_ref, d_ref, pre_ref, pim_ref,
                y_ref, hl_ref, hr_ref, hi_ref, *, t, tc):
    p8r = pre_ref[...]
    p8i = pim_ref[...]
    row = lax.broadcasted_iota(jnp.int32, (tc, SSM_TILE), 0) % SUBLANES

    def chunk(c, carry):
        cr, ci = carry
        start = pl.multiple_of(c * tc, tc)
        u = u_ref[0, pl.ds(start, tc), :]
        ub = u.astype(BF16)
        xr = _dot(ub, wre_ref[0])
        xi = _dot(ub, wim_ref[0])
        for sh in (1, 2, 4):
            ar = p8r[sh - 1:sh, :]
            ai = p8i[sh - 1:sh, :]
            sr = jnp.where(row >= sh, pltpu.roll(xr, sh, 0), 0.0)
            si = jnp.where(row >= sh, pltpu.roll(xi, sh, 0), 0.0)
            xr, xi = xr + ar * sr - ai * si, xi + ar * si + ai * sr
        for g in range(tc // SUBLANES):
            lo, hi = g * SUBLANES, (g + 1) * SUBLANES
            br = xr[lo:hi] + p8r * cr - p8i * ci
            bi = xi[lo:hi] + p8r * ci + p8i * cr
            hr_ref[lo:hi, :] = br
            hi_ref[lo:hi, :] = bi
            cr, ci = br[SUBLANES - 1:SUBLANES], bi[SUBLANES - 1:SUBLANES]
        y = (_dot(hr_ref[...].astype(BF16), cre_ref[0]) - _dot(hi_ref[...].astype(BF16), cim_ref[0])
             + d_ref[0] * u)
        y_ref[0, pl.ds(start, tc), :] = y
        return cr, ci

    cr, ci = lax.fori_loop(0, t // tc, chunk, (h0_ref[0, 0:1, :], h0_ref[0, 1:2, :]))
    hl_ref[0, 0:1, :] = cr
    hl_ref[0, 1:2, :] = ci


def _ssm(zc, h0, sw, b, t):
    wre, wim, cre, cim, dsk, pre, pim = sw
    tc = min(t, 256)
    u3 = zc.reshape(b, t, zc.shape[-1])
    ubase = (NSA_WIDTH + SB_WIDTH) // SSM_U_TILE
    wspec = lambda shp: pl.BlockSpec((1,) + shp, lambda i, k: (k, 0, 0))
    return pl.pallas_call(
        functools.partial(_ssm_kernel, t=t, tc=tc),
        grid=(b, SSM_TILES),
        in_specs=[pl.BlockSpec((1, t, SSM_U_TILE), lambda i, k: (i, 0, ubase + k)),
                  pl.BlockSpec((1, 2, SSM_TILE), lambda i, k: (i, 0, k)),
                  wspec((SSM_U_TILE, SSM_TILE)), wspec((SSM_U_TILE, SSM_TILE)),
                  wspec((SSM_TILE, SSM_U_TILE)), wspec((SSM_TILE, SSM_U_TILE)),
                  wspec((1, SSM_U_TILE)),
                  pl.BlockSpec((SUBLANES, SSM_TILE), lambda i, k: (0, k)),
                  pl.BlockSpec((SUBLANES, SSM_TILE), lambda i, k: (0, k))],
        out_specs=[pl.BlockSpec((1, t, SSM_U_TILE), lambda i, k: (i, 0, k)),
                   pl.BlockSpec((1, 2, SSM_TILE), lambda i, k: (i, 0, k))],
        out_shape=[jax.ShapeDtypeStruct((b, t, SSM_WIDTH), F32),
                   jax.ShapeDtypeStruct((b, 2, SSM_LANES), F32)],
        scratch_shapes=[pltpu.VMEM((tc, SSM_TILE), F32), pltpu.VMEM((tc, SSM_TILE), F32)],
        compiler_params=_cparams(("parallel", "parallel")),
        name="ssm_scan",
    )(u3, h0, wre, wim, cre, cim, dsk, pre, pim)


def _rms(h, gain):
    return h * lax.rsqrt(jnp.mean(h * h, axis=1, keepdims=True) + EPS) * gain


def _silu(z):
    return z * _sigmoid(z)


def _mix_kernel(ocmp_ref, osel_ref, owin_ref, gate_ref, ob_ref, yc_ref, za_ref, zb_ref, zc_ref, x_ref,
                wglu_ref, wout_ref, gain_ref, lng_ref, lnb_ref, y_ref, yb_ref, mixed_ref, *, alpha):
    gates = _sigmoid(gate_ref[...])
    for h in range(NSA_HEADS):
        sl = slice(h * HEAD_DIM, (h + 1) * HEAD_DIM)
        oa = (gates[:, 3 * h:3 * h + 1] * ocmp_ref[:, sl] + gates[:, 3 * h + 1:3 * h + 2] * osel_ref[:, sl]
              + gates[:, 3 * h + 2:3 * h + 3] * owin_ref[:, sl])
        mixed_ref[:, sl] = oa * _silu(za_ref[:, sl])
    mixed_ref[:, 0:NSA_WIDTH] = _rms(mixed_ref[:, 0:NSA_WIDTH], gain_ref[:, 0:NSA_WIDTH])
    b0, c0 = NSA_WIDTH, NSA_WIDTH + SB_WIDTH
    mixed_ref[:, b0:c0] = _rms(ob_ref[...] * _silu(zb_ref[...]), gain_ref[:, b0:c0])
    glu = _dot(yc_ref[...].astype(BF16), wglu_ref[...])
    yc = glu[:, 0:SSM_WIDTH] * _sigmoid(glu[:, SSM_WIDTH:2 * SSM_WIDTH])
    mixed_ref[:, c0:c0 + SSM_WIDTH] = _rms(yc * _silu(zc_ref[...]), gain_ref[:, c0:c0 + SSM_WIDTH])
    h = alpha * x_ref[...] + _dot(mixed_ref[...].astype(BF16), wout_ref[...])
    mu = jnp.mean(h, axis=1, keepdims=True)
    hc = h - mu
    var = jnp.mean(hc * hc, axis=1, keepdims=True)
    y = hc * lax.rsqrt(var + EPS) * lng_ref[...] + lnb_ref[...]
    y_ref[...] = y
    yb_ref[...] = y.astype(BF16)


def _mix_out(ocmp, osel, owin, gate, ob, yc, zc, x, wglu, wout, gain, lng, lnb, alpha):
    m, d = x.shape
    tm = min(m, 256)
    row = lambda w, c=0: pl.BlockSpec((tm, w), lambda i, c=c: (i, c))
    full = lambda a: pl.BlockSpec(a.shape, lambda i: (0,) * a.ndim)
    return pl.pallas_call(
        functools.partial(_mix_kernel, alpha=alpha),
        grid=(m // tm,),
        in_specs=[row(NSA_WIDTH), row(NSA_WIDTH), row(NSA_WIDTH), row(LANES), row(SB_WIDTH), row(SSM_WIDTH),
                  row(NSA_WIDTH, 0), row(SB_WIDTH, NSA_WIDTH // SB_WIDTH),
                  row(SSM_WIDTH, (NSA_WIDTH + SB_WIDTH + SSM_WIDTH) // SSM_WIDTH), row(d),
                  full(wglu), full(wout), full(gain), full(lng), full(lnb)],
        out_specs=[row(d), row(d)],
        out_shape=[jax.ShapeDtypeStruct((m, d), F32), jax.ShapeDtypeStruct((m, d), BF16)],
        scratch_shapes=[pltpu.VMEM((tm, d), F32)],
        compiler_params=_cparams(("parallel",)),
        name="mix_out",
    )(ocmp.reshape(m, -1), osel.reshape(m, -1), owin.reshape(m, -1), gate, ob.reshape(m, -1),
      yc.reshape(m, -1), zc, zc, zc, x, wglu, wout, gain, lng, lnb)


def _sample_compress_kernel(pt_ref, *refs):
    pages = refs[:PAGES_PER_STEP]
    w1_ref, w2_ref, first_ref, second_ref = refs[PAGES_PER_STEP:]
    page = pages[0].shape[2] // ROW_VECS
    per_page = page // CMP_STRIDE
    half = w1_ref.shape[1]
    pool = jnp.where(lax.broadcasted_iota(jnp.int32, (per_page, page), 1) // CMP_STRIDE
                     == lax.broadcasted_iota(jnp.int32, (per_page, page), 0), 1.0, 0.0).astype(BF16)
    for i, pg in enumerate(pages):
        rows = jnp.concatenate([_page_vec(pg, c, page) for c in range(2 * NSA_KV_HEADS)], axis=1)
        prod = jnp.concatenate([rows * w1_ref[...], rows * w2_ref[...]], axis=1)
        hi = prod.astype(BF16)
        lo = (prod - hi.astype(F32)).astype(BF16)
        sums = _dot(pool, hi) + _dot(pool, lo)
        first_ref[0, i * per_page:(i + 1) * per_page, :] = sums[:, 0:half]
        second_ref[0, i * per_page:(i + 1) * per_page, :] = sums[:, half:2 * half]


def _sample_compress(cache, layer, page_table, w_cmp):
    db, n_pages = page_table.shape
    page = cache.shape[2] // ROW_VECS
    per_page = page // CMP_STRIDE
    steps = n_pages // PAGES_PER_STEP
    nc = n_pages * per_page
    half = 4 * HEAD_DIM

    def pspec(i):
        return pl.BlockSpec((1, 1, page * ROW_VECS, HEAD_DIM),
                            lambda b, s, pt, i=i: (layer, pt[b, s * PAGES_PER_STEP + i], 0, 0))

    rows = PAGES_PER_STEP * per_page
    tiled = lambda w: jnp.concatenate([jnp.tile(w[kv], (per_page, 1)) for kv in (0, 0, 1, 1)], axis=1)
    w1, w2 = tiled(w_cmp[:, :CMP_STRIDE]), tiled(w_cmp[:, CMP_STRIDE:])
    return pl.pallas_call(
        _sample_compress_kernel,
        grid_spec=pltpu.PrefetchScalarGridSpec(
            num_scalar_prefetch=1,
            grid=(db, steps),
            in_specs=[pspec(i) for i in range(PAGES_PER_STEP)]
                     + [pl.BlockSpec((page, half), lambda b, s, pt: (0, 0)),
                        pl.BlockSpec((page, half), lambda b, s, pt: (0, 0))],
            out_specs=[pl.BlockSpec((1, rows, half), lambda b, s, pt: (b, s, 0)),
                       pl.BlockSpec((1, rows, half), lambda b, s, pt: (b, s, 0))]),
        out_shape=[jax.ShapeDtypeStruct((db, nc, half), F32), jax.ShapeDtypeStruct((db, nc, half), F32)],
        compiler_params=_cparams(("parallel", "parallel")),
        name="sample_compress",
    )(page_table, *([cache] * PAGES_PER_STEP), w1, w2)


def _sample_cmpwin_kernel(q_ref, first_ref, second_ref, bias_ref, win_ref, wbias_ref,
                          ocmp_ref, sel_ref, owin_ref, *, ts, nc, pos0):
    comp = first_ref[0] + pltpu.roll(second_ref[0], nc - 1, 0)
    nb = nc // CMP_PER_SEL
    pos = pos0 + lax.broadcasted_iota(jnp.int32, (ts, nb), 0)
    for g in range(NSA_KV_HEADS):
        qg = _stack_heads(q_ref[0, :, pl.ds(g * NSA_GROUP * HEAD_DIM, NSA_GROUP * HEAD_DIM)], HEAD_DIM)
        ck = comp[:, g * HEAD_DIM:(g + 1) * HEAD_DIM].astype(BF16)
        cv = comp[:, (2 + g) * HEAD_DIM:(3 + g) * HEAD_DIM].astype(BF16)
        bias = bias_ref[NSA_GROUP * g:NSA_GROUP * (g + 1)].reshape(NSA_GROUP * ts, nc)
        o, imp = _cmp_attend(qg, ck, cv, bias, ts)
        sel_ref[0, g] = _select_blocks(imp, pos, N_SEL - 1)
        kw = win_ref[0, :, g * HEAD_DIM:(g + 1) * HEAD_DIM].astype(BF16)
        vw = win_ref[0, :, (2 + g) * HEAD_DIM:(3 + g) * HEAD_DIM].astype(BF16)
        wb = wbias_ref[NSA_GROUP * g:NSA_GROUP * (g + 1)].reshape(NSA_GROUP * ts, kw.shape[0])
        sw = _dot_nt(qg, kw) * SCALE + wb
        ow = _dot(_masked_softmax(sw, wb > 0.5 * NEG).astype(BF16), vw)
        for r in range(NSA_GROUP):
            h = NSA_GROUP * g + r
            ocmp_ref[0, :, h * HEAD_DIM:(h + 1) * HEAD_DIM] = o[r * ts:(r + 1) * ts]
            owin_ref[0, :, h * HEAD_DIM:(h + 1) * HEAD_DIM] = ow[r * ts:(r + 1) * ts]


def _sample_cmpwin(q, first, second, bias_cmp, win_all, bias_win, db, ts, past_len):
    nc = first.shape[1]
    nb = nc // CMP_PER_SEL
    nw = win_all.shape[1]
    q3 = q.reshape(db, ts, q.shape[-1])
    return pl.pallas_call(
        functools.partial(_sample_cmpwin_kernel, ts=ts, nc=nc, pos0=past_len),
        grid=(db,),
        in_specs=[pl.BlockSpec((1, ts, NSA_WIDTH), lambda b: (b, 0, 0)),
                  pl.BlockSpec((1, nc, 4 * HEAD_DIM), lambda b: (b, 0, 0)),
                  pl.BlockSpec((1, nc, 4 * HEAD_DIM), lambda b: (b, 0, 0)),
                  pl.BlockSpec((NSA_HEADS, ts, nc), lambda b: (0, 0, 0)),
                  pl.BlockSpec((1, nw, 4 * HEAD_DIM), lambda b: (b, 0, 0)),
                  pl.BlockSpec((NSA_HEADS, ts, nw), lambda b: (0, 0, 0))],
        out_specs=[pl.BlockSpec((1, ts, NSA_WIDTH), lambda b: (b, 0, 0)),
                   pl.BlockSpec((1, NSA_KV_HEADS, ts, nb), lambda b: (b, 0, 0, 0)),
                   pl.BlockSpec((1, ts, NSA_WIDTH), lambda b: (b, 0, 0))],
        out_shape=[jax.ShapeDtypeStruct((db, ts, NSA_WIDTH), F32),
                   jax.ShapeDtypeStruct((db, NSA_KV_HEADS, ts, nb), F32),
                   jax.ShapeDtypeStruct((db, ts, NSA_WIDTH), F32)],
        compiler_params=_cparams(("parallel",)),
        name="sample_cmpwin",
    )(q3, first, second, bias_cmp, win_all, bias_win)


def _sample_sel_kernel(pt_ref, *refs, ts, page):
    pages = refs[:PAGES_PER_STEP]
    q_ref, sel_ref, new_ref, bias_ref, nbias_ref, o_ref, m_ref, l_ref, acc_ref = refs[PAGES_PER_STEP:]
    s_idx = pl.program_id(1)
    last = pl.num_programs(1) - 1
    rows = NSA_GROUP * ts
    nkeys = PAGES_PER_STEP * page
    blocks = nkeys // SEL_BLOCK

    @pl.when(s_idx == 0)
    def _():
        _flash_init(m_ref, l_ref, acc_ref)

    expand = jnp.where(lax.broadcasted_iota(jnp.int32, (blocks, nkeys), 0)
                       == lax.broadcasted_iota(jnp.int32, (blocks, nkeys), 1) // SEL_BLOCK, 1.0, 0.0).astype(BF16)
    for g in range(NSA_KV_HEADS):
        qg = _stack_heads(q_ref[0, :, pl.ds(g * NSA_GROUP * HEAD_DIM, NSA_GROUP * HEAD_DIM)], HEAD_DIM)
        k = jnp.concatenate([_page_vec(pg, 4 + g, page) for pg in pages], axis=0).astype(BF16)
        v = jnp.concatenate([_page_vec(pg, 6 + g, page) for pg in pages], axis=0).astype(BF16)
        bias = bias_ref[jnp.where(s_idx == last, 1, 0), NSA_GROUP * g:NSA_GROUP * (g + 1)].reshape(rows, nkeys)
        s = _dot_nt(qg, k) * SCALE2 + bias
        chosen = _dot(sel_ref[0, g, 0].astype(BF16), expand) > 0.5
        s = jnp.where(chosen[None], s.reshape(NSA_GROUP, ts, nkeys), NEG).reshape(rows, nkeys)
        sl = slice(g * rows, (g + 1) * rows)
        _flash_tile(s, v, m_ref.at[sl], l_ref.at[sl], acc_ref.at[sl])

        @pl.when(s_idx == last)
        def _():
            kn = _pad_keys(new_ref[0, :, g * HEAD_DIM:(g + 1) * HEAD_DIM])
            vn = _pad_keys(new_ref[0, :, (2 + g) * HEAD_DIM:(3 + g) * HEAD_DIM])
            nbias = nbias_ref[NSA_GROUP * g:NSA_GROUP * (g + 1)].reshape(rows, LANES)
            _flash_tile(_dot_nt(qg, kn) * SCALE2 + nbias, vn, m_ref.at[sl], l_ref.at[sl], acc_ref.at[sl])
            o = acc_ref[sl] / l_ref[sl]
            for r in range(NSA_GROUP):
                h = NSA_GROUP * g + r
                o_ref[0, :, h * HEAD_DIM:(h + 1) * HEAD_DIM] = o[r * ts:(r + 1) * ts]


def _sample_sel(cache, layer, page_table, q, sel_steps, kv_new, bias_steps, bias_new, db, ts):
    n_pages = page_table.shape[1]
    page = cache.shape[2] // ROW_VECS
    steps = n_pages // PAGES_PER_STEP
    half = 4 * HEAD_DIM
    nkeys = PAGES_PER_STEP * page
    blocks = nkeys // SEL_BLOCK
    q3 = q.reshape(db, ts, q.shape[-1])
    new3 = kv_new.reshape(db, ts, kv_new.shape[-1])

    def pspec(i):
        return pl.BlockSpec((1, 1, page * ROW_VECS, HEAD_DIM),
                            lambda b, s, pt, i=i: (layer, pt[b, s * PAGES_PER_STEP + i], 0, 0))

    return pl.pallas_call(
        functools.partial(_sample_sel_kernel, ts=ts, page=page),
        grid_spec=pltpu.PrefetchScalarGridSpec(
            num_scalar_prefetch=1,
            grid=(db, steps),
            in_specs=[pspec(i) for i in range(PAGES_PER_STEP)]
                     + [pl.BlockSpec((1, ts, NSA_WIDTH), lambda b, s, pt: (b, 0, 0)),
                        pl.BlockSpec((1, NSA_KV_HEADS, 1, ts, blocks), lambda b, s, pt: (b, 0, s, 0, 0)),
                        pl.BlockSpec((1, ts, half), lambda b, s, pt: (b, 0, 1)),
                        pl.BlockSpec((2, NSA_HEADS, ts, nkeys), lambda b, s, pt: (0, 0, 0, 0)),
                        pl.BlockSpec((NSA_HEADS, ts, LANES), lambda b, s, pt: (0, 0, 0))],
            out_specs=pl.BlockSpec((1, ts, NSA_WIDTH), lambda b, s, pt: (b, 0, 0)),
            scratch_shapes=[pltpu.VMEM((NSA_HEADS * ts, HEAD_DIM), F32),
                            pltpu.VMEM((NSA_HEADS * ts, HEAD_DIM), F32),
                            pltpu.VMEM((NSA_HEADS * ts, HEAD_DIM), F32)]),
        out_shape=jax.ShapeDtypeStruct((db, ts, NSA_WIDTH), F32),
        compiler_params=_cparams(("parallel", "arbitrary")),
        name="sample_sel",
    )(page_table, *([cache] * PAGES_PER_STEP), q3, sel_steps, new3, bias_steps, bias_new)


def _sample_sb_kernel(pt_ref, *refs, ts, page, seg):
    pages = refs[:PAGES_PER_STEP]
    q_ref, knew_ref, vnew_ref, o_ref, carry_ref, acc_ref = refs[PAGES_PER_STEP:]
    s_idx = pl.program_id(1)
    last = pl.num_programs(1) - 1
    upper = _strict_upper(seg)

    rows = SB_HEADS * ts
    heads = range(SB_HEADS)
    hs = lambda h: slice(h * HEAD_DIM, (h + 1) * HEAD_DIM)

    def attend(ks, vs, carry, up, valid):
        z = jnp.concatenate([_dot_nt(q_ref[0, :, hs(h)], ks[h]) for h in heads], axis=0) * SCALE2
        a, carry = _sb_weights(z, carry, up, valid)
        return jnp.concatenate([_dot(a[h * ts:(h + 1) * ts].astype(BF16), vs[h]) for h in heads], axis=0), carry

    @pl.when(s_idx == 0)
    def _():
        kn = [_pad_keys(knew_ref[0, :, hs(h)]) for h in heads]
        vn = [_pad_keys(vnew_ref[0, :, hs(h)]) for h in heads]
        valid = (lax.broadcasted_iota(jnp.int32, (rows, LANES), 1)
                 < lax.broadcasted_iota(jnp.int32, (rows, LANES), 0) % ts)
        o, carry = attend(kn, vn, jnp.zeros((rows, 1), F32), _strict_upper(LANES), valid)
        acc_ref[...] = o
        carry_ref[...] = jnp.broadcast_to(carry, (rows, HEAD_DIM))

    ks = [jnp.concatenate([_page_vec(pg, h, page) for pg in pages], axis=0).astype(BF16) for h in heads]
    vs = [jnp.concatenate([_page_vec(pg, SB_HEADS + h, page) for pg in pages], axis=0).astype(BF16) for h in heads]
    o, carry = attend(ks, vs, carry_ref[:, 0:1], upper, None)
    acc_ref[...] = acc_ref[...] + o
    carry_ref[...] = jnp.broadcast_to(carry, (rows, HEAD_DIM))

    @pl.when(s_idx == last)
    def _():
        for h in heads:
            o_ref[0, :, hs(h)] = acc_ref[h * ts:(h + 1) * ts, :]


def _sample_sb(cache, layer, page_table, q, kv_new, db, ts):
    n_pages = page_table.shape[1]
    page = cache.shape[2] // ROW_VECS
    steps = n_pages // PAGES_PER_STEP
    q3 = q.reshape(db, ts, q.shape[-1])
    new3 = kv_new.reshape(db, ts, kv_new.shape[-1])

    def pspec(i):
        return pl.BlockSpec((1, 1, page * ROW_VECS, HEAD_DIM),
                            lambda b, s, pt, i=i: (layer, pt[b, (steps - 1 - s) * PAGES_PER_STEP + i], 0, 0))

    return pl.pallas_call(
        functools.partial(_sample_sb_kernel, ts=ts, page=page, seg=TK),
        grid_spec=pltpu.PrefetchScalarGridSpec(
            num_scalar_prefetch=1,
            grid=(db, steps),
            in_specs=[pspec(i) for i in range(PAGES_PER_STEP)]
                     + [pl.BlockSpec((1, ts, SB_WIDTH), lambda b, s, pt: (b, 0, NSA_WIDTH // SB_WIDTH)),
                        pl.BlockSpec((1, ts, SB_WIDTH), lambda b, s, pt: (b, 0, 3)),
                        pl.BlockSpec((1, ts, SB_WIDTH), lambda b, s, pt: (b, 0, 4))],
            out_specs=pl.BlockSpec((1, ts, SB_WIDTH), lambda b, s, pt: (b, 0, 0)),
            scratch_shapes=[pltpu.VMEM((SB_HEADS * ts, HEAD_DIM), F32),
                            pltpu.VMEM((SB_HEADS * ts, HEAD_DIM), F32)]),
        out_shape=jax.ShapeDtypeStruct((db, ts, SB_WIDTH), F32),
        compiler_params=_cparams(("parallel", "arbitrary")),
        name="sample_sb",
    )(page_table, *([cache] * PAGES_PER_STEP), q3, new3, new3)


def _t5_bucket(dist):
    n = jnp.maximum(dist, 0)
    exact = NUM_BUCKETS // 2
    nf = jnp.maximum(n, 1).astype(F32)
    large = exact + (jnp.log(nf / exact) / math.log(MAX_DISTANCE / exact) * (NUM_BUCKETS - exact)).astype(jnp.int32)
    return jnp.where(n < exact, n, jnp.minimum(large, NUM_BUCKETS - 1))


def _bias(rel_bias, delta, valid):
    bucket = _t5_bucket(delta)[None]
    tab = rel_bias.astype(F32).T.reshape((rel_bias.shape[1], NUM_BUCKETS) + (1,) * delta.ndim)
    b = jnp.zeros((rel_bias.shape[1],) + delta.shape, F32)
    for k in range(NUM_BUCKETS):
        b = jnp.where(bucket == k, tab[:, k], b)
    return jnp.where(valid[None], b, NEG)


def _prompt_tables(rel_bias, t):
    tt = jnp.arange(TQ)[:, None]
    ss = jnp.arange(TK)[None, :]
    d = tt - ss
    assert 2 * TK - TQ >= MAX_DISTANCE
    far = _bias(rel_bias, d + 2 * TK, d > -2 * TK)
    tables = jnp.stack([
        _bias(rel_bias, d, d >= 0) - far,
        _bias(rel_bias, d + TK, d > -TK) - far,
        jnp.where(d + 2 * TK < WINDOW, 0.0, NEG) + 0.0 * far,
    ])
    nc = t // CMP_STRIDE
    dc = jnp.arange(t)[:, None] - (jnp.arange(nc)[None, :] * CMP_STRIDE + CMP_BLOCK - 1)
    return tables * LOG2E, _bias(rel_bias, dc, dc >= 0) * LOG2E


def _sample_tables(rel_bias, ts, past_len, page, nw, nw_pad):
    qpos = past_len + jnp.arange(ts)[:, None]
    nc = past_len // CMP_STRIDE
    dc = qpos - (jnp.arange(nc)[None, :] * CMP_STRIDE + CMP_BLOCK - 1)
    bias_cmp = _bias(rel_bias, dc, dc >= 0)
    widx = jnp.arange(nw_pad)[None, :]
    dw = qpos - (past_len - nw + widx)
    bias_win = _bias(rel_bias, dw, (dw >= 0) & (dw < WINDOW) & (widx < nw + ts))
    nkeys = PAGES_PER_STEP * page
    d_last = qpos - (past_len - nkeys + jnp.arange(nkeys)[None, :])
    d_far = d_last + nkeys
    bias_steps = jnp.stack([_bias(rel_bias, d_far, d_far >= 0), _bias(rel_bias, d_last, d_last >= 0)])
    nidx = jnp.arange(LANES)[None, :]
    dn = jnp.arange(ts)[:, None] - nidx
    return bias_cmp, bias_win, bias_steps * LOG2E, _bias(rel_bias, dn, (dn >= 0) & (nidx < ts)) * LOG2E


def _ssm_weights(lam_re, lam_im, log_dt, b_re, b_im, c_re, c_im, d_skip):
    lam_re, lam_im = lam_re.astype(F32), lam_im.astype(F32)
    dt = jnp.exp(log_dt.astype(F32))[:, None]
    mag = jnp.exp(lam_re * dt)
    a_re, a_im = mag * jnp.cos(lam_im * dt), mag * jnp.sin(lam_im * dt)
    den = lam_re * lam_re + lam_im * lam_im
    f_re = ((a_re - 1.0) * lam_re + a_im * lam_im) / den
    f_im = (a_im * lam_re - (a_re - 1.0) * lam_im) / den
    b_re, b_im = b_re.astype(F32), b_im.astype(F32)
    bb_re = f_re[..., None] * b_re - f_im[..., None] * b_im
    bb_im = f_re[..., None] * b_im + f_im[..., None] * b_re
    gpt = SSM_GROUPS // SSM_TILES
    eye = jnp.eye(gpt, dtype=F32)

    def in_mat(bb):
        return jnp.einsum('kgpn,gh->kgnhp', bb.reshape(SSM_TILES, gpt, SSM_STATE, SSM_CH), eye).reshape(
            SSM_TILES, SSM_U_TILE, SSM_TILE).astype(BF16)

    def out_mat(c):
        return jnp.einsum('kgnp,gh->khpgn', c.astype(F32).reshape(SSM_TILES, gpt, SSM_CH, SSM_STATE), eye).reshape(
            SSM_TILES, SSM_TILE, SSM_U_TILE).astype(BF16)

    pr, pi = a_re.reshape(1, SSM_LANES), a_im.reshape(1, SSM_LANES)
    while pr.shape[0] < SUBLANES:
        tr, ti = pr[-1:], pi[-1:]
        pr, pi = (jnp.concatenate([pr, pr * tr - pi * ti]), jnp.concatenate([pi, pr * ti + pi * tr]))
    return (in_mat(bb_re), in_mat(bb_im), out_mat(c_re), out_mat(c_im),
            d_skip.astype(F32).reshape(SSM_TILES, 1, SSM_U_TILE), pr, pi)


def _split_w_in(w_in):
    sizes = (NSA_WIDTH, 6 * NSA_KV_HEADS * HEAD_DIM, 3 * NSA_HEADS, NSA_WIDTH,
             SB_WIDTH, SB_WIDTH, SB_WIDTH, SB_WIDTH, SSM_WIDTH, SSM_WIDTH)
    offs = [0]
    for s in sizes:
        offs.append(offs[-1] + s)
    q_a, kv_a, g_a, z_a, q_b, k_b, v_b, z_b, u_c, z_c = (w_in[..., offs[i]:offs[i + 1]] for i in range(10))
    cat = lambda xs: jnp.concatenate(xs, axis=-1).astype(BF16)
    pad = jnp.zeros(g_a.shape[:-1] + (LANES - g_a.shape[-1],), g_a.dtype)
    return cat([q_a, q_b]), cat([kv_a, k_b, v_b]), cat([z_a, z_b, u_c, z_c]), cat([g_a, pad])


KV_TILE = 4 * HEAD_DIM


def _kv_proj_kernel(*refs, tm):
    x_ref, w_ref = refs[:2]
    kvf_ref, kvb_ref, nsa_ref, sb_ref = refs[-4:]
    j = pl.program_id(1)
    acc = _dot(x_ref[...], w_ref[...])
    kvf_ref[...] = acc
    kvb_ref[...] = acc.astype(BF16)

    def scatter(dst, base):
        for c in range(KV_TILE // HEAD_DIM):
            dst[0, pl.ds(base + c, tm, stride=ROW_VECS), :] = acc[:, c * HEAD_DIM:(c + 1) * HEAD_DIM]

    for step, dst, base in ((0, nsa_ref, 0), (1, nsa_ref, 4), (3, sb_ref, 0), (4, sb_ref, 4)):
        pl.when(j == step)(functools.partial(scatter, dst, base))


def _kv_proj(x, w, layer, depth, stacked):
    m, k = x.shape
    n = w.shape[1]
    tm = min(m, 1024)
    assert n == 5 * KV_TILE
    rows = pl.BlockSpec((1, tm * ROW_VECS, HEAD_DIM), lambda i, j: (layer, i, 0))
    stack_shape = jax.ShapeDtypeStruct((depth, m * ROW_VECS, HEAD_DIM), F32)
    return pl.pallas_call(
        functools.partial(_kv_proj_kernel, tm=tm),
        grid=(m // tm, n // KV_TILE),
        in_specs=[pl.BlockSpec((tm, k), lambda i, j: (i, 0)), pl.BlockSpec((k, KV_TILE), lambda i, j: (0, j)),
                  pl.BlockSpec(memory_space=pl.ANY), pl.BlockSpec(memory_space=pl.ANY)],
        out_specs=[pl.BlockSpec((tm, KV_TILE), lambda i, j: (i, j)), pl.BlockSpec((tm, KV_TILE), lambda i, j: (i, j)),
                   rows, rows],
        out_shape=[jax.ShapeDtypeStruct((m, n), F32), jax.ShapeDtypeStruct((m, n), BF16), stack_shape, stack_shape],
        input_output_aliases={2: 2, 3: 3},
        compiler_params=_cparams(("parallel", "arbitrary")),
        name="in_proj_kv",
    )(x, w, *stacked)


def _in_proj(xb, wl, layer, depth, stacked):
    wa, wb, wc, wd = wl
    (q,) = _matmul(xb, wa, (BF16,), "in_proj_q")
    kvf, kvb, nsa, sb = _kv_proj(xb, wb, layer, depth, stacked)
    (zc,) = _matmul(xb, wc, (F32,), "in_proj_z")
    (gate,) = _matmul(xb, wd, (F32,), "in_proj_gate")
    return q, kvf, kvb, zc, gate, (nsa, sb)


def kernel(x_prompt, x_sample, cache_nsa, cache_sb, state_win, state_ssm, page_table, rel_bias, w_in, w_cmp,
           ssm_lam_re, ssm_lam_im, ssm_log_dt, ssm_b_re, ssm_b_im, ssm_c_re, ssm_c_im, ssm_d, w_glu, mix_gain,
           w_out, ln_g, ln_b):
    depth = w_in.shape[0]
    b, t, d = x_prompt.shape
    db, ts, _ = x_sample.shape
    n_phys, page = cache_nsa.shape[1], cache_nsa.shape[2]
    n_pages = page_table.shape[1]
    past_len = n_pages * page
    nw = state_win.shape[2]
    alpha = (2 * depth) ** 0.25
    assert t % TQ == 0 and t % TQ_SB == 0 and TQ_SB % TK == 0 and WINDOW == 2 * TK and TQ == TK and n_pages % PAGES_PER_STEP == 0
    assert past_len % SEL_BLOCK == 0 and ts % SUBLANES == 0 and ts <= SEL_BLOCK and nw == WINDOW

    w_groups = _split_w_in(w_in)
    w_glu_b, w_out_b = w_glu.astype(BF16), w_out.astype(BF16)
    ssm_w = jax.vmap(_ssm_weights)(ssm_lam_re, ssm_lam_im, ssm_log_dt, ssm_b_re, ssm_b_im, ssm_c_re, ssm_c_im, ssm_d)
    tables_p, bias_cmp_p = _prompt_tables(rel_bias, t)
    nw_pad = -(-(nw + ts) // LANES) * LANES
    bias_cmp_s, bias_win_s, bias_steps_s, bias_new_s = _sample_tables(rel_bias, ts, past_len, page, nw, nw_pad)
    cache_nsa_r = cache_nsa.reshape(depth, n_phys, page * ROW_VECS, HEAD_DIM)
    cache_sb_r = cache_sb.reshape(depth, n_phys, page * ROW_VECS, HEAD_DIM)
    w_cmp = w_cmp.astype(F32)
    page_table = page_table.astype(jnp.int32)

    xp, xs = x_prompt.reshape(b * t, d), x_sample.reshape(db * ts, d)
    xpb, xsb = xp.astype(BF16), xs.astype(BF16)
    outs = [[] for _ in range(8)]
    rows_p = (jnp.zeros((depth, b * t * ROW_VECS, HEAD_DIM), F32),) * 2
    rows_s = (jnp.zeros((depth, db * ts * ROW_VECS, HEAD_DIM), F32),) * 2
    for l in range(depth):
        wl = tuple(w[l] for w in w_groups)
        sw = tuple(w[l] for w in ssm_w)
        gain, lng, lnb = mix_gain[l].reshape(1, d), ln_g[l].reshape(1, d), ln_b[l].reshape(1, d)

        q, kvf, kvb, zc, gate, rows_p = _in_proj(xpb, wl, l, depth, rows_p)
        o_cmp, sel = _prompt_cmp(q, kvf, w_cmp[l], bias_cmp_p, b, t)
        o_sel, o_win = _prompt_selwin(q, kvb, sel, tables_p, b, t)
        o_b = _prompt_sb(q, kvb, b, t)
        y_c, h_p = _ssm(zc, jnp.zeros((b, 2, SSM_LANES), F32), sw, b, t)
        xp, xpb = _mix_out(o_cmp, o_sel, o_win, gate, o_b, y_c, zc, xp, w_glu_b[l], w_out_b[l], gain, lng, lnb,
                           alpha)
        kv3 = kvf.reshape(b, t, -1)
        outs[4].append(kv3[:, max(t - WINDOW, 0):, 1024:1536].reshape(b, -1, 2, NSA_KV_HEADS, HEAD_DIM))
        outs[6].append(h_p.reshape(b, 2, SSM_GROUPS, SSM_STATE))

        q, kvf, kvb, zc, gate, rows_s = _in_proj(xsb, wl, l, depth, rows_s)
        kv3 = kvf.reshape(db, ts, -1)
        win_all = jnp.concatenate([state_win[l].reshape(db, nw, -1), kv3[:, :, 1024:1536],
                                   jnp.zeros((db, nw_pad - nw - ts, 4 * HEAD_DIM), F32)], axis=1)
        first, second = _sample_compress(cache_nsa_r, l, page_table, w_cmp[l])
        o_cmp, sel, o_win = _sample_cmpwin(q, first, second, bias_cmp_s, win_all, bias_win_s, db, ts, past_len)
        steps = n_pages // PAGES_PER_STEP
        sel_steps = jnp.moveaxis(sel.reshape(db, NSA_KV_HEADS, ts, steps, -1), 3, 2)
        o_sel = _sample_sel(cache_nsa_r, l, page_table, q, sel_steps, kvf, bias_steps_s, bias_new_s, db, ts)
        o_b = _sample_sb(cache_sb_r, l, page_table, q, kvf, db, ts)
        y_c, h_s = _ssm(zc, state_ssm[l].reshape(db, 2, SSM_LANES).astype(F32), sw, db, ts)
        xs, xsb = _mix_out(o_cmp, o_sel, o_win, gate, o_b, y_c, zc, xs, w_glu_b[l], w_out_b[l], gain, lng, lnb,
                           alpha)
        outs[5].append(win_all[:, ts:ts + nw].reshape(db, nw, 2, NSA_KV_HEADS, HEAD_DIM))
        outs[7].append(h_s.reshape(db, 2, SSM_GROUPS, SSM_STATE))

    win_p, win_s, ssm_p, ssm_s = (jnp.stack(outs[i]) for i in (4, 5, 6, 7))
    return (xp.reshape(b, t, d), xs.reshape(db, ts, d),
            rows_p[0].reshape(depth, b, t, 4, NSA_KV_HEADS, HEAD_DIM),
            rows_s[0].reshape(depth, db, ts, 4, NSA_KV_HEADS, HEAD_DIM),
            rows_p[1].reshape(depth, b, t, 2, SB_HEADS, HEAD_DIM),
            rows_s[1].reshape(depth, db, ts, 2, SB_HEADS, HEAD_DIM),
            win_p, win_s, ssm_p, ssm_s)
```

```python
import functools
import math

import jax
import jax.numpy as jnp
from jax import lax
from jax.experimental import pallas as pl
from jax.experimental.pallas import tpu as pltpu

F32 = jnp.float32
BF16 = jnp.bfloat16

HEAD_DIM = 128
NSA_KV_HEADS = 2
NSA_GROUP = 4
NSA_HEADS = NSA_KV_HEADS * NSA_GROUP
NSA_WIDTH = NSA_HEADS * HEAD_DIM
SB_HEADS = 4
SB_WIDTH = SB_HEADS * HEAD_DIM
SSM_CH = 16
SSM_GROUPS = 32
SSM_STATE = 64
SSM_WIDTH = SSM_CH * SSM_GROUPS
SSM_LANES = SSM_GROUPS * SSM_STATE
CMP_BLOCK = 32
CMP_STRIDE = 16
SEL_BLOCK = 64
CMP_PER_SEL = SEL_BLOCK // CMP_STRIDE
N_SEL = 16
WINDOW = 512
NUM_BUCKETS = 32
MAX_DISTANCE = 128
FORCE_SCORE = 1e4
EPS = 1e-5
SCALE = HEAD_DIM ** -0.5
LOG2E = math.log2(math.e)
SCALE2 = SCALE * LOG2E
NEG = -1e30

LANES = 128
SUBLANES = 8
VMEM_LIMIT = 48 * 1024 * 1024

TQ = 256
TK = 256
ROW_CHUNK = 256
SB_ROW_CHUNK = 256
TQ_SB = 1024
PAGES_PER_STEP = 16
ROW_VECS = 8
SSM_TILE = 512
SSM_TILES = SSM_LANES // SSM_TILE
SSM_U_TILE = SSM_WIDTH // SSM_TILES


def _cparams(sem):
    return pltpu.CompilerParams(dimension_semantics=sem, vmem_limit_bytes=VMEM_LIMIT)


def _dot(a, b):
    return jnp.dot(a, b, preferred_element_type=F32)


def _dot_nt(a, b):
    return lax.dot_general(a, b, (((1,), (1,)), ((), ())), preferred_element_type=F32)


def _split_dot(x, w):
    hi = x.astype(BF16)
    lo = (x - hi.astype(F32)).astype(BF16)
    return _dot(hi, w) + _dot(lo, w)


def _sigmoid(x):
    return 1.0 / (1.0 + jnp.exp(-x))


def _lane_tile(x, n):
    return x if n == 1 else jnp.concatenate([x] * n, axis=1)


def _mm_kernel(x_ref, w_ref, *o_refs):
    acc = _dot(x_ref[...], w_ref[...])
    for o in o_refs:
        o[...] = acc.astype(o.dtype)


def _matmul(x, w, out_dtypes, name):
    m, k = x.shape
    n = w.shape[1]
    tm = min(m, 1024)
    tn = min(n, 512)
    outs = pl.pallas_call(
        _mm_kernel,
        grid=(m // tm, n // tn),
        in_specs=[pl.BlockSpec((tm, k), lambda i, j: (i, 0)),
                  pl.BlockSpec((k, tn), lambda i, j: (0, j))],
        out_specs=[pl.BlockSpec((tm, tn), lambda i, j: (i, j)) for _ in out_dtypes],
        out_shape=[jax.ShapeDtypeStruct((m, n), d) for d in out_dtypes],
        compiler_params=_cparams(("parallel", "parallel")),
        name=name,
    )(x, w)
    return outs


def _masked_softmax(s, valid):
    s = jnp.where(valid, s, NEG)
    m = jnp.max(s, axis=1, keepdims=True)
    e = jnp.where(valid, jnp.exp(s - m), 0.0)
    return e / jnp.maximum(jnp.sum(e, axis=1, keepdims=True), 1e-30)


def _masked_softmax2(s, valid):
    e = jnp.where(valid, jnp.exp2(s - jnp.max(s, axis=1, keepdims=True)), 0.0)
    return e * (1.0 / jnp.maximum(jnp.sum(e, axis=1, keepdims=True), 1e-30))


def _cmp_attend(qg, ck, cv, bias, rows):
    s = _dot_nt(qg, ck) * SCALE + bias
    p = _masked_softmax(s, bias > 0.5 * NEG)
    o = _dot(p.astype(BF16), cv)
    imp = p[0:rows] + p[rows:2 * rows] + p[2 * rows:3 * rows] + p[3 * rows:4 * rows]
    nc = ck.shape[0]
    nb = nc // CMP_PER_SEL
    grp = (lax.broadcasted_iota(jnp.int32, (nc, nb), 0) // CMP_PER_SEL
           == lax.broadcasted_iota(jnp.int32, (nc, nb), 1))
    return o, _split_dot(imp, jnp.where(grp, 1.0, 0.0).astype(BF16))


def _select_blocks(imp, pos, k_top, groups=1):
    rows, width = imp.shape
    nb = width // groups
    lane = lax.broadcasted_iota(jnp.int32, (rows, width), 1)
    blk = lane % nb
    cur = pos // SEL_BLOCK
    score = jnp.where(blk * SEL_BLOCK <= pos, imp, -1.0)
    score = jnp.where(blk == 0, FORCE_SCORE, score)
    score = jnp.where(blk == cur, FORCE_SCORE, score)
    score = jnp.where(blk == cur - 1, FORCE_SCORE, score)
    cnt = jnp.zeros((rows, width), F32)
    for i in range(nb):
        col = score[:, i:i + 1]
        for g in range(1, groups):
            col = jnp.where(lane >= g * nb, score[:, g * nb + i:g * nb + i + 1], col)
        ge = jnp.where(col >= score, 1.0, 0.0)
        gt = jnp.where(col > score, 1.0, 0.0)
        cnt = cnt + jnp.where(blk > i, ge, gt)
    return jnp.where(cnt < k_top, 1.0, 0.0)


def _flash_init(m_ref, l_ref, acc_ref):
    m_ref[...] = jnp.full(m_ref.shape, NEG, F32)
    l_ref[...] = jnp.zeros(l_ref.shape, F32)
    acc_ref[...] = jnp.zeros(acc_ref.shape, F32)


def _flash_tile(s, v, m_ref, l_ref, acc_ref):
    m, l, acc = _flash_update(s, v, m_ref[...], l_ref[...], acc_ref[...])
    m_ref[...] = m
    l_ref[...] = l
    acc_ref[...] = acc


def _flash_update(s, v, m_prev, l_prev, acc_prev):
    m_new = jnp.maximum(m_prev, jnp.max(s, axis=1, keepdims=True))
    alpha = jnp.exp2(m_prev - m_new)
    p = jnp.exp2(s - _lane_tile(m_new, s.shape[1] // LANES))
    return (m_new, alpha * l_prev + jnp.sum(p, axis=1, keepdims=True),
            alpha * acc_prev + _dot(p.astype(BF16), v))


def _page_vec(pg, c, page):
    return pg[0, 0, pl.ds(c, page, stride=ROW_VECS), :]


def _pad_keys(x):
    pad = jnp.zeros((LANES - x.shape[0], x.shape[1]), F32)
    return jnp.concatenate([x, pad], axis=0).astype(BF16)


def _stack_heads(q_ref_slice, width):
    return jnp.concatenate([q_ref_slice[:, HEAD_DIM * r:HEAD_DIM * (r + 1)] for r in range(NSA_GROUP)], axis=0)


def _log2_sigmoid(z2):
    return jnp.minimum(z2, 0.0) - jnp.log(1.0 + jnp.exp2(-jnp.abs(z2))) * LOG2E


def _strict_upper(n):
    return jnp.where(lax.broadcasted_iota(jnp.int32, (n, n), 0) > lax.broadcasted_iota(jnp.int32, (n, n), 1),
                     1.0, 0.0).astype(BF16)


def _prompt_cmp_kernel(q_ref, k0_ref, k1_ref, v0_ref, v1_ref, w_ref, bias_ref, o_ref, sel_ref, ck_ref, cv_ref,
                       *, tq, nc):
    qt = pl.program_id(1)

    @pl.when(qt == 0)
    def _():
        for g in range(NSA_KV_HEADS):
            for kv, src, dst in ((0, (k0_ref, k1_ref)[g], ck_ref), (1, (v0_ref, v1_ref)[g], cv_ref)):
                first = jnp.zeros((nc, HEAD_DIM), F32)
                second = jnp.zeros((nc, HEAD_DIM), F32)
                for j in range(CMP_STRIDE):
                    rows = src[0, pl.ds(j, nc, stride=CMP_STRIDE), :]
                    first = first + rows * w_ref[kv, j:j + 1, :]
                    second = second + rows * w_ref[kv, CMP_STRIDE + j:CMP_STRIDE + j + 1, :]
                dst[g] = (first + pltpu.roll(second, nc - 1, 0)).astype(BF16)

    nb = nc // CMP_PER_SEL
    pool = jnp.where(lax.broadcasted_iota(jnp.int32, (nc, nb), 0) // CMP_PER_SEL
                     == lax.broadcasted_iota(jnp.int32, (nc, nb), 1), 1.0, 0.0).astype(BF16)
    chunks = [(h, off) for h in range(NSA_HEADS) for off in range(0, tq, ROW_CHUNK)]

    def scores(h, off):
        bias = bias_ref[h, off:off + ROW_CHUNK, :]
        q = q_ref[0, off:off + ROW_CHUNK, h * HEAD_DIM:(h + 1) * HEAD_DIM]
        return _dot_nt(q, ck_ref[h // NSA_GROUP]) * SCALE2 + bias, bias

    imp = {(g, off): None for g in range(NSA_KV_HEADS) for off in range(0, tq, ROW_CHUNK)}
    nxt = scores(*chunks[0])
    for c, (h, off) in enumerate(chunks):
        s, bias = nxt
        if c + 1 < len(chunks):
            nxt = scores(*chunks[c + 1])
        p = _masked_softmax2(s, bias > 0.5 * NEG)
        o_ref[0, off:off + ROW_CHUNK, h * HEAD_DIM:(h + 1) * HEAD_DIM] = _dot(p.astype(BF16), cv_ref[h // NSA_GROUP])
        key = (h // NSA_GROUP, off)
        imp[key] = p if imp[key] is None else imp[key] + p
    imp = jnp.concatenate(
        [jnp.concatenate([_split_dot(imp[(g, off)], pool) for off in range(0, tq, ROW_CHUNK)], axis=0)
         for g in range(NSA_KV_HEADS)], axis=1)
    pos = qt * tq + lax.broadcasted_iota(jnp.int32, imp.shape, 0)
    sel = _select_blocks(imp, pos, N_SEL, NSA_KV_HEADS)
    for g in range(NSA_KV_HEADS):
        sel_ref[0, g] = sel[:, g * nb:(g + 1) * nb]


def _prompt_cmp(q, kvf, w_cmp, bias_cmp, b, t):
    nc = t // CMP_STRIDE
    nb = nc // CMP_PER_SEL
    q3 = q.reshape(b, t, q.shape[-1])
    kv3 = kvf.reshape(b, t, kvf.shape[-1])
    return pl.pallas_call(
        functools.partial(_prompt_cmp_kernel, tq=TQ, nc=nc),
        grid=(b, t // TQ),
        in_specs=[pl.BlockSpec((1, TQ, NSA_WIDTH), lambda i, j: (i, j, 0)),
                  *[pl.BlockSpec((1, t, HEAD_DIM), lambda i, j, c=c: (i, 0, c)) for c in range(4)],
                  pl.BlockSpec((2, CMP_BLOCK, HEAD_DIM), lambda i, j: (0, 0, 0)),
                  pl.BlockSpec((NSA_HEADS, TQ, nc), lambda i, j: (0, j, 0))],
        out_specs=[pl.BlockSpec((1, TQ, NSA_WIDTH), lambda i, j: (i, j, 0)),
                   pl.BlockSpec((1, NSA_KV_HEADS, TQ, nb), lambda i, j: (i, 0, j, 0))],
        out_shape=[jax.ShapeDtypeStruct((b, t, NSA_WIDTH), F32),
                   jax.ShapeDtypeStruct((b, NSA_KV_HEADS, t, nb), F32)],
        scratch_shapes=[pltpu.VMEM((NSA_KV_HEADS, nc, HEAD_DIM), BF16),
                        pltpu.VMEM((NSA_KV_HEADS, nc, HEAD_DIM), BF16)],
        compiler_params=_cparams(("parallel", "arbitrary")),
        name="prompt_cmp",
    )(q3, kv3, kv3, kv3, kv3, w_cmp, bias_cmp)


def _prompt_selwin_kernel(q_ref, ks_ref, vs_ref, kw_ref, vw_ref, sel_ref, tb_ref, osel_ref, owin_ref,
                          m_ref, l_ref, acc_ref, *, tq, tk):
    i = pl.program_id(2)
    selb = sel_ref[0, 0].astype(BF16)
    nb = selb.shape[1]
    blocks_per_tile = tk // SEL_BLOCK

    def tile(k_ref, v_ref, j, n, table, extra):
        start = pl.multiple_of(j * tk, tk)
        k = k_ref[0, pl.ds(start, n * tk), :]
        v = v_ref[0, pl.ds(start, n * tk), :]
        chunks = [(r, off) for r in range(NSA_GROUP) for off in range(0, tq, ROW_CHUNK)]

        def scores(r, off):
            s = _dot_nt(q_ref[0, off:off + ROW_CHUNK, r * HEAD_DIM:(r + 1) * HEAD_DIM], k) * SCALE2
            if table is not None:
                s = s + tb_ref[table, r, off:off + ROW_CHUNK, :]
            if extra is not None:
                s = s + extra[off:off + ROW_CHUNK]
            return s

        rows = [slice(r * tq + off, r * tq + off + ROW_CHUNK) for r, off in chunks]
        state = [(m_ref[rs], l_ref[rs], acc_ref[rs]) for rs in rows]
        s_next = scores(*chunks[0])
        for c in range(len(chunks)):
            s = s_next
            if c + 1 < len(chunks):
                s_next = scores(*chunks[c + 1])
            state[c] = _flash_update(s, v, *state[c])
        for rs, (m, l, acc) in zip(rows, state):
            m_ref[rs] = m
            l_ref[rs] = l
            acc_ref[rs] = acc

    def sel_tile(j, n, table):
        erow = lax.broadcasted_iota(jnp.int32, (nb, n * tk), 0)
        ecol = lax.broadcasted_iota(jnp.int32, (nb, n * tk), 1) // SEL_BLOCK
        expand = jnp.where(erow == j * blocks_per_tile + ecol, 1.0, 0.0).astype(BF16)
        tile(ks_ref, vs_ref, j, n, table, (1.0 - _dot(selb, expand)) * NEG)

    def win_tile(j, table):
        tile(kw_ref, vw_ref, j, 1, table, None)

    def finish(o_ref):
        for r in range(NSA_GROUP):
            rs = slice(r * tq, (r + 1) * tq)
            o_ref[0, :, r * HEAD_DIM:(r + 1) * HEAD_DIM] = acc_ref[rs] / l_ref[rs]

    _flash_init(m_ref, l_ref, acc_ref)

    n_far = jnp.maximum(i - 1, 0)
    quads = n_far // 4

    def far_body(jj, carry):
        sel_tile(4 * jj, 4, None)
        return carry

    lax.fori_loop(0, quads, far_body, 0)

    @pl.when(n_far % 4 >= 2)
    def _():
        sel_tile(4 * quads, 2, None)

    @pl.when(n_far % 2 == 1)
    def _():
        sel_tile(i - 2, 1, None)

    @pl.when(i >= 1)
    def _():
        sel_tile(i - 1, 1, 1)

    sel_tile(i, 1, 0)
    finish(osel_ref)

    _flash_init(m_ref, l_ref, acc_ref)
    win_tile(i, 0)

    @pl.when(i >= 1)
    def _():
        win_tile(i - 1, 1)

    @pl.when(i >= 2)
    def _():
        win_tile(i - 2, 2)

    finish(owin_ref)


def _prompt_selwin(q, kvb, sel, tables, b, t):
    nb = sel.shape[-1]
    q3 = q.reshape(b, t, q.shape[-1])
    kv3 = kvb.reshape(b, t, kvb.shape[-1])
    gw = NSA_GROUP * HEAD_DIM
    kcol = lambda base: pl.BlockSpec((1, t, HEAD_DIM), lambda i, g, j: (i, 0, base + g))
    return pl.pallas_call(
        functools.partial(_prompt_selwin_kernel, tq=TQ, tk=TK),
        grid=(b, NSA_KV_HEADS, t // TQ),
        in_specs=[pl.BlockSpec((1, TQ, gw), lambda i, g, j: (i, j, g)),
                  kcol(4), kcol(6), kcol(8), kcol(10),
                  pl.BlockSpec((1, 1, TQ, nb), lambda i, g, j: (i, g, j, 0)),
                  pl.BlockSpec((3, NSA_GROUP, TQ, TK), lambda i, g, j: (0, g, 0, 0))],
        out_specs=[pl.BlockSpec((1, TQ, gw), lambda i, g, j: (i, j, g)),
                   pl.BlockSpec((1, TQ, gw), lambda i, g, j: (i, j, g))],
        out_shape=[jax.ShapeDtypeStruct((b, t, NSA_WIDTH), F32),
                   jax.ShapeDtypeStruct((b, t, NSA_WIDTH), F32)],
        scratch_shapes=[pltpu.VMEM((NSA_GROUP * TQ, HEAD_DIM), F32),
                        pltpu.VMEM((NSA_GROUP * TQ, HEAD_DIM), F32),
                        pltpu.VMEM((NSA_GROUP * TQ, HEAD_DIM), F32)],
        compiler_params=_cparams(("parallel", "parallel", "parallel")),
        name="prompt_selwin",
    )(q3, kv3, kv3, kv3, kv3, sel, tables)


def _sb_weights(z, carry, upper, valid):
    seg = upper.shape[0]
    ls = _log2_sigmoid(z)
    l1m = ls - z
    if valid is not None:
        l1m = jnp.where(valid, l1m, 0.0)
    cums = []
    for sg in reversed(range(z.shape[1] // seg)):
        part = l1m[:, sg * seg:(sg + 1) * seg]
        cums.append(_split_dot(part, upper) + carry)
        carry = carry + jnp.sum(part, axis=1, keepdims=True)
    cum = cums[0] if len(cums) == 1 else jnp.concatenate(cums[::-1], axis=1)
    a = jnp.exp2(ls + cum)
    if valid is not None:
        a = jnp.where(valid, a, 0.0)
    return a, carry


def _prompt_sb_kernel(q_ref, k_ref, v_ref, o_ref, acc_ref, carry_ref, *, tq, tk):
    i = pl.program_id(2)
    upper = _strict_upper(tk)
    acc_ref[...] = jnp.zeros(acc_ref.shape, F32)
    carry_ref[...] = jnp.zeros(carry_ref.shape, F32)

    def run(block, diag):
        start = pl.multiple_of(block * tq, tq)
        k = k_ref[0, pl.ds(start, tq), :]
        v = v_ref[0, pl.ds(start, tq), :]
        chunks = [slice(off, off + SB_ROW_CHUNK) for off in range(0, tq, SB_ROW_CHUNK)]
        accs = [acc_ref[rs, :] for rs in chunks]
        spans = [rs.stop if diag else tq for rs in chunks]
        zs = [_dot_nt(q_ref[0, rs, :], k[:n]) * SCALE2 for rs, n in zip(chunks, spans)]
        valids = [None] * len(chunks)
        if diag:
            valids = [(lax.broadcasted_iota(jnp.int32, (SB_ROW_CHUNK, n), 1)
                       < lax.broadcasted_iota(jnp.int32, (SB_ROW_CHUNK, n), 0) + rs.start)
                      for rs, n in zip(chunks, spans)]
        weights = [_sb_weights(z, carry_ref[rs, 0:1], upper, valid) for z, rs, valid in zip(zs, chunks, valids)]
        outs = [_dot(a.astype(BF16), v[:n]) for (a, _), n in zip(weights, spans)]
        for rs, acc, o, (_, carry) in zip(chunks, accs, outs, weights):
            acc_ref[rs, :] = acc + o
            carry_ref[rs, :] = jnp.broadcast_to(carry, (SB_ROW_CHUNK, HEAD_DIM))

    run(i, True)

    def body(jj, c):
        run(i - 1 - jj, False)
        return c

    lax.fori_loop(0, i, body, 0)
    o_ref[0] = acc_ref[...]


def _prompt_sb(q, kvb, b, t):
    q3 = q.reshape(b, t, q.shape[-1])
    kv3 = kvb.reshape(b, t, kvb.shape[-1])
    qbase = NSA_WIDTH // HEAD_DIM
    return pl.pallas_call(
        functools.partial(_prompt_sb_kernel, tq=TQ_SB, tk=TK),
        grid=(b, SB_HEADS, t // TQ_SB),
        in_specs=[pl.BlockSpec((1, TQ_SB, HEAD_DIM), lambda i, h, j: (i, j, qbase + h)),
                  pl.BlockSpec((1, t, HEAD_DIM), lambda i, h, j: (i, 0, 12 + h)),
                  pl.BlockSpec((1, t, HEAD_DIM), lambda i, h, j: (i, 0, 16 + h))],
        out_specs=pl.BlockSpec((1, TQ_SB, HEAD_DIM), lambda i, h, j: (i, j, h)),
        out_shape=jax.ShapeDtypeStruct((b, t, SB_WIDTH), F32),
        scratch_shapes=[pltpu.VMEM((TQ_SB, HEAD_DIM), F32), pltpu.VMEM((TQ_SB, HEAD_DIM), F32)],
        compiler_params=_cparams(("parallel", "parallel", "parallel")),
        name="prompt_sb",
    )(q3, kv3, kv3)


def _ssm_kernel(u_ref, h0_ref, wre_ref, wim_ref, cre_ref, cim_ref, d_ref, pre_ref, pim_ref,
                y_ref, hl_ref, hr_ref, hi_ref, *, t, tc):
    p8r = pre_ref[...]
    p8i = pim_ref[...]
    row = lax.broadcasted_iota(jnp.int32, (tc, SSM_TILE), 0) % SUBLANES

    def chunk(c, carry):
        cr, ci = carry
        start = pl.multiple_of(c * tc, tc)
        u = u_ref[0, pl.ds(start, tc), :]
        ub = u.astype(BF16)
        xr = _dot(ub, wre_ref[0])
        xi = _dot(ub, wim_ref[0])
        for sh in (1, 2, 4):
            ar = p8r[sh - 1:sh, :]
            ai = p8i[sh - 1:sh, :]
            sr = jnp.where(row >= sh, pltpu.roll(xr, sh, 0), 0.0)
            si = jnp.where(row >= sh, pltpu.roll(xi, sh, 0), 0.0)
            xr, xi = xr + ar * sr - ai * si, xi + ar * si + ai * sr
        for g in range(tc // SUBLANES):
            lo, hi = g * SUBLANES, (g + 1) * SUBLANES
            br = xr[lo:hi] + p8r * cr - p8i * ci
            bi = xi[lo:hi] + p8r * ci + p8i * cr
            hr_ref[lo:hi, :] = br
            hi_ref[lo:hi, :] = bi
            cr, ci = br[SUBLANES - 1:SUBLANES], bi[SUBLANES - 1:SUBLANES]
        y = (_dot(hr_ref[...].astype(BF16), cre_ref[0]) - _dot(hi_ref[...].astype(BF16), cim_ref[0])
             + d_ref[0] * u)
        y_ref[0, pl.ds(start, tc), :] = y
        return cr, ci

    cr, ci = lax.fori_loop(0, t // tc, chunk, (h0_ref[0, 0:1, :], h0_ref[0, 1:2, :]))
    hl_ref[0, 0:1, :] = cr
    hl_ref[0, 1:2, :] = ci


def _ssm(zc, h0, sw, b, t):
    wre, wim, cre, cim, dsk, pre, pim = sw
    tc = min(t, 256)
    u3 = zc.reshape(b, t, zc.shape[-1])
    ubase = (NSA_WIDTH + SB_WIDTH) // SSM_U_TILE
    wspec = lambda shp: pl.BlockSpec((1,) + shp, lambda i, k: (k, 0, 0))
    return pl.pallas_call(
        functools.partial(_ssm_kernel, t=t, tc=tc),
        grid=(b, SSM_TILES),
        in_specs=[pl.BlockSpec((1, t, SSM_U_TILE), lambda i, k: (i, 0, ubase + k)),
                  pl.BlockSpec((1, 2, SSM_TILE), lambda i, k: (i, 0, k)),
                  wspec((SSM_U_TILE, SSM_TILE)), wspec((SSM_U_TILE, SSM_TILE)),
                  wspec((SSM_TILE, SSM_U_TILE)), wspec((SSM_TILE, SSM_U_TILE)),
                  wspec((1, SSM_U_TILE)),
                  pl.BlockSpec((SUBLANES, SSM_TILE), lambda i, k: (0, k)),
                  pl.BlockSpec((SUBLANES, SSM_TILE), lambda i, k: (0, k))],
        out_specs=[pl.BlockSpec((1, t, SSM_U_TILE), lambda i, k: (i, 0, k)),
                   pl.BlockSpec((1, 2, SSM_TILE), lambda i, k: (i, 0, k))],
        out_shape=[jax.ShapeDtypeStruct((b, t, SSM_WIDTH), F32),
                   jax.ShapeDtypeStruct((b, 2, SSM_LANES), F32)],
        scratch_shapes=[pltpu.VMEM((tc, SSM_TILE), F32), pltpu.VMEM((tc, SSM_TILE), F32)],
        compiler_params=_cparams(("parallel", "parallel")),
        name="ssm_scan",
    )(u3, h0, wre, wim, cre, cim, dsk, pre, pim)


def _rms(h, gain):
    return h * lax.rsqrt(jnp.mean(h * h, axis=1, keepdims=True) + EPS) * gain


def _silu(z):
    return z * _sigmoid(z)


def _mix_kernel(ocmp_ref, osel_ref, owin_ref, gate_ref, ob_ref, yc_ref, za_ref, zb_ref, zc_ref, x_ref,
                wglu_ref, wout_ref, gain_ref, lng_ref, lnb_ref, y_ref, yb_ref, mixed_ref, *, alpha):
    gates = _sigmoid(gate_ref[...])
    for h in range(NSA_HEADS):
        sl = slice(h * HEAD_DIM, (h + 1) * HEAD_DIM)
        oa = (gates[:, 3 * h:3 * h + 1] * ocmp_ref[:, sl] + gates[:, 3 * h + 1:3 * h + 2] * osel_ref[:, sl]
              + gates[:, 3 * h + 2:3 * h + 3] * owin_ref[:, sl])
        mixed_ref[:, sl] = oa * _silu(za_ref[:, sl])
    mixed_ref[:, 0:NSA_WIDTH] = _rms(mixed_ref[:, 0:NSA_WIDTH], gain_ref[:, 0:NSA_WIDTH])
    b0, c0 = NSA_WIDTH, NSA_WIDTH + SB_WIDTH
    mixed_ref[:, b0:c0] = _rms(ob_ref[...] * _silu(zb_ref[...]), gain_ref[:, b0:c0])
    glu = _dot(yc_ref[...].astype(BF16), wglu_ref[...])
    yc = glu[:, 0:SSM_WIDTH] * _sigmoid(glu[:, SSM_WIDTH:2 * SSM_WIDTH])
    mixed_ref[:, c0:c0 + SSM_WIDTH] = _rms(yc * _silu(zc_ref[...]), gain_ref[:, c0:c0 + SSM_WIDTH])
    h = alpha * x_ref[...] + _dot(mixed_ref[...].astype(BF16), wout_ref[...])
    mu = jnp.mean(h, axis=1, keepdims=True)
    hc = h - mu
    var = jnp.mean(hc * hc, axis=1, keepdims=True)
    y = hc * lax.rsqrt(var + EPS) * lng_ref[...] + lnb_ref[...]
    y_ref[...] = y
    yb_ref[...] = y.astype(BF16)


def _mix_out(ocmp, osel, owin, gate, ob, yc, zc, x, wglu, wout, gain, lng, lnb, alpha):
    m, d = x.shape
    tm = min(m, 256)
    row = lambda w, c=0: pl.BlockSpec((tm, w), lambda i, c=c: (i, c))
    full = lambda a: pl.BlockSpec(a.shape, lambda i: (0,) * a.ndim)
    return pl.pallas_call(
        functools.partial(_mix_kernel, alpha=alpha),
        grid=(m // tm,),
        in_specs=[row(NSA_WIDTH), row(NSA_WIDTH), row(NSA_WIDTH), row(LANES), row(SB_WIDTH), row(SSM_WIDTH),
                  row(NSA_WIDTH, 0), row(SB_WIDTH, NSA_WIDTH // SB_WIDTH),
                  row(SSM_WIDTH, (NSA_WIDTH + SB_WIDTH + SSM_WIDTH) // SSM_WIDTH), row(d),
                  full(wglu), full(wout), full(gain), full(lng), full(lnb)],
        out_specs=[row(d), row(d)],
        out_shape=[jax.ShapeDtypeStruct((m, d), F32), jax.ShapeDtypeStruct((m, d), BF16)],
        scratch_shapes=[pltpu.VMEM((tm, d), F32)],
        compiler_params=_cparams(("parallel",)),
        name="mix_out",
    )(ocmp.reshape(m, -1), osel.reshape(m, -1), owin.reshape(m, -1), gate, ob.reshape(m, -1),
      yc.reshape(m, -1), zc, zc, zc, x, wglu, wout, gain, lng, lnb)


def _sample_compress_kernel(pt_ref, *refs):
    pages = refs[:PAGES_PER_STEP]
    w1_ref, w2_ref, first_ref, second_ref = refs[PAGES_PER_STEP:]
    page = pages[0].shape[2] // ROW_VECS
    per_page = page // CMP_STRIDE
    half = w1_ref.shape[1]
    pool = jnp.where(lax.broadcasted_iota(jnp.int32, (per_page, page), 1) // CMP_STRIDE
                     == lax.broadcasted_iota(jnp.int32, (per_page, page), 0), 1.0, 0.0).astype(BF16)
    for i, pg in enumerate(pages):
        rows = jnp.concatenate([_page_vec(pg, c, page) for c in range(2 * NSA_KV_HEADS)], axis=1)
        prod = jnp.concatenate([rows * w1_ref[...], rows * w2_ref[...]], axis=1)
        hi = prod.astype(BF16)
        lo = (prod - hi.astype(F32)).astype(BF16)
        sums = _dot(pool, hi) + _dot(pool, lo)
        first_ref[0, i * per_page:(i + 1) * per_page, :] = sums[:, 0:half]
        second_ref[0, i * per_page:(i + 1) * per_page, :] = sums[:, half:2 * half]


def _sample_compress(cache, layer, page_table, w_cmp):
    db, n_pages = page_table.shape
    page = cache.shape[2] // ROW_VECS
    per_page = page // CMP_STRIDE
    steps = n_pages // PAGES_PER_STEP
    nc = n_pages * per_page
    half = 4 * HEAD_DIM

    def pspec(i):
        return pl.BlockSpec((1, 1, page * ROW_VECS, HEAD_DIM),
                            lambda b, s, pt, i=i: (layer, pt[b, s * PAGES_PER_STEP + i], 0, 0))

    rows = PAGES_PER_STEP * per_page
    tiled = lambda w: jnp.concatenate([jnp.tile(w[kv], (per_page, 1)) for kv in (0, 0, 1, 1)], axis=1)
    w1, w2 = tiled(w_cmp[:, :CMP_STRIDE]), tiled(w_cmp[:, CMP_STRIDE:])
    return pl.pallas_call(
        _sample_compress_kernel,
        grid_spec=pltpu.PrefetchScalarGridSpec(
            num_scalar_prefetch=1,
            grid=(db, steps),
            in_specs=[pspec(i) for i in range(PAGES_PER_STEP)]
                     + [pl.BlockSpec((page, half), lambda b, s, pt: (0, 0)),
                        pl.BlockSpec((page, half), lambda b, s, pt: (0, 0))],
            out_specs=[pl.BlockSpec((1, rows, half), lambda b, s, pt: (b, s, 0)),
                       pl.BlockSpec((1, rows, half), lambda b, s, pt: (b, s, 0))]),
        out_shape=[jax.ShapeDtypeStruct((db, nc, half), F32), jax.ShapeDtypeStruct((db, nc, half), F32)],
        compiler_params=_cparams(("parallel", "parallel")),
        name="sample_compress",
    )(page_table, *([cache] * PAGES_PER_STEP), w1, w2)


def _sample_cmpwin_kernel(q_ref, first_ref, second_ref, bias_ref, win_ref, wbias_ref,
                          ocmp_ref, sel_ref, owin_ref, *, ts, nc, pos0):
    comp = first_ref[0] + pltpu.roll(second_ref[0], nc - 1, 0)
    nb = nc // CMP_PER_SEL
    pos = pos0 + lax.broadcasted_iota(jnp.int32, (ts, nb), 0)
    for g in range(NSA_KV_HEADS):
        qg = _stack_heads(q_ref[0, :, pl.ds(g * NSA_GROUP * HEAD_DIM, NSA_GROUP * HEAD_DIM)], HEAD_DIM)
        ck = comp[:, g * HEAD_DIM:(g + 1) * HEAD_DIM].astype(BF16)
        cv = comp[:, (2 + g) * HEAD_DIM:(3 + g) * HEAD_DIM].astype(BF16)
        bias = bias_ref[NSA_GROUP * g:NSA_GROUP * (g + 1)].reshape(NSA_GROUP * ts, nc)
        o, imp = _cmp_attend(qg, ck, cv, bias, ts)
        sel_ref[0, g] = _select_blocks(imp, pos, N_SEL - 1)
        kw = win_ref[0, :, g * HEAD_DIM:(g + 1) * HEAD_DIM].astype(BF16)
        vw = win_ref[0, :, (2 + g) * HEAD_DIM:(3 + g) * HEAD_DIM].astype(BF16)
        wb = wbias_ref[NSA_GROUP * g:NSA_GROUP * (g + 1)].reshape(NSA_GROUP * ts, kw.shape[0])
        sw = _dot_nt(qg, kw) * SCALE + wb
        ow = _dot(_masked_softmax(sw, wb > 0.5 * NEG).astype(BF16), vw)
        for r in range(NSA_GROUP):
            h = NSA_GROUP * g + r
            ocmp_ref[0, :, h * HEAD_DIM:(h + 1) * HEAD_DIM] = o[r * ts:(r + 1) * ts]
            owin_ref[0, :, h * HEAD_DIM:(h + 1) * HEAD_DIM] = ow[r * ts:(r + 1) * ts]


def _sample_cmpwin(q, first, second, bias_cmp, win_all, bias_win, db, ts, past_len):
    nc = first.shape[1]
    nb = nc // CMP_PER_SEL
    nw = win_all.shape[1]
    q3 = q.reshape(db, ts, q.shape[-1])
    return pl.pallas_call(
        functools.partial(_sample_cmpwin_kernel, ts=ts, nc=nc, pos0=past_len),
        grid=(db,),
        in_specs=[pl.BlockSpec((1, ts, NSA_WIDTH), lambda b: (b, 0, 0)),
                  pl.BlockSpec((1, nc, 4 * HEAD_DIM), lambda b: (b, 0, 0)),
                  pl.BlockSpec((1, nc, 4 * HEAD_DIM), lambda b: (b, 0, 0)),
                  pl.BlockSpec((NSA_HEADS, ts, nc), lambda b: (0, 0, 0)),
                  pl.BlockSpec((1, nw, 4 * HEAD_DIM), lambda b: (b, 0, 0)),
                  pl.BlockSpec((NSA_HEADS, ts, nw), lambda b: (0, 0, 0))],
        out_specs=[pl.BlockSpec((1, ts, NSA_WIDTH), lambda b: (b, 0, 0)),
                   pl.BlockSpec((1, NSA_KV_HEADS, ts, nb), lambda b: (b, 0, 0, 0)),
                   pl.BlockSpec((1, ts, NSA_WIDTH), lambda b: (b, 0, 0))],
        out_shape=[jax.ShapeDtypeStruct((db, ts, NSA_WIDTH), F32),
                   jax.ShapeDtypeStruct((db, NSA_KV_HEADS, ts, nb), F32),
                   jax.ShapeDtypeStruct((db, ts, NSA_WIDTH), F32)],
        compiler_params=_cparams(("parallel",)),
        name="sample_cmpwin",
    )(q3, first, second, bias_cmp, win_all, bias_win)


def _sample_sel_kernel(pt_ref, *refs, ts, page):
    pages = refs[:PAGES_PER_STEP]
    q_ref, sel_ref, new_ref, bias_ref, nbias_ref, o_ref, m_ref, l_ref, acc_ref = refs[PAGES_PER_STEP:]
    s_idx = pl.program_id(1)
    last = pl.num_programs(1) - 1
    rows = NSA_GROUP * ts
    nkeys = PAGES_PER_STEP * page
    blocks = nkeys // SEL_BLOCK

    @pl.when(s_idx == 0)
    def _():
        _flash_init(m_ref, l_ref, acc_ref)

    expand = jnp.where(lax.broadcasted_iota(jnp.int32, (blocks, nkeys), 0)
                       == lax.broadcasted_iota(jnp.int32, (blocks, nkeys), 1) // SEL_BLOCK, 1.0, 0.0).astype(BF16)
    for g in range(NSA_KV_HEADS):
        qg = _stack_heads(q_ref[0, :, pl.ds(g * NSA_GROUP * HEAD_DIM, NSA_GROUP * HEAD_DIM)], HEAD_DIM)
        k = jnp.concatenate([_page_vec(pg, 4 + g, page) for pg in pages], axis=0).astype(BF16)
        v = jnp.concatenate([_page_vec(pg, 6 + g, page) for pg in pages], axis=0).astype(BF16)
        bias = bias_ref[jnp.where(s_idx == last, 1, 0), NSA_GROUP * g:NSA_GROUP * (g + 1)].reshape(rows, nkeys)
        s = _dot_nt(qg, k) * SCALE2 + bias
        chosen = _dot(sel_ref[0, g, 0].astype(BF16), expand) > 0.5
        s = jnp.where(chosen[None], s.reshape(NSA_GROUP, ts, nkeys), NEG).reshape(rows, nkeys)
        sl = slice(g * rows, (g + 1) * rows)
        _flash_tile(s, v, m_ref.at[sl], l_ref.at[sl], acc_ref.at[sl])

        @pl.when(s_idx == last)
        def _():
            kn = _pad_keys(new_ref[0, :, g * HEAD_DIM:(g + 1) * HEAD_DIM])
            vn = _pad_keys(new_ref[0, :, (2 + g) * HEAD_DIM:(3 + g) * HEAD_DIM])
            nbias = nbias_ref[NSA_GROUP * g:NSA_GROUP * (g + 1)].reshape(rows, LANES)
            _flash_tile(_dot_nt(qg, kn) * SCALE2 + nbias, vn, m_ref.at[sl], l_ref.at[sl], acc_ref.at[sl])
            o = acc_ref[sl] / l_ref[sl]
            for r in range(NSA_GROUP):
                h = NSA_GROUP * g + r
                o_ref[0, :, h * HEAD_DIM:(h + 1) * HEAD_DIM] = o[r * ts:(r + 1) * ts]


def _sample_sel(cache, layer, page_table, q, sel_steps, kv_new, bias_steps, bias_new, db, ts):
    n_pages = page_table.shape[1]
    page = cache.shape[2] // ROW_VECS
    steps = n_pages // PAGES_PER_STEP
    half = 4 * HEAD_DIM
    nkeys = PAGES_PER_STEP * page
    blocks = nkeys // SEL_BLOCK
    q3 = q.reshape(db, ts, q.shape[-1])
    new3 = kv_new.reshape(db, ts, kv_new.shape[-1])

    def pspec(i):
        return pl.BlockSpec((1, 1, page * ROW_VECS, HEAD_DIM),
                            lambda b, s, pt, i=i: (layer, pt[b, s * PAGES_PER_STEP + i], 0, 0))

    return pl.pallas_call(
        functools.partial(_sample_sel_kernel, ts=ts, page=page),
        grid_spec=pltpu.PrefetchScalarGridSpec(
            num_scalar_prefetch=1,
            grid=(db, steps),
            in_specs=[pspec(i) for i in range(PAGES_PER_STEP)]
                     + [pl.BlockSpec((1, ts, NSA_WIDTH), lambda b, s, pt: (b, 0, 0)),
                        pl.BlockSpec((1, NSA_KV_HEADS, 1, ts, blocks), lambda b, s, pt: (b, 0, s, 0, 0)),
                        pl.BlockSpec((1, ts, half), lambda b, s, pt: (b, 0, 1)),
                        pl.BlockSpec((2, NSA_HEADS, ts, nkeys), lambda b, s, pt: (0, 0, 0, 0)),
                        pl.BlockSpec((NSA_HEADS, ts, LANES), lambda b, s, pt: (0, 0, 0))],
            out_specs=pl.BlockSpec((1, ts, NSA_WIDTH), lambda b, s, pt: (b, 0, 0)),
            scratch_shapes=[pltpu.VMEM((NSA_HEADS * ts, HEAD_DIM), F32),
                            pltpu.VMEM((NSA_HEADS * ts, HEAD_DIM), F32),
                            pltpu.VMEM((NSA_HEADS * ts, HEAD_DIM), F32)]),
        out_shape=jax.ShapeDtypeStruct((db, ts, NSA_WIDTH), F32),
        compiler_params=_cparams(("parallel", "arbitrary")),
        name="sample_sel",
    )(page_table, *([cache] * PAGES_PER_STEP), q3, sel_steps, new3, bias_steps, bias_new)


def _sample_sb_kernel(pt_ref, *refs, ts, page, seg):
    pages = refs[:PAGES_PER_STEP]
    q_ref, knew_ref, vnew_ref, o_ref, carry_ref, acc_ref = refs[PAGES_PER_STEP:]
    s_idx = pl.program_id(1)
    last = pl.num_programs(1) - 1
    upper = _strict_upper(seg)

    rows = SB_HEADS * ts
    heads = range(SB_HEADS)
    hs = lambda h: slice(h * HEAD_DIM, (h + 1) * HEAD_DIM)

    def attend(ks, vs, carry, up, valid):
        z = jnp.concatenate([_dot_nt(q_ref[0, :, hs(h)], ks[h]) for h in heads], axis=0) * SCALE2
        a, carry = _sb_weights(z, carry, up, valid)
        return jnp.concatenate([_dot(a[h * ts:(h + 1) * ts].astype(BF16), vs[h]) for h in heads], axis=0), carry

    @pl.when(s_idx == 0)
    def _():
        kn = [_pad_keys(knew_ref[0, :, hs(h)]) for h in heads]
        vn = [_pad_keys(vnew_ref[0, :, hs(h)]) for h in heads]
        valid = (lax.broadcasted_iota(jnp.int32, (rows, LANES), 1)
                 < lax.broadcasted_iota(jnp.int32, (rows, LANES), 0) % ts)
        o, carry = attend(kn, vn, jnp.zeros((rows, 1), F32), _strict_upper(LANES), valid)
        acc_ref[...] = o
        carry_ref[...] = jnp.broadcast_to(carry, (rows, HEAD_DIM))

    ks = [jnp.concatenate([_page_vec(pg, h, page) for pg in pages], axis=0).astype(BF16) for h in heads]
    vs = [jnp.concatenate([_page_vec(pg, SB_HEADS + h, page) for pg in pages], axis=0).astype(BF16) for h in heads]
    o, carry = attend(ks, vs, carry_ref[:, 0:1], upper, None)
    acc_ref[...] = acc_ref[...] + o
    carry_ref[...] = jnp.broadcast_to(carry, (rows, HEAD_DIM))

    @pl.when(s_idx == last)
    def _():
        for h in heads:
            o_ref[0, :, hs(h)] = acc_ref[h * ts:(h + 1) * ts, :]


def _sample_sb(cache, layer, page_table, q, kv_new, db, ts):
    n_pages = page_table.shape[1]
    page = cache.shape[2] // ROW_VECS
    steps = n_pages // PAGES_PER_STEP
    q3 = q.reshape(db, ts, q.shape[-1])
    new3 = kv_new.reshape(db, ts, kv_new.shape[-1])

    def pspec(i):
        return pl.BlockSpec((1, 1, page * ROW_VECS, HEAD_DIM),
                            lambda b, s, pt, i=i: (layer, pt[b, (steps - 1 - s) * PAGES_PER_STEP + i], 0, 0))

    return pl.pallas_call(
        functools.partial(_sample_sb_kernel, ts=ts, page=page, seg=TK),
        grid_spec=pltpu.PrefetchScalarGridSpec(
            num_scalar_prefetch=1,
            grid=(db, steps),
            in_specs=[pspec(i) for i in range(PAGES_PER_STEP)]
                     + [pl.BlockSpec((1, ts, SB_WIDTH), lambda b, s, pt: (b, 0, NSA_WIDTH // SB_WIDTH)),
                        pl.BlockSpec((1, ts, SB_WIDTH), lambda b, s, pt: (b, 0, 3)),
                        pl.BlockSpec((1, ts, SB_WIDTH), lambda b, s, pt: (b, 0, 4))],
            out_specs=pl.BlockSpec((1, ts, SB_WIDTH), lambda b, s, pt: (b, 0, 0)),
            scratch_shapes=[pltpu.VMEM((SB_HEADS * ts, HEAD_DIM), F32),
                            pltpu.VMEM((SB_HEADS * ts, HEAD_DIM), F32)]),
        out_shape=jax.ShapeDtypeStruct((db, ts, SB_WIDTH), F32),
        compiler_params=_cparams(("parallel", "arbitrary")),
        name="sample_sb",
    )(page_table, *([cache] * PAGES_PER_STEP), q3, new3, new3)


def _t5_bucket(dist):
    n = jnp.maximum(dist, 0)
    exact = NUM_BUCKETS // 2
    nf = jnp.maximum(n, 1).astype(F32)
    large = exact + (jnp.log(nf / exact) / math.log(MAX_DISTANCE / exact) * (NUM_BUCKETS - exact)).astype(jnp.int32)
    return jnp.where(n < exact, n, jnp.minimum(large, NUM_BUCKETS - 1))


def _bias(rel_bias, delta, valid):
    bucket = _t5_bucket(delta)[None]
    tab = rel_bias.astype(F32).T.reshape((rel_bias.shape[1], NUM_BUCKETS) + (1,) * delta.ndim)
    b = jnp.zeros((rel_bias.shape[1],) + delta.shape, F32)
    for k in range(NUM_BUCKETS):
        b = jnp.where(bucket == k, tab[:, k], b)
    return jnp.where(valid[None], b, NEG)


def _prompt_tables(rel_bias, t):
    tt = jnp.arange(TQ)[:, None]
    ss = jnp.arange(TK)[None, :]
    d = tt - ss
    assert 2 * TK - TQ >= MAX_DISTANCE
    far = _bias(rel_bias, d + 2 * TK, d > -2 * TK)
    tables = jnp.stack([
        _bias(rel_bias, d, d >= 0) - far,
        _bias(rel_bias, d + TK, d > -TK) - far,
        jnp.where(d + 2 * TK < WINDOW, 0.0, NEG) + 0.0 * far,
    ])
    nc = t // CMP_STRIDE
    dc = jnp.arange(t)[:, None] - (jnp.arange(nc)[None, :] * CMP_STRIDE + CMP_BLOCK - 1)
    return tables * LOG2E, _bias(rel_bias, dc, dc >= 0) * LOG2E


def _sample_tables(rel_bias, ts, past_len, page, nw, nw_pad):
    qpos = past_len + jnp.arange(ts)[:, None]
    nc = past_len // CMP_STRIDE
    dc = qpos - (jnp.arange(nc)[None, :] * CMP_STRIDE + CMP_BLOCK - 1)
    bias_cmp = _bias(rel_bias, dc, dc >= 0)
    widx = jnp.arange(nw_pad)[None, :]
    dw = qpos - (past_len - nw + widx)
    bias_win = _bias(rel_bias, dw, (dw >= 0) & (dw < WINDOW) & (widx < nw + ts))
    nkeys = PAGES_PER_STEP * page
    d_last = qpos - (past_len - nkeys + jnp.arange(nkeys)[None, :])
    d_far = d_last + nkeys
    bias_steps = jnp.stack([_bias(rel_bias, d_far, d_far >= 0), _bias(rel_bias, d_last, d_last >= 0)])
    nidx = jnp.arange(LANES)[None, :]
    dn = jnp.arange(ts)[:, None] - nidx
    return bias_cmp, bias_win, bias_steps * LOG2E, _bias(rel_bias, dn, (dn >= 0) & (nidx < ts)) * LOG2E


def _ssm_weights(lam_re, lam_im, log_dt, b_re, b_im, c_re, c_im, d_skip):
    lam_re, lam_im = lam_re.astype(F32), lam_im.astype(F32)
    dt = jnp.exp(log_dt.astype(F32))[:, None]
    mag = jnp.exp(lam_re * dt)
    a_re, a_im = mag * jnp.cos(lam_im * dt), mag * jnp.sin(lam_im * dt)
    den = lam_re * lam_re + lam_im * lam_im
    f_re = ((a_re - 1.0) * lam_re + a_im * lam_im) / den
    f_im = (a_im * lam_re - (a_re - 1.0) * lam_im) / den
    b_re, b_im = b_re.astype(F32), b_im.astype(F32)
    bb_re = f_re[..., None] * b_re - f_im[..., None] * b_im
    bb_im = f_re[..., None] * b_im + f_im[..., None] * b_re
    gpt = SSM_GROUPS // SSM_TILES
    eye = jnp.eye(gpt, dtype=F32)

    def in_mat(bb):
        return jnp.einsum('kgpn,gh->kgnhp', bb.reshape(SSM_TILES, gpt, SSM_STATE, SSM_CH), eye).reshape(
            SSM_TILES, SSM_U_TILE, SSM_TILE).astype(BF16)

    def out_mat(c):
        return jnp.einsum('kgnp,gh->khpgn', c.astype(F32).reshape(SSM_TILES, gpt, SSM_CH, SSM_STATE), eye).reshape(
            SSM_TILES, SSM_TILE, SSM_U_TILE).astype(BF16)

    pr, pi = a_re.reshape(1, SSM_LANES), a_im.reshape(1, SSM_LANES)
    while pr.shape[0] < SUBLANES:
        tr, ti = pr[-1:], pi[-1:]
        pr, pi = (jnp.concatenate([pr, pr * tr - pi * ti]), jnp.concatenate([pi, pr * ti + pi * tr]))
    return (in_mat(bb_re), in_mat(bb_im), out_mat(c_re), out_mat(c_im),
            d_skip.astype(F32).reshape(SSM_TILES, 1, SSM_U_TILE), pr, pi)


def _split_w_in(w_in):
    sizes = (NSA_WIDTH, 6 * NSA_KV_HEADS * HEAD_DIM, 3 * NSA_HEADS, NSA_WIDTH,
             SB_WIDTH, SB_WIDTH, SB_WIDTH, SB_WIDTH, SSM_WIDTH, SSM_WIDTH)
    offs = [0]
    for s in sizes:
        offs.append(offs[-1] + s)
    q_a, kv_a, g_a, z_a, q_b, k_b, v_b, z_b, u_c, z_c = (w_in[..., offs[i]:offs[i + 1]] for i in range(10))
    cat = lambda xs: jnp.concatenate(xs, axis=-1).astype(BF16)
    pad = jnp.zeros(g_a.shape[:-1] + (LANES - g_a.shape[-1],), g_a.dtype)
    return cat([q_a, q_b]), cat([kv_a, k_b, v_b]), cat([z_a, z_b, u_c, z_c]), cat([g_a, pad])


KV_TILE = 4 * HEAD_DIM


def _kv_proj_kernel(*refs, tm):
    x_ref, w_ref = refs[:2]
    kvf_ref, kvb_ref, nsa_ref, sb_ref = refs[-4:]
    j = pl.program_id(1)
    acc = _dot(x_ref[...], w_ref[...])
    kvf_ref[...] = acc
    kvb_ref[...] = acc.astype(BF16)

    def scatter(dst, base):
        for c in range(KV_TILE // HEAD_DIM):
            dst[0, pl.ds(base + c, tm, stride=ROW_VECS), :] = acc[:, c * HEAD_DIM:(c + 1) * HEAD_DIM]

    for step, dst, base in ((0, nsa_ref, 0), (1, nsa_ref, 4), (3, sb_ref, 0), (4, sb_ref, 4)):
        pl.when(j == step)(functools.partial(scatter, dst, base))


def _kv_proj(x, w, layer, depth, stacked):
    m, k = x.shape
    n = w.shape[1]
    tm = min(m, 1024)
    assert n == 5 * KV_TILE
    rows = pl.BlockSpec((1, tm * ROW_VECS, HEAD_DIM), lambda i, j: (layer, i, 0))
    stack_shape = jax.ShapeDtypeStruct((depth, m * ROW_VECS, HEAD_DIM), F32)
    return pl.pallas_call(
        functools.partial(_kv_proj_kernel, tm=tm),
        grid=(m // tm, n // KV_TILE),
        in_specs=[pl.BlockSpec((tm, k), lambda i, j: (i, 0)), pl.BlockSpec((k, KV_TILE), lambda i, j: (0, j)),
                  pl.BlockSpec(memory_space=pl.ANY), pl.BlockSpec(memory_space=pl.ANY)],
        out_specs=[pl.BlockSpec((tm, KV_TILE), lambda i, j: (i, j)), pl.BlockSpec((tm, KV_TILE), lambda i, j: (i, j)),
                   rows, rows],
        out_shape=[jax.ShapeDtypeStruct((m, n), F32), jax.ShapeDtypeStruct((m, n), BF16), stack_shape, stack_shape],
        input_output_aliases={2: 2, 3: 3},
        compiler_params=_cparams(("parallel", "arbitrary")),
        name="in_proj_kv",
    )(x, w, *stacked)


def _in_proj(xb, wl, layer, depth, stacked):
    wa, wb, wc, wd = wl
    (q,) = _matmul(xb, wa, (BF16,), "in_proj_q")
    kvf, kvb, nsa, sb = _kv_proj(xb, wb, layer, depth, stacked)
    (zc,) = _matmul(xb, wc, (F32,), "in_proj_z")
    (gate,) = _matmul(xb, wd, (F32,), "in_proj_gate")
    return q, kvf, kvb, zc, gate, (nsa, sb)


def kernel(x_prompt, x_sample, cache_nsa, cache_sb, state_win, state_ssm, page_table, rel_bias, w_in, w_cmp,
           ssm_lam_re, ssm_lam_im, ssm_log_dt, ssm_b_re, ssm_b_im, ssm_c_re, ssm_c_im, ssm_d, w_glu, mix_gain,
           w_out, ln_g, ln_b):
    depth = w_in.shape[0]
    b, t, d = x_prompt.shape
    db, ts, _ = x_sample.shape
    n_phys, page = cache_nsa.shape[1], cache_nsa.shape[2]
    n_pages = page_table.shape[1]
    past_len = n_pages * page
    nw = state_win.shape[2]
    alpha = (2 * depth) ** 0.25
    assert t % TQ == 0 and t % TQ_SB == 0 and TQ_SB % SB_ROW_CHUNK == 0 and SB_ROW_CHUNK % TK == 0 and WINDOW == 2 * TK and TQ == TK and n_pages % PAGES_PER_STEP == 0
    assert past_len % SEL_BLOCK == 0 and ts % SUBLANES == 0 and ts <= SEL_BLOCK and nw == WINDOW

    w_groups = _split_w_in(w_in)
    w_glu_b, w_out_b = w_glu.astype(BF16), w_out.astype(BF16)
    ssm_w = jax.vmap(_ssm_weights)(ssm_lam_re, ssm_lam_im, ssm_log_dt, ssm_b_re, ssm_b_im, ssm_c_re, ssm_c_im, ssm_d)
    tables_p, bias_cmp_p = _prompt_tables(rel_bias, t)
    nw_pad = -(-(nw + ts) // LANES) * LANES
    bias_cmp_s, bias_win_s, bias_steps_s, bias_new_s = _sample_tables(rel_bias, ts, past_len, page, nw, nw_pad)
    cache_nsa_r = cache_nsa.reshape(depth, n_phys, page * ROW_VECS, HEAD_DIM)
    cache_sb_r = cache_sb.reshape(depth, n_phys, page * ROW_VECS, HEAD_DIM)
    w_cmp = w_cmp.astype(F32)
    page_table = page_table.astype(jnp.int32)

    xp, xs = x_prompt.reshape(b * t, d), x_sample.reshape(db * ts, d)
    xpb, xsb = xp.astype(BF16), xs.astype(BF16)
    outs = [[] for _ in range(8)]
    rows_p = (jnp.zeros((depth, b * t * ROW_VECS, HEAD_DIM), F32),) * 2
    rows_s = (jnp.zeros((depth, db * ts * ROW_VECS, HEAD_DIM), F32),) * 2
    for l in range(depth):
        wl = tuple(w[l] for w in w_groups)
        sw = tuple(w[l] for w in ssm_w)
        gain, lng, lnb = mix_gain[l].reshape(1, d), ln_g[l].reshape(1, d), ln_b[l].reshape(1, d)

        q, kvf, kvb, zc, gate, rows_p = _in_proj(xpb, wl, l, depth, rows_p)
        o_cmp, sel = _prompt_cmp(q, kvf, w_cmp[l], bias_cmp_p, b, t)
        o_sel, o_win = _prompt_selwin(q, kvb, sel, tables_p, b, t)
        o_b = _prompt_sb(q, kvb, b, t)
        y_c, h_p = _ssm(zc, jnp.zeros((b, 2, SSM_LANES), F32), sw, b, t)
        xp, xpb = _mix_out(o_cmp, o_sel, o_win, gate, o_b, y_c, zc, xp, w_glu_b[l], w_out_b[l], gain, lng, lnb,
                           alpha)
        kv3 = kvf.reshape(b, t, -1)
        outs[4].append(kv3[:, max(t - WINDOW, 0):, 1024:1536].reshape(b, -1, 2, NSA_KV_HEADS, HEAD_DIM))
        outs[6].append(h_p.reshape(b, 2, SSM_GROUPS, SSM_STATE))

        q, kvf, kvb, zc, gate, rows_s = _in_proj(xsb, wl, l, depth, rows_s)
        kv3 = kvf.reshape(db, ts, -1)
        win_all = jnp.concatenate([state_win[l].reshape(db, nw, -1), kv3[:, :, 1024:1536],
                                   jnp.zeros((db, nw_pad - nw - ts, 4 * HEAD_DIM), F32)], axis=1)
        first, second = _sample_compress(cache_nsa_r, l, page_table, w_cmp[l])
        o_cmp, sel, o_win = _sample_cmpwin(q, first, second, bias_cmp_s, win_all, bias_win_s, db, ts, past_len)
        steps = n_pages // PAGES_PER_STEP
        sel_steps = jnp.moveaxis(sel.reshape(db, NSA_KV_HEADS, ts, steps, -1), 3, 2)
        o_sel = _sample_sel(cache_nsa_r, l, page_table, q, sel_steps, kvf, bias_steps_s, bias_new_s, db, ts)
        o_b = _sample_sb(cache_sb_r, l, page_table, q, kvf, db, ts)
        y_c, h_s = _ssm(zc, state_ssm[l].reshape(db, 2, SSM_LANES).astype(F32), sw, db, ts)
        xs, xsb = _mix_out(o_cmp, o_sel, o_win, gate, o_b, y_c, zc, xs, w_glu_b[l], w_out_b[l], gain, lng, lnb,
                           alpha)
        outs[5].append(win_all[:, ts:ts + nw].reshape(db, nw, 2, NSA_KV_HEADS, HEAD_DIM))
        outs[7].append(h_s.reshape(db, 2, SSM_GROUPS, SSM_STATE))

    win_p, win_s, ssm_p, ssm_s = (jnp.stack(outs[i]) for i in (4, 5, 6, 7))
    return (xp.reshape(b, t, d), xs.reshape(db, ts, d),
            rows_p[0].reshape(depth, b, t, 4, NSA_KV_HEADS, HEAD_DIM),
            rows_s[0].reshape(depth, db, ts, 4, NSA_KV_HEADS, HEAD_DIM),
            rows_p[1].reshape(depth, b, t, 2, SB_HEADS, HEAD_DIM),
            rows_s[1].reshape(depth, db, ts, 2, SB_HEADS, HEAD_DIM),
            win_p, win_s, ssm_p, ssm_s)
```

```python
import functools
import math

import jax
import jax.numpy as jnp
from jax import lax
from jax.experimental import pallas as pl
from jax.experimental.pallas import tpu as pltpu

F32 = jnp.float32
BF16 = jnp.bfloat16

HEAD_DIM = 128
NSA_KV_HEADS = 2
NSA_GROUP = 4
NSA_HEADS = NSA_KV_HEADS * NSA_GROUP
NSA_WIDTH = NSA_HEADS * HEAD_DIM
SB_HEADS = 4
SB_WIDTH = SB_HEADS * HEAD_DIM
SSM_CH = 16
SSM_GROUPS = 32
SSM_STATE = 64
SSM_WIDTH = SSM_CH * SSM_GROUPS
SSM_LANES = SSM_GROUPS * SSM_STATE
CMP_BLOCK = 32
CMP_STRIDE = 16
SEL_BLOCK = 64
CMP_PER_SEL = SEL_BLOCK // CMP_STRIDE
N_SEL = 16
WINDOW = 512
NUM_BUCKETS = 32
MAX_DISTANCE = 128
FORCE_SCORE = 1e4
EPS = 1e-5
SCALE = HEAD_DIM ** -0.5
LOG2E = math.log2(math.e)
SCALE2 = SCALE * LOG2E
NEG = -1e30

LANES = 128
SUBLANES = 8
VMEM_LIMIT = 48 * 1024 * 1024

TQ = 256
TK = 256
ROW_CHUNK = 256
SB_ROW_CHUNK = 256
TQ_SB = 1024
PAGES_PER_STEP = 32
ROW_VECS = 8
SSM_TILE = 512
SSM_TILES = SSM_LANES // SSM_TILE
SSM_U_TILE = SSM_WIDTH // SSM_TILES


def _cparams(sem):
    return pltpu.CompilerParams(dimension_semantics=sem, vmem_limit_bytes=VMEM_LIMIT)


def _dot(a, b):
    return jnp.dot(a, b, preferred_element_type=F32)


def _dot_nt(a, b):
    return lax.dot_general(a, b, (((1,), (1,)), ((), ())), preferred_element_type=F32)


def _split_dot(x, w):
    hi = x.astype(BF16)
    lo = (x - hi.astype(F32)).astype(BF16)
    return _dot(hi, w) + _dot(lo, w)


def _sigmoid(x):
    return 1.0 / (1.0 + jnp.exp(-x))


def _lane_tile(x, n):
    return x if n == 1 else jnp.concatenate([x] * n, axis=1)


def _mm_kernel(x_ref, w_ref, *o_refs):
    acc = _dot(x_ref[...], w_ref[...])
    for o in o_refs:
        o[...] = acc.astype(o.dtype)


def _matmul(x, w, out_dtypes, name):
    m, k = x.shape
    n = w.shape[1]
    tm = min(m, 1024)
    tn = min(n, 512)
    outs = pl.pallas_call(
        _mm_kernel,
        grid=(m // tm, n // tn),
        in_specs=[pl.BlockSpec((tm, k), lambda i, j: (i, 0)),
                  pl.BlockSpec((k, tn), lambda i, j: (0, j))],
        out_specs=[pl.BlockSpec((tm, tn), lambda i, j: (i, j)) for _ in out_dtypes],
        out_shape=[jax.ShapeDtypeStruct((m, n), d) for d in out_dtypes],
        compiler_params=_cparams(("parallel", "parallel")),
        name=name,
    )(x, w)
    return outs


def _masked_softmax(s, valid):
    s = jnp.where(valid, s, NEG)
    m = jnp.max(s, axis=1, keepdims=True)
    e = jnp.where(valid, jnp.exp(s - m), 0.0)
    return e / jnp.maximum(jnp.sum(e, axis=1, keepdims=True), 1e-30)


def _masked_softmax2(s, valid):
    e = jnp.where(valid, jnp.exp2(s - jnp.max(s, axis=1, keepdims=True)), 0.0)
    return e * (1.0 / jnp.maximum(jnp.sum(e, axis=1, keepdims=True), 1e-30))


def _cmp_attend(qg, ck, cv, bias, rows):
    s = _dot_nt(qg, ck) * SCALE + bias
    p = _masked_softmax(s, bias > 0.5 * NEG)
    o = _dot(p.astype(BF16), cv)
    imp = p[0:rows] + p[rows:2 * rows] + p[2 * rows:3 * rows] + p[3 * rows:4 * rows]
    nc = ck.shape[0]
    nb = nc // CMP_PER_SEL
    grp = (lax.broadcasted_iota(jnp.int32, (nc, nb), 0) // CMP_PER_SEL
           == lax.broadcasted_iota(jnp.int32, (nc, nb), 1))
    return o, _split_dot(imp, jnp.where(grp, 1.0, 0.0).astype(BF16))


def _select_blocks(imp, pos, k_top, groups=1):
    rows, width = imp.shape
    nb = width // groups
    lane = lax.broadcasted_iota(jnp.int32, (rows, width), 1)
    blk = lane % nb
    cur = pos // SEL_BLOCK
    score = jnp.where(blk * SEL_BLOCK <= pos, imp, -1.0)
    score = jnp.where(blk == 0, FORCE_SCORE, score)
    score = jnp.where(blk == cur, FORCE_SCORE, score)
    score = jnp.where(blk == cur - 1, FORCE_SCORE, score)
    cnt = jnp.zeros((rows, width), F32)
    for i in range(nb):
        col = score[:, i:i + 1]
        for g in range(1, groups):
            col = jnp.where(lane >= g * nb, score[:, g * nb + i:g * nb + i + 1], col)
        ge = jnp.where(col >= score, 1.0, 0.0)
        gt = jnp.where(col > score, 1.0, 0.0)
        cnt = cnt + jnp.where(blk > i, ge, gt)
    return jnp.where(cnt < k_top, 1.0, 0.0)


def _flash_init(m_ref, l_ref, acc_ref):
    m_ref[...] = jnp.full(m_ref.shape, NEG, F32)
    l_ref[...] = jnp.zeros(l_ref.shape, F32)
    acc_ref[...] = jnp.zeros(acc_ref.shape, F32)


def _flash_tile(s, v, m_ref, l_ref, acc_ref):
    m, l, acc = _flash_update(s, v, m_ref[...], l_ref[...], acc_ref[...])
    m_ref[...] = m
    l_ref[...] = l
    acc_ref[...] = acc


def _flash_update(s, v, m_prev, l_prev, acc_prev):
    m_new = jnp.maximum(m_prev, jnp.max(s, axis=1, keepdims=True))
    alpha = jnp.exp2(m_prev - m_new)
    p = jnp.exp2(s - _lane_tile(m_new, s.shape[1] // LANES))
    return (m_new, alpha * l_prev + jnp.sum(p, axis=1, keepdims=True),
            alpha * acc_prev + _dot(p.astype(BF16), v))


def _page_vec(pg, c, page):
    return pg[0, 0, pl.ds(c, page, stride=ROW_VECS), :]


def _pad_keys(x):
    pad = jnp.zeros((LANES - x.shape[0], x.shape[1]), F32)
    return jnp.concatenate([x, pad], axis=0).astype(BF16)


def _stack_heads(q_ref_slice, width):
    return jnp.concatenate([q_ref_slice[:, HEAD_DIM * r:HEAD_DIM * (r + 1)] for r in range(NSA_GROUP)], axis=0)


def _log2_sigmoid(z2):
    return jnp.minimum(z2, 0.0) - jnp.log(1.0 + jnp.exp2(-jnp.abs(z2))) * LOG2E


def _strict_upper(n):
    return jnp.where(lax.broadcasted_iota(jnp.int32, (n, n), 0) > lax.broadcasted_iota(jnp.int32, (n, n), 1),
                     1.0, 0.0).astype(BF16)


def _prompt_cmp_kernel(q_ref, k0_ref, k1_ref, v0_ref, v1_ref, w_ref, bias_ref, o_ref, sel_ref, ck_ref, cv_ref,
                       *, tq, nc):
    qt = pl.program_id(1)

    @pl.when(qt == 0)
    def _():
        for g in range(NSA_KV_HEADS):
            for kv, src, dst in ((0, (k0_ref, k1_ref)[g], ck_ref), (1, (v0_ref, v1_ref)[g], cv_ref)):
                first = jnp.zeros((nc, HEAD_DIM), F32)
                second = jnp.zeros((nc, HEAD_DIM), F32)
                for j in range(CMP_STRIDE):
                    rows = src[0, pl.ds(j, nc, stride=CMP_STRIDE), :]
                    first = first + rows * w_ref[kv, j:j + 1, :]
                    second = second + rows * w_ref[kv, CMP_STRIDE + j:CMP_STRIDE + j + 1, :]
                dst[g] = (first + pltpu.roll(second, nc - 1, 0)).astype(BF16)

    nb = nc // CMP_PER_SEL
    pool = jnp.where(lax.broadcasted_iota(jnp.int32, (nc, nb), 0) // CMP_PER_SEL
                     == lax.broadcasted_iota(jnp.int32, (nc, nb), 1), 1.0, 0.0).astype(BF16)
    chunks = [(h, off) for h in range(NSA_HEADS) for off in range(0, tq, ROW_CHUNK)]

    def scores(h, off):
        bias = bias_ref[h, off:off + ROW_CHUNK, :]
        q = q_ref[0, off:off + ROW_CHUNK, h * HEAD_DIM:(h + 1) * HEAD_DIM]
        return _dot_nt(q, ck_ref[h // NSA_GROUP]) * SCALE2 + bias, bias

    imp = {(g, off): None for g in range(NSA_KV_HEADS) for off in range(0, tq, ROW_CHUNK)}
    nxt = scores(*chunks[0])
    for c, (h, off) in enumerate(chunks):
        s, bias = nxt
        if c + 1 < len(chunks):
            nxt = scores(*chunks[c + 1])
        p = _masked_softmax2(s, bias > 0.5 * NEG)
        o_ref[0, off:off + ROW_CHUNK, h * HEAD_DIM:(h + 1) * HEAD_DIM] = _dot(p.astype(BF16), cv_ref[h // NSA_GROUP])
        key = (h // NSA_GROUP, off)
        imp[key] = p if imp[key] is None else imp[key] + p
    imp = jnp.concatenate(
        [jnp.concatenate([_split_dot(imp[(g, off)], pool) for off in range(0, tq, ROW_CHUNK)], axis=0)
         for g in range(NSA_KV_HEADS)], axis=1)
    pos = qt * tq + lax.broadcasted_iota(jnp.int32, imp.shape, 0)
    sel = _select_blocks(imp, pos, N_SEL, NSA_KV_HEADS)
    for g in range(NSA_KV_HEADS):
        sel_ref[0, g] = sel[:, g * nb:(g + 1) * nb]


def _prompt_cmp(q, kvf, w_cmp, bias_cmp, b, t):
    nc = t // CMP_STRIDE
    nb = nc // CMP_PER_SEL
    q3 = q.reshape(b, t, q.shape[-1])
    kv3 = kvf.reshape(b, t, kvf.shape[-1])
    return pl.pallas_call(
        functools.partial(_prompt_cmp_kernel, tq=TQ, nc=nc),
        grid=(b, t // TQ),
        in_specs=[pl.BlockSpec((1, TQ, NSA_WIDTH), lambda i, j: (i, j, 0)),
                  *[pl.BlockSpec((1, t, HEAD_DIM), lambda i, j, c=c: (i, 0, c)) for c in range(4)],
                  pl.BlockSpec((2, CMP_BLOCK, HEAD_DIM), lambda i, j: (0, 0, 0)),
                  pl.BlockSpec((NSA_HEADS, TQ, nc), lambda i, j: (0, j, 0))],
        out_specs=[pl.BlockSpec((1, TQ, NSA_WIDTH), lambda i, j: (i, j, 0)),
                   pl.BlockSpec((1, NSA_KV_HEADS, TQ, nb), lambda i, j: (i, 0, j, 0))],
        out_shape=[jax.ShapeDtypeStruct((b, t, NSA_WIDTH), F32),
                   jax.ShapeDtypeStruct((b, NSA_KV_HEADS, t, nb), F32)],
        scratch_shapes=[pltpu.VMEM((NSA_KV_HEADS, nc, HEAD_DIM), BF16),
                        pltpu.VMEM((NSA_KV_HEADS, nc, HEAD_DIM), BF16)],
        compiler_params=_cparams(("parallel", "arbitrary")),
        name="prompt_cmp",
    )(q3, kv3, kv3, kv3, kv3, w_cmp, bias_cmp)


def _prompt_selwin_kernel(q_ref, ks_ref, vs_ref, kw_ref, vw_ref, sel_ref, tb_ref, osel_ref, owin_ref,
                          m_ref, l_ref, acc_ref, *, tq, tk):
    i = pl.program_id(2)
    selb = sel_ref[0, 0].astype(BF16)
    nb = selb.shape[1]
    blocks_per_tile = tk // SEL_BLOCK

    def tile(k_ref, v_ref, j, n, table, extra):
        start = pl.multiple_of(j * tk, tk)
        k = k_ref[0, pl.ds(start, n * tk), :]
        v = v_ref[0, pl.ds(start, n * tk), :]
        chunks = [(r, off) for r in range(NSA_GROUP) for off in range(0, tq, ROW_CHUNK)]

        def scores(r, off):
            s = _dot_nt(q_ref[0, off:off + ROW_CHUNK, r * HEAD_DIM:(r + 1) * HEAD_DIM], k) * SCALE2
            if table is not None:
                s = s + tb_ref[table, r, off:off + ROW_CHUNK, :]
            if extra is not None:
                s = s + extra[off:off + ROW_CHUNK]
            return s

        rows = [slice(r * tq + off, r * tq + off + ROW_CHUNK) for r, off in chunks]
        state = [(m_ref[rs], l_ref[rs], acc_ref[rs]) for rs in rows]
        s_next = scores(*chunks[0])
        for c in range(len(chunks)):
            s = s_next
            if c + 1 < len(chunks):
                s_next = scores(*chunks[c + 1])
            state[c] = _flash_update(s, v, *state[c])
        for rs, (m, l, acc) in zip(rows, state):
            m_ref[rs] = m
            l_ref[rs] = l
            acc_ref[rs] = acc

    def sel_tile(j, n, table):
        erow = lax.broadcasted_iota(jnp.int32, (nb, n * tk), 0)
        ecol = lax.broadcasted_iota(jnp.int32, (nb, n * tk), 1) // SEL_BLOCK
        expand = jnp.where(erow == j * blocks_per_tile + ecol, 1.0, 0.0).astype(BF16)
        tile(ks_ref, vs_ref, j, n, table, (1.0 - _dot(selb, expand)) * NEG)

    def win_tile(j, table):
        tile(kw_ref, vw_ref, j, 1, table, None)

    def finish(o_ref):
        for r in range(NSA_GROUP):
            rs = slice(r * tq, (r + 1) * tq)
            o_ref[0, :, r * HEAD_DIM:(r + 1) * HEAD_DIM] = acc_ref[rs] / l_ref[rs]

    _flash_init(m_ref, l_ref, acc_ref)

    n_far = jnp.maximum(i - 1, 0)
    quads = n_far // 4

    def far_body(jj, carry):
        sel_tile(4 * jj, 4, None)
        return carry

    lax.fori_loop(0, quads, far_body, 0)

    @pl.when(n_far % 4 >= 2)
    def _():
        sel_tile(4 * quads, 2, None)

    @pl.when(n_far % 2 == 1)
    def _():
        sel_tile(i - 2, 1, None)

    @pl.when(i >= 1)
    def _():
        sel_tile(i - 1, 1, 1)

    sel_tile(i, 1, 0)
    finish(osel_ref)

    _flash_init(m_ref, l_ref, acc_ref)
    win_tile(i, 0)

    @pl.when(i >= 1)
    def _():
        win_tile(i - 1, 1)

    @pl.when(i >= 2)
    def _():
        win_tile(i - 2, 2)

    finish(owin_ref)


def _prompt_selwin(q, kvb, sel, tables, b, t):
    nb = sel.shape[-1]
    q3 = q.reshape(b, t, q.shape[-1])
    kv3 = kvb.reshape(b, t, kvb.shape[-1])
    gw = NSA_GROUP * HEAD_DIM
    kcol = lambda base: pl.BlockSpec((1, t, HEAD_DIM), lambda i, g, j: (i, 0, base + g))
    return pl.pallas_call(
        functools.partial(_prompt_selwin_kernel, tq=TQ, tk=TK),
        grid=(b, NSA_KV_HEADS, t // TQ),
        in_specs=[pl.BlockSpec((1, TQ, gw), lambda i, g, j: (i, j, g)),
                  kcol(4), kcol(6), kcol(8), kcol(10),
                  pl.BlockSpec((1, 1, TQ, nb), lambda i, g, j: (i, g, j, 0)),
                  pl.BlockSpec((3, NSA_GROUP, TQ, TK), lambda i, g, j: (0, g, 0, 0))],
        out_specs=[pl.BlockSpec((1, TQ, gw), lambda i, g, j: (i, j, g)),
                   pl.BlockSpec((1, TQ, gw), lambda i, g, j: (i, j, g))],
        out_shape=[jax.ShapeDtypeStruct((b, t, NSA_WIDTH), F32),
                   jax.ShapeDtypeStruct((b, t, NSA_WIDTH), F32)],
        scratch_shapes=[pltpu.VMEM((NSA_GROUP * TQ, HEAD_DIM), F32),
                        pltpu.VMEM((NSA_GROUP * TQ, HEAD_DIM), F32),
                        pltpu.VMEM((NSA_GROUP * TQ, HEAD_DIM), F32)],
        compiler_params=_cparams(("parallel", "parallel", "parallel")),
        name="prompt_selwin",
    )(q3, kv3, kv3, kv3, kv3, sel, tables)


def _sb_weights(z, carry, upper, valid):
    seg = upper.shape[0]
    ls = _log2_sigmoid(z)
    l1m = ls - z
    if valid is not None:
        l1m = jnp.where(valid, l1m, 0.0)
    cums = []
    for sg in reversed(range(z.shape[1] // seg)):
        part = l1m[:, sg * seg:(sg + 1) * seg]
        cums.append(_split_dot(part, upper) + carry)
        carry = carry + jnp.sum(part, axis=1, keepdims=True)
    cum = cums[0] if len(cums) == 1 else jnp.concatenate(cums[::-1], axis=1)
    a = jnp.exp2(ls + cum)
    if valid is not None:
        a = jnp.where(valid, a, 0.0)
    return a, carry


def _prompt_sb_kernel(q_ref, k_ref, v_ref, o_ref, acc_ref, carry_ref, *, tq, tk):
    i = pl.program_id(2)
    upper = _strict_upper(tk)
    acc_ref[...] = jnp.zeros(acc_ref.shape, F32)
    carry_ref[...] = jnp.zeros(carry_ref.shape, F32)

    def run(block, diag):
        start = pl.multiple_of(block * tq, tq)
        k = k_ref[0, pl.ds(start, tq), :]
        v = v_ref[0, pl.ds(start, tq), :]
        chunks = [slice(off, off + SB_ROW_CHUNK) for off in range(0, tq, SB_ROW_CHUNK)]
        accs = [acc_ref[rs, :] for rs in chunks]
        spans = [rs.stop if diag else tq for rs in chunks]
        zs = [_dot_nt(q_ref[0, rs, :], k[:n]) * SCALE2 for rs, n in zip(chunks, spans)]
        valids = [None] * len(chunks)
        if diag:
            valids = [(lax.broadcasted_iota(jnp.int32, (SB_ROW_CHUNK, n), 1)
                       < lax.broadcasted_iota(jnp.int32, (SB_ROW_CHUNK, n), 0) + rs.start)
                      for rs, n in zip(chunks, spans)]
        weights = [_sb_weights(z, carry_ref[rs, 0:1], upper, valid) for z, rs, valid in zip(zs, chunks, valids)]
        outs = [_dot(a.astype(BF16), v[:n]) for (a, _), n in zip(weights, spans)]
        for rs, acc, o, (_, carry) in zip(chunks, accs, outs, weights):
            acc_ref[rs, :] = acc + o
            carry_ref[rs, :] = jnp.broadcast_to(carry, (SB_ROW_CHUNK, HEAD_DIM))

    run(i, True)

    def body(jj, c):
        run(i - 1 - jj, False)
        return c

    lax.fori_loop(0, i, body, 0)
    o_ref[0] = acc_ref[...]


def _prompt_sb(q, kvb, b, t):
    q3 = q.reshape(b, t, q.shape[-1])
    kv3 = kvb.reshape(b, t, kvb.shape[-1])
    qbase = NSA_WIDTH // HEAD_DIM
    return pl.pallas_call(
        functools.partial(_prompt_sb_kernel, tq=TQ_SB, tk=TK),
        grid=(b, SB_HEADS, t // TQ_SB),
        in_specs=[pl.BlockSpec((1, TQ_SB, HEAD_DIM), lambda i, h, j: (i, j, qbase + h)),
                  pl.BlockSpec((1, t, HEAD_DIM), lambda i, h, j: (i, 0, 12 + h)),
                  pl.BlockSpec((1, t, HEAD_DIM), lambda i, h, j: (i, 0, 16 + h))],
        out_specs=pl.BlockSpec((1, TQ_SB, HEAD_DIM), lambda i, h, j: (i, j, h)),
        out_shape=jax.ShapeDtypeStruct((b, t, SB_WIDTH), F32),
        scratch_shapes=[pltpu.VMEM((TQ_SB, HEAD_DIM), F32), pltpu.VMEM((TQ_SB, HEAD_DIM), F32)],
        compiler_params=_cparams(("parallel", "parallel", "parallel")),
        name="prompt_sb",
    )(q3, kv3, kv3)


def _ssm_kernel(u_ref, h0_ref, wre_ref, wim_ref, cre_ref, cim_ref, d_ref, pre_ref, pim_ref,
                y_ref, hl_ref, hr_ref, hi_ref, *, t, tc):
    p8r = pre_ref[...]
    p8i = pim_ref[...]
    row = lax.broadcasted_iota(jnp.int32, (tc, SSM_TILE), 0) % SUBLANES

    def chunk(c, carry):
        cr, ci = carry
        start = pl.multiple_of(c * tc, tc)
        u = u_ref[0, pl.ds(start, tc), :]
        ub = u.astype(BF16)
        xr = _dot(ub, wre_ref[0])
        xi = _dot(ub, wim_ref[0])
        for sh in (1, 2, 4):
            ar = p8r[sh - 1:sh, :]
            ai = p8i[sh - 1:sh, :]
            sr = jnp.where(row >= sh, pltpu.roll(xr, sh, 0), 0.0)
            si = jnp.where(row >= sh, pltpu.roll(xi, sh, 0), 0.0)
            xr, xi = xr + ar * sr - ai * si, xi + ar * si + ai * sr
        for g in range(tc // SUBLANES):
            lo, hi = g * SUBLANES, (g + 1) * SUBLANES
            br = xr[lo:hi] + p8r * cr - p8i * ci
            bi = xi[lo:hi] + p8r * ci + p8i * cr
            hr_ref[lo:hi, :] = br
            hi_ref[lo:hi, :] = bi
            cr, ci = br[SUBLANES - 1:SUBLANES], bi[SUBLANES - 1:SUBLANES]
        y = (_dot(hr_ref[...].astype(BF16), cre_ref[0]) - _dot(hi_ref[...].astype(BF16), cim_ref[0])
             + d_ref[0] * u)
        y_ref[0, pl.ds(start, tc), :] = y
        return cr, ci

    cr, ci = lax.fori_loop(0, t // tc, chunk, (h0_ref[0, 0:1, :], h0_ref[0, 1:2, :]))
    hl_ref[0, 0:1, :] = cr
    hl_ref[0, 1:2, :] = ci


def _ssm(zc, h0, sw, b, t):
    wre, wim, cre, cim, dsk, pre, pim = sw
    tc = min(t, 256)
    u3 = zc.reshape(b, t, zc.shape[-1])
    ubase = (NSA_WIDTH + SB_WIDTH) // SSM_U_TILE
    wspec = lambda shp: pl.BlockSpec((1,) + shp, lambda i, k: (k, 0, 0))
    return pl.pallas_call(
        functools.partial(_ssm_kernel, t=t, tc=tc),
        grid=(b, SSM_TILES),
        in_specs=[pl.BlockSpec((1, t, SSM_U_TILE), lambda i, k: (i, 0, ubase + k)),
                  pl.BlockSpec((1, 2, SSM_TILE), lambda i, k: (i, 0, k)),
                  wspec((SSM_U_TILE, SSM_TILE)), wspec((SSM_U_TILE, SSM_TILE)),
                  wspec((SSM_TILE, SSM_U_TILE)), wspec((SSM_TILE, SSM_U_TILE)),
                  wspec((1, SSM_U_TILE)),
                  pl.BlockSpec((SUBLANES, SSM_TILE), lambda i, k: (0, k)),
                  pl.BlockSpec((SUBLANES, SSM_TILE), lambda i, k: (0, k))],
        out_specs=[pl.BlockSpec((1, t, SSM_U_TILE), lambda i, k: (i, 0, k)),
                   pl.BlockSpec((1, 2, SSM_TILE), lambda i, k: (i, 0, k))],
        out_shape=[jax.ShapeDtypeStruct((b, t, SSM_WIDTH), F32),
                   jax.ShapeDtypeStruct((b, 2, SSM_LANES), F32)],
        scratch_shapes=[pltpu.VMEM((tc, SSM_TILE), F32), pltpu.VMEM((tc, SSM_TILE), F32)],
        compiler_params=_cparams(("parallel", "parallel")),
        name="ssm_scan",
    )(u3, h0, wre, wim, cre, cim, dsk, pre, pim)


def _rms(h, gain):
    return h * lax.rsqrt(jnp.mean(h * h, axis=1, keepdims=True) + EPS) * gain


def _silu(z):
    return z * _sigmoid(z)


def _mix_kernel(ocmp_ref, osel_ref, owin_ref, gate_ref, ob_ref, yc_ref, za_ref, zb_ref, zc_ref, x_ref,
                wglu_ref, wout_ref, gain_ref, lng_ref, lnb_ref, y_ref, yb_ref, mixed_ref, *, alpha):
    gates = _sigmoid(gate_ref[...])
    for h in range(NSA_HEADS):
        sl = slice(h * HEAD_DIM, (h + 1) * HEAD_DIM)
        oa = (gates[:, 3 * h:3 * h + 1] * ocmp_ref[:, sl] + gates[:, 3 * h + 1:3 * h + 2] * osel_ref[:, sl]
              + gates[:, 3 * h + 2:3 * h + 3] * owin_ref[:, sl])
        mixed_ref[:, sl] = oa * _silu(za_ref[:, sl])
    mixed_ref[:, 0:NSA_WIDTH] = _rms(mixed_ref[:, 0:NSA_WIDTH], gain_ref[:, 0:NSA_WIDTH])
    b0, c0 = NSA_WIDTH, NSA_WIDTH + SB_WIDTH
    mixed_ref[:, b0:c0] = _rms(ob_ref[...] * _silu(zb_ref[...]), gain_ref[:, b0:c0])
    glu = _dot(yc_ref[...].astype(BF16), wglu_ref[...])
    yc = glu[:, 0:SSM_WIDTH] * _sigmoid(glu[:, SSM_WIDTH:2 * SSM_WIDTH])
    mixed_ref[:, c0:c0 + SSM_WIDTH] = _rms(yc * _silu(zc_ref[...]), gain_ref[:, c0:c0 + SSM_WIDTH])
    h = alpha * x_ref[...] + _dot(mixed_ref[...].astype(BF16), wout_ref[...])
    mu = jnp.mean(h, axis=1, keepdims=True)
    hc = h - mu
    var = jnp.mean(hc * hc, axis=1, keepdims=True)
    y = hc * lax.rsqrt(var + EPS) * lng_ref[...] + lnb_ref[...]
    y_ref[...] = y
    yb_ref[...] = y.astype(BF16)


def _mix_out(ocmp, osel, owin, gate, ob, yc, zc, x, wglu, wout, gain, lng, lnb, alpha):
    m, d = x.shape
    tm = min(m, 256)
    row = lambda w, c=0: pl.BlockSpec((tm, w), lambda i, c=c: (i, c))
    full = lambda a: pl.BlockSpec(a.shape, lambda i: (0,) * a.ndim)
    return pl.pallas_call(
        functools.partial(_mix_kernel, alpha=alpha),
        grid=(m // tm,),
        in_specs=[row(NSA_WIDTH), row(NSA_WIDTH), row(NSA_WIDTH), row(LANES), row(SB_WIDTH), row(SSM_WIDTH),
                  row(NSA_WIDTH, 0), row(SB_WIDTH, NSA_WIDTH // SB_WIDTH),
                  row(SSM_WIDTH, (NSA_WIDTH + SB_WIDTH + SSM_WIDTH) // SSM_WIDTH), row(d),
                  full(wglu), full(wout), full(gain), full(lng), full(lnb)],
        out_specs=[row(d), row(d)],
        out_shape=[jax.ShapeDtypeStruct((m, d), F32), jax.ShapeDtypeStruct((m, d), BF16)],
        scratch_shapes=[pltpu.VMEM((tm, d), F32)],
        compiler_params=_cparams(("parallel",)),
        name="mix_out",
    )(ocmp.reshape(m, -1), osel.reshape(m, -1), owin.reshape(m, -1), gate, ob.reshape(m, -1),
      yc.reshape(m, -1), zc, zc, zc, x, wglu, wout, gain, lng, lnb)


def _sample_compress_kernel(pt_ref, *refs):
    pages = refs[:PAGES_PER_STEP]
    w1_ref, w2_ref, first_ref, second_ref = refs[PAGES_PER_STEP:]
    page = pages[0].shape[2] // ROW_VECS
    per_page = page // CMP_STRIDE
    half = w1_ref.shape[1]
    pool = jnp.where(lax.broadcasted_iota(jnp.int32, (per_page, page), 1) // CMP_STRIDE
                     == lax.broadcasted_iota(jnp.int32, (per_page, page), 0), 1.0, 0.0).astype(BF16)
    for i, pg in enumerate(pages):
        rows = jnp.concatenate([_page_vec(pg, c, page) for c in range(2 * NSA_KV_HEADS)], axis=1)
        prod = jnp.concatenate([rows * w1_ref[...], rows * w2_ref[...]], axis=1)
        hi = prod.astype(BF16)
        lo = (prod - hi.astype(F32)).astype(BF16)
        sums = _dot(pool, hi) + _dot(pool, lo)
        first_ref[0, i * per_page:(i + 1) * per_page, :] = sums[:, 0:half]
        second_ref[0, i * per_page:(i + 1) * per_page, :] = sums[:, half:2 * half]


def _sample_compress(cache, layer, page_table, w_cmp):
    db, n_pages = page_table.shape
    page = cache.shape[2] // ROW_VECS
    per_page = page // CMP_STRIDE
    steps = n_pages // PAGES_PER_STEP
    nc = n_pages * per_page
    half = 4 * HEAD_DIM

    def pspec(i):
        return pl.BlockSpec((1, 1, page * ROW_VECS, HEAD_DIM),
                            lambda b, s, pt, i=i: (layer, pt[b, s * PAGES_PER_STEP + i], 0, 0))

    rows = PAGES_PER_STEP * per_page
    tiled = lambda w: jnp.concatenate([jnp.tile(w[kv], (per_page, 1)) for kv in (0, 0, 1, 1)], axis=1)
    w1, w2 = tiled(w_cmp[:, :CMP_STRIDE]), tiled(w_cmp[:, CMP_STRIDE:])
    return pl.pallas_call(
        _sample_compress_kernel,
        grid_spec=pltpu.PrefetchScalarGridSpec(
            num_scalar_prefetch=1,
            grid=(db, steps),
            in_specs=[pspec(i) for i in range(PAGES_PER_STEP)]
                     + [pl.BlockSpec((page, half), lambda b, s, pt: (0, 0)),
                        pl.BlockSpec((page, half), lambda b, s, pt: (0, 0))],
            out_specs=[pl.BlockSpec((1, rows, half), lambda b, s, pt: (b, s, 0)),
                       pl.BlockSpec((1, rows, half), lambda b, s, pt: (b, s, 0))]),
        out_shape=[jax.ShapeDtypeStruct((db, nc, half), F32), jax.ShapeDtypeStruct((db, nc, half), F32)],
        compiler_params=_cparams(("parallel", "parallel")),
        name="sample_compress",
    )(page_table, *([cache] * PAGES_PER_STEP), w1, w2)


def _sample_cmpwin_kernel(q_ref, first_ref, second_ref, bias_ref, win_ref, wbias_ref,
                          ocmp_ref, sel_ref, owin_ref, *, ts, nc, pos0):
    comp = first_ref[0] + pltpu.roll(second_ref[0], nc - 1, 0)
    nb = nc // CMP_PER_SEL
    pos = pos0 + lax.broadcasted_iota(jnp.int32, (ts, nb), 0)
    for g in range(NSA_KV_HEADS):
        qg = _stack_heads(q_ref[0, :, pl.ds(g * NSA_GROUP * HEAD_DIM, NSA_GROUP * HEAD_DIM)], HEAD_DIM)
        ck = comp[:, g * HEAD_DIM:(g + 1) * HEAD_DIM].astype(BF16)
        cv = comp[:, (2 + g) * HEAD_DIM:(3 + g) * HEAD_DIM].astype(BF16)
        bias = bias_ref[NSA_GROUP * g:NSA_GROUP * (g + 1)].reshape(NSA_GROUP * ts, nc)
        o, imp = _cmp_attend(qg, ck, cv, bias, ts)
        sel_ref[0, g] = _select_blocks(imp, pos, N_SEL - 1)
        kw = win_ref[0, :, g * HEAD_DIM:(g + 1) * HEAD_DIM].astype(BF16)
        vw = win_ref[0, :, (2 + g) * HEAD_DIM:(3 + g) * HEAD_DIM].astype(BF16)
        wb = wbias_ref[NSA_GROUP * g:NSA_GROUP * (g + 1)].reshape(NSA_GROUP * ts, kw.shape[0])
        sw = _dot_nt(qg, kw) * SCALE + wb
        ow = _dot(_masked_softmax(sw, wb > 0.5 * NEG).astype(BF16), vw)
        for r in range(NSA_GROUP):
            h = NSA_GROUP * g + r
            ocmp_ref[0, :, h * HEAD_DIM:(h + 1) * HEAD_DIM] = o[r * ts:(r + 1) * ts]
            owin_ref[0, :, h * HEAD_DIM:(h + 1) * HEAD_DIM] = ow[r * ts:(r + 1) * ts]


def _sample_cmpwin(q, first, second, bias_cmp, win_all, bias_win, db, ts, past_len):
    nc = first.shape[1]
    nb = nc // CMP_PER_SEL
    nw = win_all.shape[1]
    q3 = q.reshape(db, ts, q.shape[-1])
    return pl.pallas_call(
        functools.partial(_sample_cmpwin_kernel, ts=ts, nc=nc, pos0=past_len),
        grid=(db,),
        in_specs=[pl.BlockSpec((1, ts, NSA_WIDTH), lambda b: (b, 0, 0)),
                  pl.BlockSpec((1, nc, 4 * HEAD_DIM), lambda b: (b, 0, 0)),
                  pl.BlockSpec((1, nc, 4 * HEAD_DIM), lambda b: (b, 0, 0)),
                  pl.BlockSpec((NSA_HEADS, ts, nc), lambda b: (0, 0, 0)),
                  pl.BlockSpec((1, nw, 4 * HEAD_DIM), lambda b: (b, 0, 0)),
                  pl.BlockSpec((NSA_HEADS, ts, nw), lambda b: (0, 0, 0))],
        out_specs=[pl.BlockSpec((1, ts, NSA_WIDTH), lambda b: (b, 0, 0)),
                   pl.BlockSpec((1, NSA_KV_HEADS, ts, nb), lambda b: (b, 0, 0, 0)),
                   pl.BlockSpec((1, ts, NSA_WIDTH), lambda b: (b, 0, 0))],
        out_shape=[jax.ShapeDtypeStruct((db, ts, NSA_WIDTH), F32),
                   jax.ShapeDtypeStruct((db, NSA_KV_HEADS, ts, nb), F32),
                   jax.ShapeDtypeStruct((db, ts, NSA_WIDTH), F32)],
        compiler_params=_cparams(("parallel",)),
        name="sample_cmpwin",
    )(q3, first, second, bias_cmp, win_all, bias_win)


def _sample_sel_kernel(pt_ref, *refs, ts, page):
    pages = refs[:PAGES_PER_STEP]
    q_ref, sel_ref, new_ref, bias_ref, nbias_ref, o_ref, m_ref, l_ref, acc_ref = refs[PAGES_PER_STEP:]
    s_idx = pl.program_id(1)
    last = pl.num_programs(1) - 1
    rows = NSA_GROUP * ts
    nkeys = PAGES_PER_STEP * page
    blocks = nkeys // SEL_BLOCK

    @pl.when(s_idx == 0)
    def _():
        _flash_init(m_ref, l_ref, acc_ref)

    expand = jnp.where(lax.broadcasted_iota(jnp.int32, (blocks, nkeys), 0)
                       == lax.broadcasted_iota(jnp.int32, (blocks, nkeys), 1) // SEL_BLOCK, 1.0, 0.0).astype(BF16)
    for g in range(NSA_KV_HEADS):
        qg = _stack_heads(q_ref[0, :, pl.ds(g * NSA_GROUP * HEAD_DIM, NSA_GROUP * HEAD_DIM)], HEAD_DIM)
        k = jnp.concatenate([_page_vec(pg, 4 + g, page) for pg in pages], axis=0).astype(BF16)
        v = jnp.concatenate([_page_vec(pg, 6 + g, page) for pg in pages], axis=0).astype(BF16)
        bias = bias_ref[jnp.where(s_idx == last, 1, 0), NSA_GROUP * g:NSA_GROUP * (g + 1)].reshape(rows, nkeys)
        s = _dot_nt(qg, k) * SCALE2 + bias
        chosen = _dot(sel_ref[0, g, 0].astype(BF16), expand) > 0.5
        s = jnp.where(chosen[None], s.reshape(NSA_GROUP, ts, nkeys), NEG).reshape(rows, nkeys)
        sl = slice(g * rows, (g + 1) * rows)
        _flash_tile(s, v, m_ref.at[sl], l_ref.at[sl], acc_ref.at[sl])

        @pl.when(s_idx == last)
        def _():
            kn = _pad_keys(new_ref[0, :, g * HEAD_DIM:(g + 1) * HEAD_DIM])
            vn = _pad_keys(new_ref[0, :, (2 + g) * HEAD_DIM:(3 + g) * HEAD_DIM])
            nbias = nbias_ref[NSA_GROUP * g:NSA_GROUP * (g + 1)].reshape(rows, LANES)
            _flash_tile(_dot_nt(qg, kn) * SCALE2 + nbias, vn, m_ref.at[sl], l_ref.at[sl], acc_ref.at[sl])
            o = acc_ref[sl] / l_ref[sl]
            for r in range(NSA_GROUP):
                h = NSA_GROUP * g + r
                o_ref[0, :, h * HEAD_DIM:(h + 1) * HEAD_DIM] = o[r * ts:(r + 1) * ts]


def _sample_sel(cache, layer, page_table, q, sel_steps, kv_new, bias_steps, bias_new, db, ts):
    n_pages = page_table.shape[1]
    page = cache.shape[2] // ROW_VECS
    steps = n_pages // PAGES_PER_STEP
    half = 4 * HEAD_DIM
    nkeys = PAGES_PER_STEP * page
    blocks = nkeys // SEL_BLOCK
    q3 = q.reshape(db, ts, q.shape[-1])
    new3 = kv_new.reshape(db, ts, kv_new.shape[-1])

    def pspec(i):
        return pl.BlockSpec((1, 1, page * ROW_VECS, HEAD_DIM),
                            lambda b, s, pt, i=i: (layer, pt[b, s * PAGES_PER_STEP + i], 0, 0))

    return pl.pallas_call(
        functools.partial(_sample_sel_kernel, ts=ts, page=page),
        grid_spec=pltpu.PrefetchScalarGridSpec(
            num_scalar_prefetch=1,
            grid=(db, steps),
            in_specs=[pspec(i) for i in range(PAGES_PER_STEP)]
                     + [pl.BlockSpec((1, ts, NSA_WIDTH), lambda b, s, pt: (b, 0, 0)),
                        pl.BlockSpec((1, NSA_KV_HEADS, 1, ts, blocks), lambda b, s, pt: (b, 0, s, 0, 0)),
                        pl.BlockSpec((1, ts, half), lambda b, s, pt: (b, 0, 1)),
                        pl.BlockSpec((2, NSA_HEADS, ts, nkeys), lambda b, s, pt: (0, 0, 0, 0)),
                        pl.BlockSpec((NSA_HEADS, ts, LANES), lambda b, s, pt: (0, 0, 0))],
            out_specs=pl.BlockSpec((1, ts, NSA_WIDTH), lambda b, s, pt: (b, 0, 0)),
            scratch_shapes=[pltpu.VMEM((NSA_HEADS * ts, HEAD_DIM), F32),
                            pltpu.VMEM((NSA_HEADS * ts, HEAD_DIM), F32),
                            pltpu.VMEM((NSA_HEADS * ts, HEAD_DIM), F32)]),
        out_shape=jax.ShapeDtypeStruct((db, ts, NSA_WIDTH), F32),
        compiler_params=_cparams(("parallel", "arbitrary")),
        name="sample_sel",
    )(page_table, *([cache] * PAGES_PER_STEP), q3, sel_steps, new3, bias_steps, bias_new)


def _sample_sb_kernel(pt_ref, *refs, ts, page, seg):
    pages = refs[:PAGES_PER_STEP]
    q_ref, knew_ref, vnew_ref, o_ref, carry_ref, acc_ref = refs[PAGES_PER_STEP:]
    s_idx = pl.program_id(1)
    last = pl.num_programs(1) - 1
    upper = _strict_upper(seg)

    rows = SB_HEADS * ts
    heads = range(SB_HEADS)
    hs = lambda h: slice(h * HEAD_DIM, (h + 1) * HEAD_DIM)

    def attend(ks, vs, carry, up, valid):
        z = jnp.concatenate([_dot_nt(q_ref[0, :, hs(h)], ks[h]) for h in heads], axis=0) * SCALE2
        a, carry = _sb_weights(z, carry, up, valid)
        return jnp.concatenate([_dot(a[h * ts:(h + 1) * ts].astype(BF16), vs[h]) for h in heads], axis=0), carry

    @pl.when(s_idx == 0)
    def _():
        kn = [_pad_keys(knew_ref[0, :, hs(h)]) for h in heads]
        vn = [_pad_keys(vnew_ref[0, :, hs(h)]) for h in heads]
        valid = (lax.broadcasted_iota(jnp.int32, (rows, LANES), 1)
                 < lax.broadcasted_iota(jnp.int32, (rows, LANES), 0) % ts)
        o, carry = attend(kn, vn, jnp.zeros((rows, 1), F32), _strict_upper(LANES), valid)
        acc_ref[...] = o
        carry_ref[...] = jnp.broadcast_to(carry, (rows, HEAD_DIM))

    ks = [jnp.concatenate([_page_vec(pg, h, page) for pg in pages], axis=0).astype(BF16) for h in heads]
    vs = [jnp.concatenate([_page_vec(pg, SB_HEADS + h, page) for pg in pages], axis=0).astype(BF16) for h in heads]
    o, carry = attend(ks, vs, carry_ref[:, 0:1], upper, None)
    acc_ref[...] = acc_ref[...] + o
    carry_ref[...] = jnp.broadcast_to(carry, (rows, HEAD_DIM))

    @pl.when(s_idx == last)
    def _():
        for h in heads:
            o_ref[0, :, hs(h)] = acc_ref[h * ts:(h + 1) * ts, :]


def _sample_sb(cache, layer, page_table, q, kv_new, db, ts):
    n_pages = page_table.shape[1]
    page = cache.shape[2] // ROW_VECS
    steps = n_pages // PAGES_PER_STEP
    q3 = q.reshape(db, ts, q.shape[-1])
    new3 = kv_new.reshape(db, ts, kv_new.shape[-1])

    def pspec(i):
        return pl.BlockSpec((1, 1, page * ROW_VECS, HEAD_DIM),
                            lambda b, s, pt, i=i: (layer, pt[b, (steps - 1 - s) * PAGES_PER_STEP + i], 0, 0))

    return pl.pallas_call(
        functools.partial(_sample_sb_kernel, ts=ts, page=page, seg=TK),
        grid_spec=pltpu.PrefetchScalarGridSpec(
            num_scalar_prefetch=1,
            grid=(db, steps),
            in_specs=[pspec(i) for i in range(PAGES_PER_STEP)]
                     + [pl.BlockSpec((1, ts, SB_WIDTH), lambda b, s, pt: (b, 0, NSA_WIDTH // SB_WIDTH)),
                        pl.BlockSpec((1, ts, SB_WIDTH), lambda b, s, pt: (b, 0, 3)),
                        pl.BlockSpec((1, ts, SB_WIDTH), lambda b, s, pt: (b, 0, 4))],
            out_specs=pl.BlockSpec((1, ts, SB_WIDTH), lambda b, s, pt: (b, 0, 0)),
            scratch_shapes=[pltpu.VMEM((SB_HEADS * ts, HEAD_DIM), F32),
                            pltpu.VMEM((SB_HEADS * ts, HEAD_DIM), F32)]),
        out_shape=jax.ShapeDtypeStruct((db, ts, SB_WIDTH), F32),
        compiler_params=_cparams(("parallel", "arbitrary")),
        name="sample_sb",
    )(page_table, *([cache] * PAGES_PER_STEP), q3, new3, new3)


def _t5_bucket(dist):
    n = jnp.maximum(dist, 0)
    exact = NUM_BUCKETS // 2
    nf = jnp.maximum(n, 1).astype(F32)
    large = exact + (jnp.log(nf / exact) / math.log(MAX_DISTANCE / exact) * (NUM_BUCKETS - exact)).astype(jnp.int32)
    return jnp.where(n < exact, n, jnp.minimum(large, NUM_BUCKETS - 1))


def _bias(rel_bias, delta, valid):
    bucket = _t5_bucket(delta)[None]
    tab = rel_bias.astype(F32).T.reshape((rel_bias.shape[1], NUM_BUCKETS) + (1,) * delta.ndim)
    b = jnp.zeros((rel_bias.shape[1],) + delta.shape, F32)
    for k in range(NUM_BUCKETS):
        b = jnp.where(bucket == k, tab[:, k], b)
    return jnp.where(valid[None], b, NEG)


def _prompt_tables(rel_bias, t):
    tt = jnp.arange(TQ)[:, None]
    ss = jnp.arange(TK)[None, :]
    d = tt - ss
    assert 2 * TK - TQ >= MAX_DISTANCE
    far = _bias(rel_bias, d + 2 * TK, d > -2 * TK)
    tables = jnp.stack([
        _bias(rel_bias, d, d >= 0) - far,
        _bias(rel_bias, d + TK, d > -TK) - far,
        jnp.where(d + 2 * TK < WINDOW, 0.0, NEG) + 0.0 * far,
    ])
    nc = t // CMP_STRIDE
    dc = jnp.arange(t)[:, None] - (jnp.arange(nc)[None, :] * CMP_STRIDE + CMP_BLOCK - 1)
    return tables * LOG2E, _bias(rel_bias, dc, dc >= 0) * LOG2E


def _sample_tables(rel_bias, ts, past_len, page, nw, nw_pad):
    qpos = past_len + jnp.arange(ts)[:, None]
    nc = past_len // CMP_STRIDE
    dc = qpos - (jnp.arange(nc)[None, :] * CMP_STRIDE + CMP_BLOCK - 1)
    bias_cmp = _bias(rel_bias, dc, dc >= 0)
    widx = jnp.arange(nw_pad)[None, :]
    dw = qpos - (past_len - nw + widx)
    bias_win = _bias(rel_bias, dw, (dw >= 0) & (dw < WINDOW) & (widx < nw + ts))
    nkeys = PAGES_PER_STEP * page
    d_last = qpos - (past_len - nkeys + jnp.arange(nkeys)[None, :])
    d_far = d_last + nkeys
    bias_steps = jnp.stack([_bias(rel_bias, d_far, d_far >= 0), _bias(rel_bias, d_last, d_last >= 0)])
    nidx = jnp.arange(LANES)[None, :]
    dn = jnp.arange(ts)[:, None] - nidx
    return bias_cmp, bias_win, bias_steps * LOG2E, _bias(rel_bias, dn, (dn >= 0) & (nidx < ts)) * LOG2E


def _ssm_weights(lam_re, lam_im, log_dt, b_re, b_im, c_re, c_im, d_skip):
    lam_re, lam_im = lam_re.astype(F32), lam_im.astype(F32)
    dt = jnp.exp(log_dt.astype(F32))[:, None]
    mag = jnp.exp(lam_re * dt)
    a_re, a_im = mag * jnp.cos(lam_im * dt), mag * jnp.sin(lam_im * dt)
    den = lam_re * lam_re + lam_im * lam_im
    f_re = ((a_re - 1.0) * lam_re + a_im * lam_im) / den
    f_im = (a_im * lam_re - (a_re - 1.0) * lam_im) / den
    b_re, b_im = b_re.astype(F32), b_im.astype(F32)
    bb_re = f_re[..., None] * b_re - f_im[..., None] * b_im
    bb_im = f_re[..., None] * b_im + f_im[..., None] * b_re
    gpt = SSM_GROUPS // SSM_TILES
    eye = jnp.eye(gpt, dtype=F32)

    def in_mat(bb):
        return jnp.einsum('kgpn,gh->kgnhp', bb.reshape(SSM_TILES, gpt, SSM_STATE, SSM_CH), eye).reshape(
            SSM_TILES, SSM_U_TILE, SSM_TILE).astype(BF16)

    def out_mat(c):
        return jnp.einsum('kgnp,gh->khpgn', c.astype(F32).reshape(SSM_TILES, gpt, SSM_CH, SSM_STATE), eye).reshape(
            SSM_TILES, SSM_TILE, SSM_U_TILE).astype(BF16)

    pr, pi = a_re.reshape(1, SSM_LANES), a_im.reshape(1, SSM_LANES)
    while pr.shape[0] < SUBLANES:
        tr, ti = pr[-1:], pi[-1:]
        pr, pi = (jnp.concatenate([pr, pr * tr - pi * ti]), jnp.concatenate([pi, pr * ti + pi * tr]))
    return (in_mat(bb_re), in_mat(bb_im), out_mat(c_re), out_mat(c_im),
            d_skip.astype(F32).reshape(SSM_TILES, 1, SSM_U_TILE), pr, pi)


def _split_w_in(w_in):
    sizes = (NSA_WIDTH, 6 * NSA_KV_HEADS * HEAD_DIM, 3 * NSA_HEADS, NSA_WIDTH,
             SB_WIDTH, SB_WIDTH, SB_WIDTH, SB_WIDTH, SSM_WIDTH, SSM_WIDTH)
    offs = [0]
    for s in sizes:
        offs.append(offs[-1] + s)
    q_a, kv_a, g_a, z_a, q_b, k_b, v_b, z_b, u_c, z_c = (w_in[..., offs[i]:offs[i + 1]] for i in range(10))
    cat = lambda xs: jnp.concatenate(xs, axis=-1).astype(BF16)
    pad = jnp.zeros(g_a.shape[:-1] + (LANES - g_a.shape[-1],), g_a.dtype)
    return cat([q_a, q_b]), cat([kv_a, k_b, v_b]), cat([z_a, z_b, u_c, z_c]), cat([g_a, pad])


KV_TILE = 4 * HEAD_DIM


def _kv_proj_kernel(*refs, tm):
    x_ref, w_ref = refs[:2]
    kvf_ref, kvb_ref, nsa_ref, sb_ref = refs[-4:]
    j = pl.program_id(1)
    acc = _dot(x_ref[...], w_ref[...])
    kvf_ref[...] = acc
    kvb_ref[...] = acc.astype(BF16)

    def scatter(dst, base):
        for c in range(KV_TILE // HEAD_DIM):
            dst[0, pl.ds(base + c, tm, stride=ROW_VECS), :] = acc[:, c * HEAD_DIM:(c + 1) * HEAD_DIM]

    for step, dst, base in ((0, nsa_ref, 0), (1, nsa_ref, 4), (3, sb_ref, 0), (4, sb_ref, 4)):
        pl.when(j == step)(functools.partial(scatter, dst, base))


def _kv_proj(x, w, layer, depth, stacked):
    m, k = x.shape
    n = w.shape[1]
    tm = min(m, 1024)
    assert n == 5 * KV_TILE
    rows = pl.BlockSpec((1, tm * ROW_VECS, HEAD_DIM), lambda i, j: (layer, i, 0))
    stack_shape = jax.ShapeDtypeStruct((depth, m * ROW_VECS, HEAD_DIM), F32)
    return pl.pallas_call(
        functools.partial(_kv_proj_kernel, tm=tm),
        grid=(m // tm, n // KV_TILE),
        in_specs=[pl.BlockSpec((tm, k), lambda i, j: (i, 0)), pl.BlockSpec((k, KV_TILE), lambda i, j: (0, j)),
                  pl.BlockSpec(memory_space=pl.ANY), pl.BlockSpec(memory_space=pl.ANY)],
        out_specs=[pl.BlockSpec((tm, KV_TILE), lambda i, j: (i, j)), pl.BlockSpec((tm, KV_TILE), lambda i, j: (i, j)),
                   rows, rows],
        out_shape=[jax.ShapeDtypeStruct((m, n), F32), jax.ShapeDtypeStruct((m, n), BF16), stack_shape, stack_shape],
        input_output_aliases={2: 2, 3: 3},
        compiler_params=_cparams(("parallel", "arbitrary")),
        name="in_proj_kv",
    )(x, w, *stacked)


def _in_proj(xb, wl, layer, depth, stacked):
    wa, wb, wc, wd = wl
    (q,) = _matmul(xb, wa, (BF16,), "in_proj_q")
    kvf, kvb, nsa, sb = _kv_proj(xb, wb, layer, depth, stacked)
    (zc,) = _matmul(xb, wc, (F32,), "in_proj_z")
    (gate,) = _matmul(xb, wd, (F32,), "in_proj_gate")
    return q, kvf, kvb, zc, gate, (nsa, sb)


def kernel(x_prompt, x_sample, cache_nsa, cache_sb, state_win, state_ssm, page_table, rel_bias, w_in, w_cmp,
           ssm_lam_re, ssm_lam_im, ssm_log_dt, ssm_b_re, ssm_b_im, ssm_c_re, ssm_c_im, ssm_d, w_glu, mix_gain,
           w_out, ln_g, ln_b):
    depth = w_in.shape[0]
    b, t, d = x_prompt.shape
    db, ts, _ = x_sample.shape
    n_phys, page = cache_nsa.shape[1], cache_nsa.shape[2]
    n_pages = page_table.shape[1]
    past_len = n_pages * page
    nw = state_win.shape[2]
    alpha = (2 * depth) ** 0.25
    assert t % TQ == 0 and t % TQ_SB == 0 and TQ_SB % SB_ROW_CHUNK == 0 and SB_ROW_CHUNK % TK == 0 and WINDOW == 2 * TK and TQ == TK and n_pages % PAGES_PER_STEP == 0
    assert past_len % SEL_BLOCK == 0 and ts % SUBLANES == 0 and ts <= SEL_BLOCK and nw == WINDOW

    w_groups = _split_w_in(w_in)
    w_glu_b, w_out_b = w_glu.astype(BF16), w_out.astype(BF16)
    ssm_w = jax.vmap(_ssm_weights)(ssm_lam_re, ssm_lam_im, ssm_log_dt, ssm_b_re, ssm_b_im, ssm_c_re, ssm_c_im, ssm_d)
    tables_p, bias_cmp_p = _prompt_tables(rel_bias, t)
    nw_pad = -(-(nw + ts) // LANES) * LANES
    bias_cmp_s, bias_win_s, bias_steps_s, bias_new_s = _sample_tables(rel_bias, ts, past_len, page, nw, nw_pad)
    cache_nsa_r = cache_nsa.reshape(depth, n_phys, page * ROW_VECS, HEAD_DIM)
    cache_sb_r = cache_sb.reshape(depth, n_phys, page * ROW_VECS, HEAD_DIM)
    w_cmp = w_cmp.astype(F32)
    page_table = page_table.astype(jnp.int32)

    xp, xs = x_prompt.reshape(b * t, d), x_sample.reshape(db * ts, d)
    xpb, xsb = xp.astype(BF16), xs.astype(BF16)
    outs = [[] for _ in range(8)]
    rows_p = (jnp.zeros((depth, b * t * ROW_VECS, HEAD_DIM), F32),) * 2
    rows_s = (jnp.zeros((depth, db * ts * ROW_VECS, HEAD_DIM), F32),) * 2
    for l in range(depth):
        wl = tuple(w[l] for w in w_groups)
        sw = tuple(w[l] for w in ssm_w)
        gain, lng, lnb = mix_gain[l].reshape(1, d), ln_g[l].reshape(1, d), ln_b[l].reshape(1, d)

        q, kvf, kvb, zc, gate, rows_p = _in_proj(xpb, wl, l, depth, rows_p)
        o_cmp, sel = _prompt_cmp(q, kvf, w_cmp[l], bias_cmp_p, b, t)
        o_sel, o_win = _prompt_selwin(q, kvb, sel, tables_p, b, t)
        o_b = _prompt_sb(q, kvb, b, t)
        y_c, h_p = _ssm(zc, jnp.zeros((b, 2, SSM_LANES), F32), sw, b, t)
        xp, xpb = _mix_out(o_cmp, o_sel, o_win, gate, o_b, y_c, zc, xp, w_glu_b[l], w_out_b[l], gain, lng, lnb,
                           alpha)
        kv3 = kvf.reshape(b, t, -1)
        outs[4].append(kv3[:, max(t - WINDOW, 0):, 1024:1536].reshape(b, -1, 2, NSA_KV_HEADS, HEAD_DIM))
        outs[6].append(h_p.reshape(b, 2, SSM_GROUPS, SSM_STATE))

        q, kvf, kvb, zc, gate, rows_s = _in_proj(xsb, wl, l, depth, rows_s)
        kv3 = kvf.reshape(db, ts, -1)
        win_all = jnp.concatenate([state_win[l].reshape(db, nw, -1), kv3[:, :, 1024:1536],
                                   jnp.zeros((db, nw_pad - nw - ts, 4 * HEAD_DIM), F32)], axis=1)
        first, second = _sample_compress(cache_nsa_r, l, page_table, w_cmp[l])
        o_cmp, sel, o_win = _sample_cmpwin(q, first, second, bias_cmp_s, win_all, bias_win_s, db, ts, past_len)
        steps = n_pages // PAGES_PER_STEP
        sel_steps = jnp.moveaxis(sel.reshape(db, NSA_KV_HEADS, ts, steps, -1), 3, 2)
        o_sel = _sample_sel(cache_nsa_r, l, page_table, q, sel_steps, kvf, bias_steps_s, bias_new_s, db, ts)
        o_b = _sample_sb(cache_sb_r, l, page_table, q, kvf, db, ts)
        y_c, h_s = _ssm(zc, state_ssm[l].reshape(db, 2, SSM_LANES).astype(F32), sw, db, ts)
        xs, xsb = _mix_out(o_cmp, o_sel, o_win, gate, o_b, y_c, zc, xs, w_glu_b[l], w_out_b[l], gain, lng, lnb,
                           alpha)
        outs[5].append(win_all[:, ts:ts + nw].reshape(db, nw, 2, NSA_KV_HEADS, HEAD_DIM))
        outs[7].append(h_s.reshape(db, 2, SSM_GROUPS, SSM_STATE))

    win_p, win_s, ssm_p, ssm_s = (jnp.stack(outs[i]) for i in (4, 5, 6, 7))
    return (xp.reshape(b, t, d), xs.reshape(db, ts, d),
            rows_p[0].reshape(depth, b, t, 4, NSA_KV_HEADS, HEAD_DIM),
            rows_s[0].reshape(depth, db, ts, 4, NSA_KV_HEADS, HEAD_DIM),
            rows_p[1].reshape(depth, b, t, 2, SB_HEADS, HEAD_DIM),
            rows_s[1].reshape(depth, db, ts, 2, SB_HEADS, HEAD_DIM),
            win_p, win_s, ssm_p, ssm_s)
```
